```python
import math
import jax, jax.numpy as jnp
from jax import lax
import numpy as np

D_MODEL = 2048
BATCH = 4
SEQ = 2048
DEPTH = 1
DEC_BATCH = 128
DEC_SEQ = 1
PAST_LEN = 16384
PAGE_SIZE = 128

GLA_HEADS = 4
GLA_DK = (D_MODEL // 2) // GLA_HEADS
GLA_DV = D_MODEL // GLA_HEADS
GLA_RANK = 16
GLA_TAU = 16
GLA_CHUNK = 64
CONV_DIM = D_MODEL
CONV_WIDTH = 3
PEER_HEADS = 8
N_KEYS = 128
N_EXPERTS = N_KEYS * N_KEYS
PEER_TOPK = 16
DQ_HALF = 128
PEER_BLOCK = 128
EPS = 1e-6

W_IN_WIDTHS = (GLA_HEADS * GLA_DK, GLA_HEADS * GLA_DK, GLA_HEADS * GLA_DV, GLA_HEADS * GLA_DV,
               GLA_RANK, CONV_DIM, CONV_DIM, CONV_DIM, D_MODEL, D_MODEL)
D_IN = sum(W_IN_WIDTHS)

kernel_name = "gla_shortconv_peer_hybrid_step"


def rmsnorm(x, g):
    xf = x.astype(jnp.float32)
    xf = xf * lax.rsqrt(jnp.mean(xf * xf, axis=-1, keepdims=True) + EPS)
    return (xf * g.astype(jnp.float32)).astype(x.dtype)


def gla_chunked(q, k, v, loga, s0):
    B, L, H, DK = q.shape
    DV = v.shape[-1]
    C = math.gcd(L, GLA_CHUNK)
    n = L // C

    def to_chunks(t):
        return t.astype(jnp.float32).reshape(B, n, C, H, t.shape[-1]).transpose(1, 0, 3, 2, 4)

    qs, ks, vs, gs = to_chunks(q), to_chunks(k), to_chunks(v), to_chunks(loga)
    causal = jnp.tril(jnp.ones((C, C), dtype=bool))

    def step(S, inp):
        qc, kc, vc, gc = inp
        b = jnp.cumsum(gc, axis=2)
        diff = b[:, :, :, None, :] - b[:, :, None, :, :]
        decay = jnp.exp(jnp.where(causal[None, None, :, :, None], diff, -jnp.inf))
        attn = jnp.einsum('bhid,bhjd,bhijd->bhij', qc, kc, decay)
        o = (jnp.einsum('bhij,bhjv->bhiv', attn, vc)
             + jnp.einsum('bhid,bhdv->bhiv', qc * jnp.exp(b), S))
        bl = b[:, :, -1:, :]
        S_new = (jnp.exp(bl[:, :, 0, :])[..., None] * S
                 + jnp.einsum('bhjd,bhjv->bhdv', kc * jnp.exp(bl - b), vc))
        return S_new, o

    S_fin, o = lax.scan(step, s0.astype(jnp.float32), (qs, ks, vs, gs))
    o = o.transpose(1, 0, 3, 2, 4).reshape(B, L, H, DV)
    return o.astype(q.dtype), S_fin.astype(s0.dtype)


def peer(h, w_query, sub_keys, expert_u, expert_v):
    B, L, D = h.shape
    T = B * L
    ht = h.reshape(T, D)
    q = (ht @ w_query).reshape(T, PEER_HEADS, 2, DQ_HALF)
    s1 = jnp.einsum('thd,nd->thn', q[:, :, 0], sub_keys[0]).astype(jnp.float32)
    s2 = jnp.einsum('thd,nd->thn', q[:, :, 1], sub_keys[1]).astype(jnp.float32)
    v1, i1 = lax.top_k(s1, PEER_TOPK)
    v2, i2 = lax.top_k(s2, PEER_TOPK)
    cand = (v1[..., :, None] + v2[..., None, :]).reshape(T, PEER_HEADS, PEER_TOPK * PEER_TOPK)
    cidx = (i1[..., :, None] * N_KEYS + i2[..., None, :]).reshape(T, PEER_HEADS, PEER_TOPK * PEER_TOPK)
    top, pos = lax.top_k(cand, PEER_TOPK)
    idx = jnp.take_along_axis(cidx, pos, axis=-1)
    g = jax.nn.softmax(top, axis=-1).astype(h.dtype)

    blk = math.gcd(T, PEER_BLOCK)
    nb = T // blk

    def one_block(args):
        xb, ib, gb = args
        u = jnp.take(expert_u, ib, axis=0)
        a = jnp.einsum('thkd,td->thk', u, xb)
        w = gb * jax.nn.gelu(a, approximate=False)
        vg = jnp.take(expert_v, ib, axis=0)
        return jnp.einsum('thk,thkd->td', w, vg)

    out = lax.map(one_block, (ht.reshape(nb, blk, D),
                              idx.reshape(nb, blk, PEER_HEADS, PEER_TOPK),
                              g.reshape(nb, blk, PEER_HEADS, PEER_TOPK)))
    return out.reshape(B, L, D)


def layer(x, s_gla, s_conv, norm1_g, w_in, w_alpha_up, b_alpha, gla_norm_g, conv_w, conv_b,
          w_out, norm2_g, w_query, sub_keys, expert_u, expert_v):
    B, L, _ = x.shape
    h = rmsnorm(x, norm1_g)
    p = h @ w_in
    points = []
    acc = 0
    for w in W_IN_WIDTHS[:-1]:
        acc += w
        points.append(acc)
    q, k, v, r, alr, cb, cc, cx, ga, gb = jnp.split(p, points, axis=-1)

    loga = jax.nn.log_sigmoid((alr @ w_alpha_up + b_alpha).astype(jnp.float32)) / GLA_TAU
    q = q.reshape(B, L, GLA_HEADS, GLA_DK) * (GLA_DK ** -0.5)
    k = k.reshape(B, L, GLA_HEADS, GLA_DK)
    v = v.reshape(B, L, GLA_HEADS, GLA_DV)
    loga = loga.reshape(B, L, GLA_HEADS, GLA_DK)
    o, s_gla_new = gla_chunked(q, k, v, loga, s_gla)
    o = rmsnorm(o, gla_norm_g.reshape(GLA_HEADS, GLA_DV))
    y_a = jax.nn.silu(r) * o.reshape(B, L, GLA_HEADS * GLA_DV)

    u = cc * cx
    buf = jnp.concatenate([s_conv.astype(u.dtype), u], axis=1)
    z = conv_b
    for j in range(CONV_WIDTH):
        z = z + buf[:, j:j + L] * conv_w[j]
    y_b = cb * z
    s_conv_new = buf[:, L:]

    m = jax.nn.sigmoid(ga) * y_a + jax.nn.sigmoid(gb) * y_b
    x = x + m @ w_out

    x = x + peer(rmsnorm(x, norm2_g), w_query, sub_keys, expert_u, expert_v)
    return x, s_gla_new, s_conv_new.astype(s_conv.dtype)


def setup_inputs(seed: int = 0) -> dict:
    key = jax.random.key(seed)
    ks = jax.random.split(key, 20)
    nrm = jax.random.normal
    f32 = jnp.float32
    return {
        "x_prompt": nrm(ks[0], (BATCH, SEQ, D_MODEL), f32),
        "x_sample": nrm(ks[1], (DEC_BATCH, DEC_SEQ, D_MODEL), f32),
        "state_gla": 0.5 * nrm(ks[2], (DEPTH, DEC_BATCH, GLA_HEADS, GLA_DK, GLA_DV), f32),
        "state_conv": nrm(ks[3], (DEPTH, DEC_BATCH, CONV_WIDTH - 1, CONV_DIM), f32),
        "norm1_g": 1.0 + 0.02 * nrm(ks[4], (DEPTH, D_MODEL), f32),
        "w_in": nrm(ks[5], (DEPTH, D_MODEL, D_IN), f32) * D_MODEL ** -0.5,
        "w_alpha_up": nrm(ks[6], (DEPTH, GLA_RANK, GLA_HEADS * GLA_DK), f32) * GLA_RANK ** -0.5,
        "b_alpha": 1.0 + 0.1 * nrm(ks[7], (DEPTH, GLA_HEADS * GLA_DK), f32),
        "gla_norm_g": 1.0 + 0.02 * nrm(ks[8], (DEPTH, GLA_HEADS * GLA_DV), f32),
        "conv_w": nrm(ks[9], (DEPTH, CONV_WIDTH, CONV_DIM), f32) * CONV_WIDTH ** -0.5,
        "conv_b": 0.02 * nrm(ks[10], (DEPTH, CONV_DIM), f32),
        "w_out": nrm(ks[11], (DEPTH, D_MODEL, D_MODEL), f32) * D_MODEL ** -0.5,
        "norm2_g": 1.0 + 0.02 * nrm(ks[12], (DEPTH, D_MODEL), f32),
        "w_query": nrm(ks[13], (DEPTH, D_MODEL, PEER_HEADS * 2 * DQ_HALF), f32) * D_MODEL ** -0.5,
        "sub_keys": nrm(ks[14], (DEPTH, 2, N_KEYS, DQ_HALF), f32) * DQ_HALF ** -0.5,
        "expert_u": nrm(ks[15], (DEPTH, N_EXPERTS, D_MODEL), f32) * D_MODEL ** -0.5,
        "expert_v": 0.1 * nrm(ks[16], (DEPTH, N_EXPERTS, D_MODEL), f32),
        "norm_f_g": 1.0 + 0.02 * nrm(ks[17], (D_MODEL,), f32),
    }


def reference(x_prompt, x_sample, state_gla, state_conv, norm1_g, w_in, w_alpha_up, b_alpha,
              gla_norm_g, conv_w, conv_b, w_out, norm2_g, w_query, sub_keys, expert_u, expert_v,
              norm_f_g):
    xp, xs = x_prompt, x_sample
    zero_gla = jnp.zeros((BATCH, GLA_HEADS, GLA_DK, GLA_DV), state_gla.dtype)
    zero_conv = jnp.zeros((BATCH, CONV_WIDTH - 1, CONV_DIM), state_conv.dtype)
    gla_p, conv_p, gla_s, conv_s = [], [], [], []
    for l in range(DEPTH):
        wts = (norm1_g[l], w_in[l], w_alpha_up[l], b_alpha[l], gla_norm_g[l], conv_w[l], conv_b[l],
               w_out[l], norm2_g[l], w_query[l], sub_keys[l], expert_u[l], expert_v[l])
        xp, sg, sc = layer(xp, zero_gla, zero_conv, *wts)
        gla_p.append(sg)
        conv_p.append(sc)
        xs, sg, sc = layer(xs, state_gla[l], state_conv[l], *wts)
        gla_s.append(sg)
        conv_s.append(sc)
    y_prompt = rmsnorm(xp, norm_f_g)
    y_sample = rmsnorm(xs, norm_f_g)
    return (y_prompt, y_sample, jnp.stack(gla_p), jnp.stack(conv_p), jnp.stack(gla_s), jnp.stack(conv_s))
```

```python
import functools
import math

import jax
import jax.numpy as jnp
from jax import lax
from jax.experimental import pallas as pl
from jax.experimental.pallas import tpu as pltpu

F32 = jnp.float32
BF16 = jnp.bfloat16

D_MODEL = 2048
GLA_HEADS = 4
GLA_DK = 256
GLA_DV = 512
GLA_RANK = 16
GLA_TAU = 16.0
GLA_CHUNK = 64
CONV_WIDTH = 3
PEER_HEADS = 8
N_KEYS = 128
N_EXPERTS = N_KEYS * N_KEYS
PEER_TOPK = 16
DQ_HALF = 128
EPS = 1e-6

LANES = 128
SUBLANES = 8
VMEM_LIMIT_BYTES = 60000 * 1024

N_MAIN = 16384
GLA_SUB = 16
NEG_INF = float("-inf")
INV_SQRT2 = 1.0 / math.sqrt(2.0)


def _params(semantics):
    return pltpu.CompilerParams(dimension_semantics=semantics,
                                vmem_limit_bytes=VMEM_LIMIT_BYTES)


def _rms(x, g):
    return x * lax.rsqrt(jnp.mean(x * x, axis=-1, keepdims=True) + EPS) * g


def _inproj_kernel(x_ref, g_ref, w_ref, walr_ref, p_ref, alr_ref, h_ref):
    @pl.when(pl.program_id(1) == 0)
    def _():
        hb = _rms(x_ref[...], g_ref[...]).astype(BF16)
        h_ref[...] = hb
        alr_ref[...] = jnp.dot(hb, walr_ref[...], preferred_element_type=F32)

    p_ref[...] = jnp.dot(h_ref[...], w_ref[...], preferred_element_type=F32)


def _inproj(x, g, w_main, w_alr, tm, tn):
    t = x.shape[0]
    return pl.pallas_call(
        _inproj_kernel,
        grid=(t // tm, N_MAIN // tn),
        in_specs=[
            pl.BlockSpec((tm, D_MODEL), lambda i, j: (i, 0)),
            pl.BlockSpec((1, D_MODEL), lambda i, j: (0, 0)),
            pl.BlockSpec((D_MODEL, tn), lambda i, j: (0, j)),
            pl.BlockSpec((D_MODEL, LANES), lambda i, j: (0, 0)),
        ],
        out_specs=[
            pl.BlockSpec((tm, tn), lambda i, j: (i, j)),
            pl.BlockSpec((tm, LANES), lambda i, j: (i, 0)),
        ],
        out_shape=[
            jax.ShapeDtypeStruct((t, N_MAIN), F32),
            jax.ShapeDtypeStruct((t, LANES), F32),
        ],
        scratch_shapes=[pltpu.VMEM((tm, D_MODEL), BF16)],
        compiler_params=_params(("arbitrary", "arbitrary")),
        name="inproj",
    )(x, g, w_main, w_alr)


def _log_decay(alr, wup, ba):
    xa = jnp.dot(alr.astype(BF16), wup, preferred_element_type=F32) + ba
    return -(jnp.maximum(-xa, 0.0) + jnp.log1p(jnp.exp(-jnp.abs(xa)))) * (1.0 / GLA_TAU)


def _col_scale(row, width):
    col = jnp.transpose(jnp.broadcast_to(row, (LANES, row.shape[1])))
    return jnp.concatenate([col] * (width // LANES), axis=1)


def _gla_prompt_kernel(q_ref, k_ref, v_ref, alr_ref, wup_ref, ba_ref,
                       o_ref, sout_ref, s_ref, b_ref, *, rows):
    c = pl.program_id(2)

    @pl.when(c == 0)
    def _():
        s_ref[...] = jnp.zeros_like(s_ref)

    g = _log_decay(alr_ref[...], wup_ref[...], ba_ref[...])
    rmod = lax.broadcasted_iota(jnp.int32, (rows, GLA_DK), 0) & (GLA_CHUNK - 1)
    step = 1
    while step < GLA_CHUNK:
        g = g + jnp.where(rmod >= step, pltpu.roll(g, step, 0), 0.0)
        step *= 2
    b_ref[...] = g

    row = lax.broadcasted_iota(jnp.int32, (GLA_CHUNK, GLA_DK), 0)
    rsub = row & (GLA_SUB - 1)
    ri = lax.broadcasted_iota(jnp.int32, (GLA_CHUNK, GLA_CHUNK), 0)
    cj = lax.broadcasted_iota(jnp.int32, (GLA_CHUNK, GLA_CHUNK), 1)
    n_sub = GLA_CHUNK // GLA_SUB

    def chunk(ci, carry):
        r0 = pl.multiple_of(ci * GLA_CHUNK, GLA_CHUNK)
        qc = q_ref[pl.ds(r0, GLA_CHUNK), :] * (GLA_DK ** -0.5)
        kc = k_ref[pl.ds(r0, GLA_CHUNK), :]
        vc = v_ref[pl.ds(r0, GLA_CHUNK), :]
        b = b_ref[pl.ds(r0, GLA_CHUNK), :]
        bl = b[GLA_CHUNK - 1:GLA_CHUNK, :]
        s = s_ref[...]
        vb = vc.astype(BF16)

        o = jnp.dot((qc * jnp.exp(b)).astype(BF16), s.astype(BF16),
                    preferred_element_type=F32)

        parts = [jnp.zeros((GLA_SUB, GLA_CHUNK), F32)]
        for blk in range(1, n_sub):
            lo = blk * GLA_SUB
            ref = b[lo - 1:lo, :]
            qs = qc[lo:lo + GLA_SUB, :] * jnp.exp(b[lo:lo + GLA_SUB, :] - ref)
            ks = kc * jnp.exp(jnp.where(row < lo, ref - b, NEG_INF))
            parts.append(lax.dot_general(qs.astype(BF16), ks.astype(BF16),
                                         (((1,), (1,)), ((), ())),
                                         preferred_element_type=F32))
        attn = jnp.concatenate(parts, axis=0)

        for d in range(GLA_SUB):
            if d == 0:
                t = qc * kc
            else:
                kd = pltpu.roll(kc, d, 0)
                bd = pltpu.roll(b, d, 0)
                t = qc * kd * jnp.exp(jnp.where(rsub >= d, b - bd, NEG_INF))
            cs = jnp.sum(t, axis=1, keepdims=True)
            attn = attn + jnp.where(ri - cj == d, cs, 0.0)

        o = o + jnp.dot(attn.astype(BF16), vb, preferred_element_type=F32)
        o_ref[pl.ds(r0, GLA_CHUNK), :] = o

        kdec = (kc * jnp.exp(bl - b)).astype(BF16)
        s_ref[...] = (_col_scale(jnp.exp(bl), GLA_DV) * s
                      + lax.dot_general(kdec, vb, (((0,), (0,)), ((), ())),
                                        preferred_element_type=F32))
        return carry

    lax.fori_loop(0, rows // GLA_CHUNK, chunk, 0)

    @pl.when(c == pl.num_programs(2) - 1)
    def _():
        sout_ref[0, 0] = s_ref[...]


def _gla_prompt(p, alr, wup, ba, batch, seq, rows):
    nsteps = seq // rows
    kq = GLA_HEADS * GLA_DK // GLA_DK
    kv = 2 * GLA_HEADS * GLA_DK // GLA_DV

    def rowmap(b, h, c):
        return b * nsteps + c

    return pl.pallas_call(
        functools.partial(_gla_prompt_kernel, rows=rows),
        grid=(batch, GLA_HEADS, nsteps),
        in_specs=[
            pl.BlockSpec((rows, GLA_DK), lambda b, h, c: (rowmap(b, h, c), h)),
            pl.BlockSpec((rows, GLA_DK), lambda b, h, c: (rowmap(b, h, c), kq + h)),
            pl.BlockSpec((rows, GLA_DV), lambda b, h, c: (rowmap(b, h, c), kv + h)),
            pl.BlockSpec((rows, LANES), lambda b, h, c: (rowmap(b, h, c), 0)),
            pl.BlockSpec((LANES, GLA_DK), lambda b, h, c: (0, h)),
            pl.BlockSpec((1, GLA_DK), lambda b, h, c: (0, h)),
        ],
        out_specs=[
            pl.BlockSpec((rows, GLA_DV), lambda b, h, c: (rowmap(b, h, c), h)),
            pl.BlockSpec((1, 1, GLA_DK, GLA_DV), lambda b, h, c: (b, h, 0, 0)),
        ],
        out_shape=[
            jax.ShapeDtypeStruct((batch * seq, GLA_HEADS * GLA_DV), F32),
            jax.ShapeDtypeStruct((batch, GLA_HEADS, GLA_DK, GLA_DV), F32),
        ],
        scratch_shapes=[pltpu.VMEM((GLA_DK, GLA_DV), F32),
                        pltpu.VMEM((rows, GLA_DK), F32)],
        compiler_params=_params(("arbitrary", "arbitrary", "arbitrary")),
        name="gla_prompt",
    )(p, p, p, alr, wup, ba)


def _gla_sample_kernel(q_ref, k_ref, v_ref, alr_ref, wup_ref, ba_ref, s0_ref,
                       o_ref, s1_ref):
    g_all = _log_decay(jnp.broadcast_to(alr_ref[0], (SUBLANES, LANES)),
                       wup_ref[...], ba_ref[...])
    first = lax.broadcasted_iota(jnp.int32, (SUBLANES, GLA_DK), 0) == 0
    for h in range(GLA_HEADS):
        dk = slice(h * GLA_DK, (h + 1) * GLA_DK)
        dv = slice(h * GLA_DV, (h + 1) * GLA_DV)
        q = jnp.broadcast_to(q_ref[0, :, dk], (SUBLANES, GLA_DK)) * (GLA_DK ** -0.5)
        k = jnp.broadcast_to(k_ref[0, :, dk], (SUBLANES, GLA_DK))
        v = jnp.broadcast_to(v_ref[0, :, dv], (SUBLANES, GLA_DV))
        g = g_all[:, dk]
        s = s0_ref[0, h]
        attn = jnp.sum(q * k, axis=1, keepdims=True)
        o = attn * v + jnp.dot((q * jnp.exp(g)).astype(BF16), s.astype(BF16),
                               preferred_element_type=F32)
        o_ref[0, :, dv] = o[0:1, :]
        k1 = jnp.where(first, k, 0.0).astype(BF16)
        s1_ref[0, h] = (_col_scale(jnp.exp(g[0:1, :]), GLA_DV) * s
                        + lax.dot_general(k1, v.astype(BF16), (((0,), (0,)), ((), ())),
                                          preferred_element_type=F32))


def _gla_sample(p3, alr3, wup, ba, s0):
    n = p3.shape[0]
    qk = GLA_HEADS * GLA_DK
    vw = GLA_HEADS * GLA_DV
    return pl.pallas_call(
        _gla_sample_kernel,
        grid=(n,),
        in_specs=[
            pl.BlockSpec((1, 1, qk), lambda b: (b, 0, 0)),
            pl.BlockSpec((1, 1, qk), lambda b: (b, 0, 1)),
            pl.BlockSpec((1, 1, vw), lambda b: (b, 0, 2 * qk // vw)),
            pl.BlockSpec((1, 1, LANES), lambda b: (b, 0, 0)),
            pl.BlockSpec((LANES, qk), lambda b: (0, 0)),
            pl.BlockSpec((1, qk), lambda b: (0, 0)),
            pl.BlockSpec((1, GLA_HEADS, GLA_DK, GLA_DV), lambda b: (b, 0, 0, 0)),
        ],
        out_specs=[
            pl.BlockSpec((1, 1, vw), lambda b: (b, 0, 0)),
            pl.BlockSpec((1, GLA_HEADS, GLA_DK, GLA_DV), lambda b: (b, 0, 0, 0)),
        ],
        out_shape=[
            jax.ShapeDtypeStruct((n, 1, vw), F32),
            jax.ShapeDtypeStruct((n, GLA_HEADS, GLA_DK, GLA_DV), F32),
        ],
        compiler_params=_params(("arbitrary",)),
        name="gla_sample",
    )(p3, p3, p3, alr3, wup, ba, s0)


def _merge_kernel(o_ref, r_ref, cb_ref, cc_ref, cx_ref, ga_ref, gb_ref, pa_ref, pb_ref,
                  gg_ref, cw_ref, cbias_ref, m_ref, u_ref, *, prompt, tiles_per_seq):
    u = cc_ref[...] * cx_ref[...]
    tm = u.shape[0]
    if prompt:
        start = (pl.program_id(0) % tiles_per_seq) == 0
        prev = jnp.where(start, 0.0, pa_ref[...] * pb_ref[...])
        row = lax.broadcasted_iota(jnp.int32, (tm, D_MODEL), 0)
        um1 = jnp.where(row == 0, prev[7:8, :], pltpu.roll(u, 1, 0))
        um2 = jnp.where(row == 0, prev[6:7, :],
                        jnp.where(row == 1, prev[7:8, :], pltpu.roll(u, 2, 0)))
        u_ref[0] = u[tm - SUBLANES:, :]
    else:
        um2 = pa_ref[...]
        um1 = pb_ref[...]
        u_ref[...] = u
    z = cbias_ref[...] + um2 * cw_ref[0:1, :] + um1 * cw_ref[1:2, :] + u * cw_ref[2:3, :]
    y_b = cb_ref[...] * z
    r = r_ref[...]
    sr = r * jax.nn.sigmoid(r)
    m = jax.nn.sigmoid(gb_ref[...]) * y_b
    sga = jax.nn.sigmoid(ga_ref[...])
    for h in range(GLA_HEADS):
        dv = slice(h * GLA_DV, (h + 1) * GLA_DV)
        on = _rms(o_ref[:, dv], gg_ref[:, dv])
        m_ref[:, dv] = (m[:, dv] + sga[:, dv] * (sr[:, dv] * on)).astype(BF16)


def _merge(o, p, pa, pb, gg, cw, cbias, tm, prompt, seq):
    t = o.shape[0]
    n_tiles = t // tm
    col = lambda c: pl.BlockSpec((tm, D_MODEL), lambda i: (i, c))
    if prompt:
        prev_rows = tm // SUBLANES
        prev = lambda c: pl.BlockSpec(
            (SUBLANES, D_MODEL), lambda i: (jnp.maximum(i * prev_rows - 1, 0), c))
        pspecs = [prev(4), prev(5)]
        uspec = pl.BlockSpec((1, SUBLANES, D_MODEL), lambda i: (i, 0, 0))
        ushape = jax.ShapeDtypeStruct((n_tiles, SUBLANES, D_MODEL), F32)
    else:
        pspecs = [pl.BlockSpec((tm, D_MODEL), lambda i: (i, 0))] * 2
        uspec = pl.BlockSpec((tm, D_MODEL), lambda i: (i, 0))
        ushape = jax.ShapeDtypeStruct((t, D_MODEL), F32)
    vec = lambda r: pl.BlockSpec((r, D_MODEL), lambda i: (0, 0))
    return pl.pallas_call(
        functools.partial(_merge_kernel, prompt=prompt,
                          tiles_per_seq=max(seq // tm, 1)),
        grid=(n_tiles,),
        in_specs=[pl.BlockSpec((tm, D_MODEL), lambda i: (i, 0)),
                  col(2), col(3), col(4), col(5), col(6), col(7)] + pspecs
                 + [vec(1), vec(CONV_WIDTH), vec(1)],
        out_specs=[pl.BlockSpec((tm, D_MODEL), lambda i: (i, 0)), uspec],
        out_shape=[jax.ShapeDtypeStruct((t, D_MODEL), BF16), ushape],
        compiler_params=_params(("arbitrary",)),
        name="merge_prompt" if prompt else "merge_sample",
    )(o, p, p, p, p, p, p, pa, pb, gg, cw, cbias)


def _outproj_kernel(x_ref, m_ref, wo_ref, g2_ref, wq_ref, xm_ref, h2_ref, qp_ref):
    xm = x_ref[...] + jnp.dot(m_ref[...], wo_ref[...], preferred_element_type=F32)
    xm_ref[...] = xm
    h2 = _rms(xm, g2_ref[...]).astype(BF16)
    h2_ref[...] = h2
    qp_ref[...] = jnp.dot(h2, wq_ref[...], preferred_element_type=F32)


def _outproj(x, m, wo, g2, wq, tm):
    t = x.shape[0]
    tile = pl.BlockSpec((tm, D_MODEL), lambda i: (i, 0))
    whole = pl.BlockSpec((D_MODEL, D_MODEL), lambda i: (0, 0))
    return pl.pallas_call(
        _outproj_kernel,
        grid=(t // tm,),
        in_specs=[tile, tile, whole, pl.BlockSpec((1, D_MODEL), lambda i: (0, 0)), whole],
        out_specs=[tile, tile, tile],
        out_shape=[jax.ShapeDtypeStruct((t, D_MODEL), F32),
                   jax.ShapeDtypeStruct((t, D_MODEL), BF16),
                   jax.ShapeDtypeStruct((t, D_MODEL), F32)],
        compiler_params=_params(("arbitrary",)),
        name="outproj",
    )(x, m, wo, g2, wq)


def _top_values(work, n):
    vals = []
    for _ in range(n):
        mx = jnp.max(work, axis=0, keepdims=True)
        vals.append(mx)
        work = jnp.where(work == mx, NEG_INF, work)
    return vals


def _route_kernel(qp_ref, keys_ref, s1_ref, s2_ref, p1_ref, p2_ref, thr_ref):
    for h in range(PEER_HEADS):
        st = []
        for side in range(2):
            c0 = (2 * h + side) * DQ_HALF
            st.append(lax.dot_general(keys_ref[side], qp_ref[:, c0:c0 + DQ_HALF],
                                      (((1,), (1,)), ((), ())),
                                      precision=lax.Precision.HIGHEST,
                                      preferred_element_type=F32))
        v1 = _top_values(st[0], PEER_TOPK)
        v2 = jnp.concatenate(_top_values(st[1], PEER_TOPK), axis=0)
        cand = jnp.concatenate([a + v2 for a in v1], axis=0)
        thr = _top_values(cand, PEER_TOPK)[-1]
        m1 = v1[0]
        m2 = v2[0:1, :]
        z = jnp.sum(jnp.where(cand >= thr, jnp.exp(cand - (m1 + m2)), 0.0),
                    axis=0, keepdims=True)
        s1_ref[h] = st[0]
        s2_ref[h] = st[1]
        p1_ref[h] = jnp.exp(st[0] - m1)
        p2_ref[h] = jnp.exp(st[1] - m2) / z
        thr_ref[h:h + 1, :] = thr


def _route(qp, keys, tm):
    t = qp.shape[0]
    big = pl.BlockSpec((PEER_HEADS, N_KEYS, tm), lambda i: (0, 0, i))
    bshape = jax.ShapeDtypeStruct((PEER_HEADS, N_KEYS, t), F32)
    return pl.pallas_call(
        _route_kernel,
        grid=(t // tm,),
        in_specs=[pl.BlockSpec((tm, D_MODEL), lambda i: (i, 0)),
                  pl.BlockSpec((2, N_KEYS, DQ_HALF), lambda i: (0, 0, 0))],
        out_specs=[big, big, big, big, pl.BlockSpec((PEER_HEADS, tm), lambda i: (0, i))],
        out_shape=[bshape, bshape, bshape, bshape,
                   jax.ShapeDtypeStruct((PEER_HEADS, t), F32)],
        compiler_params=_params(("arbitrary",)),
        name="route",
    )(qp, keys)


def _peer_kernel(h2_ref, u_ref, vt_ref, s1_ref, s2_ref, p1_ref, p2_ref, thr_ref,
                 out_ref, acc_ref, *, te):
    j = pl.program_id(1)

    @pl.when(j == 0)
    def _():
        acc_ref[...] = jnp.zeros_like(acc_ref)

    tm = h2_ref.shape[0]
    gates = []
    for l in range(te // N_KEYS):
        e1 = j * (te // N_KEYS) + l
        g = jnp.zeros((N_KEYS, tm), F32)
        for h in range(PEER_HEADS):
            sel = (s1_ref[h, pl.ds(e1, 1), :] + s2_ref[h]) >= thr_ref[h:h + 1, :]
            g = g + jnp.where(sel, p2_ref[h], 0.0) * p1_ref[h, pl.ds(e1, 1), :]
        gates.append(g)
    gate = jnp.concatenate(gates, axis=0) if len(gates) > 1 else gates[0]

    a = lax.dot_general(u_ref[...], h2_ref[...], (((1,), (1,)), ((), ())),
                        preferred_element_type=F32)
    w = (0.5 * a * (1.0 + lax.erf(a * INV_SQRT2)) * gate).astype(BF16)
    acc_ref[...] += jnp.dot(vt_ref[...], w, preferred_element_type=F32)

    @pl.when(j == pl.num_programs(1) - 1)
    def _():
        out_ref[...] = jnp.transpose(acc_ref[...])


def _peer(h2, u_bf, vt_bf, s1, s2, p1, p2, thr, tm, te):
    t = h2.shape[0]
    big = pl.BlockSpec((PEER_HEADS, N_KEYS, tm), lambda i, j: (0, 0, i))
    return pl.pallas_call(
        functools.partial(_peer_kernel, te=te),
        grid=(t // tm, N_EXPERTS // te),
        in_specs=[pl.BlockSpec((tm, D_MODEL), lambda i, j: (i, 0)),
                  pl.BlockSpec((te, D_MODEL), lambda i, j: (j, 0)),
                  pl.BlockSpec((D_MODEL, te), lambda i, j: (0, j)),
                  big, big, big, big,
                  pl.BlockSpec((PEER_HEADS, tm), lambda i, j: (0, i))],
        out_specs=pl.BlockSpec((tm, D_MODEL), lambda i, j: (i, 0)),
        out_shape=jax.ShapeDtypeStruct((t, D_MODEL), F32),
        scratch_shapes=[pltpu.VMEM((D_MODEL, tm), F32)],
        compiler_params=_params(("arbitrary", "arbitrary")),
        name="peer",
    )(h2, u_bf, vt_bf, s1, s2, p1, p2, thr)


def _final_kernel(xm_ref, po_ref, g_ref, y_ref):
    y_ref[...] = _rms(xm_ref[...] + po_ref[...], g_ref[...])


def _final(xm, po, g, tm):
    t = xm.shape[0]
    tile = pl.BlockSpec((tm, D_MODEL), lambda i: (i, 0))
    return pl.pallas_call(
        _final_kernel,
        grid=(t // tm,),
        in_specs=[tile, tile, pl.BlockSpec((1, D_MODEL), lambda i: (0, 0))],
        out_specs=tile,
        out_shape=jax.ShapeDtypeStruct((t, D_MODEL), F32),
        compiler_params=_params(("arbitrary",)),
        name="final",
    )(xm, po, g)


def _channel_mixer(x, m, wts, tm_proj, tm_route, tm_peer, te):
    xm, h2, qp = _outproj(x, m, wts["wo"], wts["g2"], wts["wq"], tm_proj)
    s1, s2, p1, p2, thr = _route(qp, wts["keys"], tm_route)
    po = _peer(h2, wts["u"], wts["vt"], s1, s2, p1, p2, thr, tm_peer, te)
    return _final(xm, po, wts["gf"], tm_proj)


def kernel(x_prompt, x_sample, state_gla, state_conv, norm1_g, w_in, w_alpha_up, b_alpha,
           gla_norm_g, conv_w, conv_b, w_out, norm2_g, w_query, sub_keys, expert_u, expert_v,
           norm_f_g):
    batch, seq, _ = x_prompt.shape
    n_dec = x_sample.shape[0]
    assert w_in.shape[0] == 1 and x_sample.shape[1] == 1

    alr0 = 2 * GLA_HEADS * GLA_DK + 2 * GLA_HEADS * GLA_DV
    w_main = jnp.concatenate([w_in[0][:, :alr0], w_in[0][:, alr0 + GLA_RANK:]],
                             axis=1).astype(BF16)
    w_alr = jnp.pad(w_in[0][:, alr0:alr0 + GLA_RANK],
                    ((0, 0), (0, LANES - GLA_RANK))).astype(BF16)
    wup = jnp.pad(w_alpha_up[0], ((0, LANES - GLA_RANK), (0, 0))).astype(BF16)
    ba = b_alpha[0][None, :]
    g1 = norm1_g[0][None, :]
    gg = gla_norm_g[0][None, :]
    cw = conv_w[0]
    cbias = conv_b[0][None, :]
    wts = dict(wo=w_out[0].astype(BF16), g2=norm2_g[0][None, :],
               wq=w_query[0].astype(BF16), keys=sub_keys[0],
               u=expert_u[0].astype(BF16), vt=jnp.transpose(expert_v[0]).astype(BF16),
               gf=norm_f_g[None, :])

    xp = x_prompt.reshape(batch * seq, D_MODEL)
    p, alr = _inproj(xp, g1, w_main, w_alr, tm=1024, tn=1024)
    o, s_p = _gla_prompt(p, alr, wup, ba, batch, seq, rows=256)
    tm_merge = 256
    m, utail = _merge(o, p, p, p, gg, cw, cbias, tm_merge, True, seq)
    y_p = _channel_mixer(xp, m, wts, tm_proj=256, tm_route=256, tm_peer=512, te=256)
    last = utail.reshape(batch, seq // tm_merge, SUBLANES, D_MODEL)[:, -1, SUBLANES - 2:, :]

    xs = x_sample.reshape(n_dec, D_MODEL)
    ps, alrs = _inproj(xs, g1, w_main, w_alr, tm=n_dec, tn=1024)
    os_, s_s = _gla_sample(ps.reshape(n_dec, 1, N_MAIN), alrs.reshape(n_dec, 1, LANES),
                           wup, ba, state_gla[0])
    ms, us = _merge(os_.reshape(n_dec, D_MODEL), ps, state_conv[0][:, 0, :],
                    state_conv[0][:, 1, :], gg, cw, cbias, n_dec, False, 1)
    y_s = _channel_mixer(xs, ms, wts, tm_proj=n_dec, tm_route=n_dec, tm_peer=n_dec, te=256)
    conv_s = jnp.stack([state_conv[0][:, 1, :], us], axis=1)

    return (y_p.reshape(batch, seq, D_MODEL),
            y_s.reshape(n_dec, 1, D_MODEL),
            s_p[None],
            last[None],
            s_s[None],
            conv_s[None])
```

```python
import functools
import math

import jax
import jax.numpy as jnp
from jax import lax
from jax.experimental import pallas as pl
from jax.experimental.pallas import tpu as pltpu

F32 = jnp.float32
BF16 = jnp.bfloat16

D_MODEL = 2048
GLA_HEADS = 4
GLA_DK = 256
GLA_DV = 512
GLA_RANK = 16
GLA_TAU = 16.0
GLA_CHUNK = 64
CONV_WIDTH = 3
PEER_HEADS = 8
N_KEYS = 128
N_EXPERTS = N_KEYS * N_KEYS
PEER_TOPK = 16
DQ_HALF = 128
EPS = 1e-6

LANES = 128
SUBLANES = 8
BF16_ROWS = 16
VMEM_LIMIT_BYTES = 60000 * 1024

N_MAIN = 16384
GLA_SUB = 16
PEER_PARTS = 2
NEG_INF = float("-inf")
INV_SQRT2 = 1.0 / math.sqrt(2.0)


def _params(semantics):
    return pltpu.CompilerParams(dimension_semantics=semantics,
                                vmem_limit_bytes=VMEM_LIMIT_BYTES)


def _rms(x, g):
    return x * lax.rsqrt(jnp.mean(x * x, axis=-1, keepdims=True) + EPS) * g


def _inproj_kernel(x_ref, g_ref, w_ref, walr_ref, p_ref, alr_ref, h_ref):
    @pl.when(pl.program_id(1) == 0)
    def _():
        hb = _rms(x_ref[...], g_ref[...]).astype(BF16)
        h_ref[...] = hb
        alr_ref[...] = jnp.dot(hb, walr_ref[...], preferred_element_type=F32)

    p_ref[...] = jnp.dot(h_ref[...], w_ref[...], preferred_element_type=F32)


def _inproj(x, g, w_main, w_alr, tm, tn):
    t = x.shape[0]
    return pl.pallas_call(
        _inproj_kernel,
        grid=(t // tm, N_MAIN // tn),
        in_specs=[
            pl.BlockSpec((tm, D_MODEL), lambda i, j: (i, 0)),
            pl.BlockSpec((1, D_MODEL), lambda i, j: (0, 0)),
            pl.BlockSpec((D_MODEL, tn), lambda i, j: (0, j)),
            pl.BlockSpec((D_MODEL, LANES), lambda i, j: (0, 0)),
        ],
        out_specs=[
            pl.BlockSpec((tm, tn), lambda i, j: (i, j)),
            pl.BlockSpec((tm, LANES), lambda i, j: (i, 0)),
        ],
        out_shape=[
            jax.ShapeDtypeStruct((t, N_MAIN), F32),
            jax.ShapeDtypeStruct((t, LANES), F32),
        ],
        scratch_shapes=[pltpu.VMEM((tm, D_MODEL), BF16)],
        compiler_params=_params(("arbitrary", "arbitrary")),
        name="inproj",
    )(x, g, w_main, w_alr)


def _log_decay(alr, wup, ba):
    xa = jnp.dot(alr.astype(BF16), wup, preferred_element_type=F32) + ba
    return -(jnp.maximum(-xa, 0.0) + jnp.log1p(jnp.exp(-jnp.abs(xa)))) * (1.0 / GLA_TAU)


def _col_scale(row, width):
    col = jnp.transpose(jnp.broadcast_to(row, (LANES, row.shape[1])))
    return jnp.concatenate([col] * (width // LANES), axis=1)


def _gla_prompt_kernel(q_ref, k_ref, v_ref, alr_ref, wup_ref, ba_ref,
                       o_ref, sout_ref, s_ref, b_ref, *, rows):
    c = pl.program_id(2)

    @pl.when(c == 0)
    def _():
        s_ref[...] = jnp.zeros_like(s_ref)

    g = _log_decay(alr_ref[...], wup_ref[...], ba_ref[...])
    rmod = lax.broadcasted_iota(jnp.int32, (rows, GLA_DK), 0) & (GLA_CHUNK - 1)
    step = 1
    while step < GLA_CHUNK:
        g = g + jnp.where(rmod >= step, pltpu.roll(g, step, 0), 0.0)
        step *= 2
    b_ref[...] = g

    row = lax.broadcasted_iota(jnp.int32, (GLA_CHUNK, GLA_DK), 0)
    rsub = row & (GLA_SUB - 1)
    ri = lax.broadcasted_iota(jnp.int32, (GLA_CHUNK, GLA_CHUNK), 0)
    cj = lax.broadcasted_iota(jnp.int32, (GLA_CHUNK, GLA_CHUNK), 1)
    n_sub = GLA_CHUNK // GLA_SUB

    def chunk(ci, carry):
        r0 = pl.multiple_of(ci * GLA_CHUNK, GLA_CHUNK)
        qc = q_ref[pl.ds(r0, GLA_CHUNK), :] * (GLA_DK ** -0.5)
        kc = k_ref[pl.ds(r0, GLA_CHUNK), :]
        vc = v_ref[pl.ds(r0, GLA_CHUNK), :]
        b = b_ref[pl.ds(r0, GLA_CHUNK), :]
        bl = b[GLA_CHUNK - 1:GLA_CHUNK, :]
        s = s_ref[...]
        vb = vc.astype(BF16)

        o = jnp.dot((qc * jnp.exp(b)).astype(BF16), s.astype(BF16),
                    preferred_element_type=F32)

        parts = [jnp.zeros((GLA_SUB, GLA_CHUNK), F32)]
        for blk in range(1, n_sub):
            lo = blk * GLA_SUB
            ref = b[lo - 1:lo, :]
            qs = qc[lo:lo + GLA_SUB, :] * jnp.exp(b[lo:lo + GLA_SUB, :] - ref)
            ks = kc * jnp.exp(jnp.where(row < lo, ref - b, NEG_INF))
            parts.append(lax.dot_general(qs.astype(BF16), ks.astype(BF16),
                                         (((1,), (1,)), ((), ())),
                                         preferred_element_type=F32))
        attn = jnp.concatenate(parts, axis=0)

        for d in range(GLA_SUB):
            if d == 0:
                t = qc * kc
            else:
                kd = pltpu.roll(kc, d, 0)
                bd = pltpu.roll(b, d, 0)
                t = qc * kd * jnp.exp(jnp.where(rsub >= d, b - bd, NEG_INF))
            cs = jnp.sum(t, axis=1, keepdims=True)
            attn = attn + jnp.where(ri - cj == d, cs, 0.0)

        o = o + jnp.dot(attn.astype(BF16), vb, preferred_element_type=F32)
        o_ref[pl.ds(r0, GLA_CHUNK), :] = o

        kdec = (kc * jnp.exp(bl - b)).astype(BF16)
        s_ref[...] = (_col_scale(jnp.exp(bl), GLA_DV) * s
                      + lax.dot_general(kdec, vb, (((0,), (0,)), ((), ())),
                                        preferred_element_type=F32))
        return carry

    lax.fori_loop(0, rows // GLA_CHUNK, chunk, 0)

    @pl.when(c == pl.num_programs(2) - 1)
    def _():
        sout_ref[0, 0] = s_ref[...]


def _gla_prompt(p, alr, wup, ba, batch, seq, rows):
    nsteps = seq // rows
    kq = GLA_HEADS * GLA_DK // GLA_DK
    kv = 2 * GLA_HEADS * GLA_DK // GLA_DV

    def rowmap(b, h, c):
        return b * nsteps + c

    return pl.pallas_call(
        functools.partial(_gla_prompt_kernel, rows=rows),
        grid=(batch, GLA_HEADS, nsteps),
        in_specs=[
            pl.BlockSpec((rows, GLA_DK), lambda b, h, c: (rowmap(b, h, c), h)),
            pl.BlockSpec((rows, GLA_DK), lambda b, h, c: (rowmap(b, h, c), kq + h)),
            pl.BlockSpec((rows, GLA_DV), lambda b, h, c: (rowmap(b, h, c), kv + h)),
            pl.BlockSpec((rows, LANES), lambda b, h, c: (rowmap(b, h, c), 0)),
            pl.BlockSpec((LANES, GLA_DK), lambda b, h, c: (0, h)),
            pl.BlockSpec((1, GLA_DK), lambda b, h, c: (0, h)),
        ],
        out_specs=[
            pl.BlockSpec((rows, GLA_DV), lambda b, h, c: (rowmap(b, h, c), h)),
            pl.BlockSpec((1, 1, GLA_DK, GLA_DV), lambda b, h, c: (b, h, 0, 0)),
        ],
        out_shape=[
            jax.ShapeDtypeStruct((batch * seq, GLA_HEADS * GLA_DV), F32),
            jax.ShapeDtypeStruct((batch, GLA_HEADS, GLA_DK, GLA_DV), F32),
        ],
        scratch_shapes=[pltpu.VMEM((GLA_DK, GLA_DV), F32),
                        pltpu.VMEM((rows, GLA_DK), F32)],
        compiler_params=_params(("arbitrary", "arbitrary", "arbitrary")),
        name="gla_prompt",
    )(p, p, p, alr, wup, ba)


def _gla_sample_kernel(q_ref, k_ref, v_ref, alr_ref, wup_ref, ba_ref, s0_ref,
                       o_ref, s1_ref):
    g_all = _log_decay(jnp.broadcast_to(alr_ref[0], (SUBLANES, LANES)),
                       wup_ref[...], ba_ref[...])
    first = lax.broadcasted_iota(jnp.int32, (SUBLANES, GLA_DK), 0) == 0
    for h in range(GLA_HEADS):
        dk = slice(h * GLA_DK, (h + 1) * GLA_DK)
        dv = slice(h * GLA_DV, (h + 1) * GLA_DV)
        q = jnp.broadcast_to(q_ref[0, :, dk], (SUBLANES, GLA_DK)) * (GLA_DK ** -0.5)
        k = jnp.broadcast_to(k_ref[0, :, dk], (SUBLANES, GLA_DK))
        v = jnp.broadcast_to(v_ref[0, :, dv], (SUBLANES, GLA_DV))
        g = g_all[:, dk]
        s = s0_ref[0, h]
        attn = jnp.sum(q * k, axis=1, keepdims=True)
        o = attn * v + jnp.dot((q * jnp.exp(g)).astype(BF16), s.astype(BF16),
                               preferred_element_type=F32)
        o_ref[0, :, dv] = o[0:1, :]
        k1 = jnp.where(first, k, 0.0).astype(BF16)
        s1_ref[0, h] = (_col_scale(jnp.exp(g[0:1, :]), GLA_DV) * s
                        + lax.dot_general(k1, v.astype(BF16), (((0,), (0,)), ((), ())),
                                          preferred_element_type=F32))


def _gla_sample(p3, alr3, wup, ba, s0):
    n = p3.shape[0]
    qk = GLA_HEADS * GLA_DK
    vw = GLA_HEADS * GLA_DV
    return pl.pallas_call(
        _gla_sample_kernel,
        grid=(n,),
        in_specs=[
            pl.BlockSpec((1, 1, qk), lambda b: (b, 0, 0)),
            pl.BlockSpec((1, 1, qk), lambda b: (b, 0, 1)),
            pl.BlockSpec((1, 1, vw), lambda b: (b, 0, 2 * qk // vw)),
            pl.BlockSpec((1, 1, LANES), lambda b: (b, 0, 0)),
            pl.BlockSpec((LANES, qk), lambda b: (0, 0)),
            pl.BlockSpec((1, qk), lambda b: (0, 0)),
            pl.BlockSpec((1, GLA_HEADS, GLA_DK, GLA_DV), lambda b: (b, 0, 0, 0)),
        ],
        out_specs=[
            pl.BlockSpec((1, 1, vw), lambda b: (b, 0, 0)),
            pl.BlockSpec((1, GLA_HEADS, GLA_DK, GLA_DV), lambda b: (b, 0, 0, 0)),
        ],
        out_shape=[
            jax.ShapeDtypeStruct((n, 1, vw), F32),
            jax.ShapeDtypeStruct((n, GLA_HEADS, GLA_DK, GLA_DV), F32),
        ],
        compiler_params=_params(("arbitrary",)),
        name="gla_sample",
    )(p3, p3, p3, alr3, wup, ba, s0)


def _merge_kernel(o_ref, r_ref, cb_ref, cc_ref, cx_ref, ga_ref, gb_ref, pa_ref, pb_ref,
                  gg_ref, cw_ref, cbias_ref, m_ref, u_ref, *, prompt, tiles_per_seq):
    u = cc_ref[...] * cx_ref[...]
    tm = u.shape[0]
    if prompt:
        start = (pl.program_id(0) % tiles_per_seq) == 0
        prev = jnp.where(start, 0.0, pa_ref[...] * pb_ref[...])
        row = lax.broadcasted_iota(jnp.int32, (tm, D_MODEL), 0)
        um1 = jnp.where(row == 0, prev[7:8, :], pltpu.roll(u, 1, 0))
        um2 = jnp.where(row == 0, prev[6:7, :],
                        jnp.where(row == 1, prev[7:8, :], pltpu.roll(u, 2, 0)))
        u_ref[0] = u[tm - SUBLANES:, :]
    else:
        um2 = pa_ref[...]
        um1 = pb_ref[...]
        u_ref[...] = u
    z = cbias_ref[...] + um2 * cw_ref[0:1, :] + um1 * cw_ref[1:2, :] + u * cw_ref[2:3, :]
    y_b = cb_ref[...] * z
    r = r_ref[...]
    sr = r * jax.nn.sigmoid(r)
    m = jax.nn.sigmoid(gb_ref[...]) * y_b
    sga = jax.nn.sigmoid(ga_ref[...])
    for h in range(GLA_HEADS):
        dv = slice(h * GLA_DV, (h + 1) * GLA_DV)
        on = _rms(o_ref[:, dv], gg_ref[:, dv])
        m_ref[:, dv] = (m[:, dv] + sga[:, dv] * (sr[:, dv] * on)).astype(BF16)


def _merge(o, p, pa, pb, gg, cw, cbias, tm, prompt, seq):
    t = o.shape[0]
    n_tiles = t // tm
    col = lambda c: pl.BlockSpec((tm, D_MODEL), lambda i: (i, c))
    if prompt:
        prev_rows = tm // SUBLANES
        prev = lambda c: pl.BlockSpec(
            (SUBLANES, D_MODEL), lambda i: (jnp.maximum(i * prev_rows - 1, 0), c))
        pspecs = [prev(4), prev(5)]
        uspec = pl.BlockSpec((1, SUBLANES, D_MODEL), lambda i: (i, 0, 0))
        ushape = jax.ShapeDtypeStruct((n_tiles, SUBLANES, D_MODEL), F32)
    else:
        pspecs = [pl.BlockSpec((tm, D_MODEL), lambda i: (i, 0))] * 2
        uspec = pl.BlockSpec((tm, D_MODEL), lambda i: (i, 0))
        ushape = jax.ShapeDtypeStruct((t, D_MODEL), F32)
    vec = lambda r: pl.BlockSpec((r, D_MODEL), lambda i: (0, 0))
    return pl.pallas_call(
        functools.partial(_merge_kernel, prompt=prompt,
                          tiles_per_seq=max(seq // tm, 1)),
        grid=(n_tiles,),
        in_specs=[pl.BlockSpec((tm, D_MODEL), lambda i: (i, 0)),
                  col(2), col(3), col(4), col(5), col(6), col(7)] + pspecs
                 + [vec(1), vec(CONV_WIDTH), vec(1)],
        out_specs=[pl.BlockSpec((tm, D_MODEL), lambda i: (i, 0)), uspec],
        out_shape=[jax.ShapeDtypeStruct((t, D_MODEL), BF16), ushape],
        compiler_params=_params(("arbitrary",)),
        name="merge_prompt" if prompt else "merge_sample",
    )(o, p, p, p, p, p, p, pa, pb, gg, cw, cbias)


def _outproj_kernel(x_ref, m_ref, wo_ref, g2_ref, wq_ref, xm_ref, h2t_ref, qp_ref):
    xm = x_ref[...] + jnp.dot(m_ref[...], wo_ref[...], preferred_element_type=F32)
    xm_ref[...] = xm
    h2 = _rms(xm, g2_ref[...])
    h2t_ref[...] = jnp.transpose(h2).astype(BF16)
    qp_ref[...] = jnp.dot(h2.astype(BF16), wq_ref[...], preferred_element_type=F32)


def _outproj(x, m, wo, g2, wq, tm):
    t = x.shape[0]
    tile = pl.BlockSpec((tm, D_MODEL), lambda i: (i, 0))
    whole = pl.BlockSpec((D_MODEL, D_MODEL), lambda i: (0, 0))
    return pl.pallas_call(
        _outproj_kernel,
        grid=(t // tm,),
        in_specs=[tile, tile, whole, pl.BlockSpec((1, D_MODEL), lambda i: (0, 0)), whole],
        out_specs=[tile, pl.BlockSpec((D_MODEL, tm), lambda i: (0, i)), tile],
        out_shape=[jax.ShapeDtypeStruct((t, D_MODEL), F32),
                   jax.ShapeDtypeStruct((D_MODEL, t), BF16),
                   jax.ShapeDtypeStruct((t, D_MODEL), F32)],
        compiler_params=_params(("arbitrary",)),
        name="outproj",
    )(x, m, wo, g2, wq)


def _top_values(work, n):
    vals = []
    rank = jnp.full(work.shape, float(n), F32)
    for r in range(n):
        mx = jnp.max(work, axis=0, keepdims=True)
        hit = work == mx
        vals.append(mx)
        rank = jnp.where(hit, float(r), rank)
        work = jnp.where(hit, NEG_INF, work)
    return vals, rank


def _route_kernel(qp_ref, keys_ref, n1_ref, p1_ref, rank2_ref, p2_ref):
    for h in range(PEER_HEADS):
        st = []
        for side in range(2):
            c0 = (2 * h + side) * DQ_HALF
            st.append(lax.dot_general(keys_ref[side], qp_ref[:, c0:c0 + DQ_HALF],
                                      (((1,), (1,)), ((), ())),
                                      precision=lax.Precision.HIGHEST,
                                      preferred_element_type=F32))
        v1, rank1 = _top_values(st[0], PEER_TOPK)
        v2, rank2 = _top_values(st[1], PEER_TOPK)
        v2 = jnp.concatenate(v2, axis=0)
        cand = [a + v2 for a in v1]
        thr = _top_values(jnp.concatenate(cand, axis=0), PEER_TOPK)[0][-1]
        m1 = v1[0]
        m2 = v2[0:1, :]
        z = jnp.zeros_like(m1)
        n1 = jnp.zeros_like(rank1)
        for a, ca in enumerate(cand):
            keep = ca >= thr
            z = z + jnp.sum(jnp.where(keep, jnp.exp(ca - (m1 + m2)), 0.0),
                            axis=0, keepdims=True)
            cnt = jnp.sum(jnp.where(keep, 1.0, 0.0), axis=0, keepdims=True)
            n1 = jnp.where(rank1 == float(a), cnt, n1)
        n1_ref[h] = n1
        p1_ref[h] = jnp.exp(st[0] - m1)
        rank2_ref[h] = rank2.astype(BF16)
        p2_ref[h] = (jnp.exp(st[1] - m2) / z).astype(BF16)


def _route(qp, keys, tm):
    t = qp.shape[0]
    big = pl.BlockSpec((PEER_HEADS, N_KEYS, tm), lambda i: (0, 0, i))
    f32s = jax.ShapeDtypeStruct((PEER_HEADS, N_KEYS, t), F32)
    bf16s = jax.ShapeDtypeStruct((PEER_HEADS, N_KEYS, t), BF16)
    return pl.pallas_call(
        _route_kernel,
        grid=(t // tm,),
        in_specs=[pl.BlockSpec((tm, D_MODEL), lambda i: (i, 0)),
                  pl.BlockSpec((2, N_KEYS, DQ_HALF), lambda i: (0, 0, 0))],
        out_specs=[big, big, big, big],
        out_shape=[f32s, f32s, bf16s, bf16s],
        compiler_params=_params(("arbitrary",)),
        name="route",
    )(qp, keys)


def _row_bf16(ref, h, e1, tm):
    row = jnp.broadcast_to(ref[h, pl.ds(e1, 1), :], (BF16_ROWS, tm)).astype(BF16)
    return jnp.concatenate([row] * (N_KEYS // BF16_ROWS), axis=0)


def _peer_kernel(h2t_ref, u_ref, v_ref, n1_ref, p1_ref, rank2_ref, p2_ref,
                 xm_ref, gf_ref, out_ref, *, te):
    j = pl.program_id(1)

    @pl.when(j == 0)
    def _():
        out_ref[...] = jnp.zeros_like(out_ref)

    tm = h2t_ref.shape[1]
    rows_per_part = te // PEER_PARTS
    acc = None
    for part in range(PEER_PARTS):
        lo = part * rows_per_part
        a = jnp.dot(u_ref[lo:lo + rows_per_part, :], h2t_ref[...],
                    preferred_element_type=F32)
        blocks = []
        for l in range(rows_per_part // N_KEYS):
            e1 = j * (te // N_KEYS) + part * (rows_per_part // N_KEYS) + l
            g = jnp.zeros((N_KEYS, tm), BF16)
            for h in range(PEER_HEADS):
                keep = rank2_ref[h] < _row_bf16(n1_ref, h, e1, tm)
                g = g + jnp.where(keep, p2_ref[h], 0.0) * _row_bf16(p1_ref, h, e1, tm)
            al = a[l * N_KEYS:(l + 1) * N_KEYS, :]
            gelu = 0.5 * al * (1.0 + lax.erf(al * INV_SQRT2))
            blocks.append(gelu.astype(BF16) * g)
        wt = jnp.concatenate(blocks, axis=0) if len(blocks) > 1 else blocks[0]
        d = jnp.dot(jnp.transpose(wt), v_ref[lo:lo + rows_per_part, :],
                    preferred_element_type=F32)
        acc = d if acc is None else acc + d
    out_ref[...] += acc

    @pl.when(j == pl.num_programs(1) - 1)
    def _():
        out_ref[...] = _rms(xm_ref[...] + out_ref[...], gf_ref[...])


def _peer(h2t, u_bf, v_bf, n1, p1, rank2, p2, xm, gf, tm, te):
    t = xm.shape[0]
    big = pl.BlockSpec((PEER_HEADS, N_KEYS, tm), lambda i, j: (0, 0, i))
    return pl.pallas_call(
        functools.partial(_peer_kernel, te=te),
        grid=(t // tm, N_EXPERTS // te),
        in_specs=[pl.BlockSpec((D_MODEL, tm), lambda i, j: (0, i)),
                  pl.BlockSpec((te, D_MODEL), lambda i, j: (j, 0)),
                  pl.BlockSpec((te, D_MODEL), lambda i, j: (j, 0)),
                  big, big, big, big,
                  pl.BlockSpec((tm, D_MODEL), lambda i, j: (i, 0)),
                  pl.BlockSpec((1, D_MODEL), lambda i, j: (0, 0))],
        out_specs=pl.BlockSpec((tm, D_MODEL), lambda i, j: (i, 0)),
        out_shape=jax.ShapeDtypeStruct((t, D_MODEL), F32),
        compiler_params=_params(("arbitrary", "arbitrary")),
        name="peer",
    )(h2t, u_bf, v_bf, n1, p1, rank2, p2, xm, gf)


def _channel_mixer(x, m, wts, tm_proj, tm_route, tm_peer, te):
    xm, h2t, qp = _outproj(x, m, wts["wo"], wts["g2"], wts["wq"], tm_proj)
    n1, p1, rank2, p2 = _route(qp, wts["keys"], tm_route)
    return _peer(h2t, wts["u"], wts["v"], n1, p1, rank2, p2, xm, wts["gf"], tm_peer, te)


def kernel(x_prompt, x_sample, state_gla, state_conv, norm1_g, w_in, w_alpha_up, b_alpha,
           gla_norm_g, conv_w, conv_b, w_out, norm2_g, w_query, sub_keys, expert_u, expert_v,
           norm_f_g):
    batch, seq, _ = x_prompt.shape
    n_dec = x_sample.shape[0]
    assert w_in.shape[0] == 1 and x_sample.shape[1] == 1

    alr0 = 2 * GLA_HEADS * GLA_DK + 2 * GLA_HEADS * GLA_DV
    w_main = jnp.concatenate([w_in[0][:, :alr0], w_in[0][:, alr0 + GLA_RANK:]],
                             axis=1).astype(BF16)
    w_alr = jnp.pad(w_in[0][:, alr0:alr0 + GLA_RANK],
                    ((0, 0), (0, LANES - GLA_RANK))).astype(BF16)
    wup = jnp.pad(w_alpha_up[0], ((0, LANES - GLA_RANK), (0, 0))).astype(BF16)
    ba = b_alpha[0][None, :]
    g1 = norm1_g[0][None, :]
    gg = gla_norm_g[0][None, :]
    cw = conv_w[0]
    cbias = conv_b[0][None, :]
    wts = dict(wo=w_out[0].astype(BF16), g2=norm2_g[0][None, :],
               wq=w_query[0].astype(BF16), keys=sub_keys[0],
               u=expert_u[0].astype(BF16), v=expert_v[0].astype(BF16),
               gf=norm_f_g[None, :])

    xp = x_prompt.reshape(batch * seq, D_MODEL)
    p, alr = _inproj(xp, g1, w_main, w_alr, tm=1024, tn=1024)
    o, s_p = _gla_prompt(p, alr, wup, ba, batch, seq, rows=256)
    tm_merge = 256
    m, utail = _merge(o, p, p, p, gg, cw, cbias, tm_merge, True, seq)
    y_p = _channel_mixer(xp, m, wts, tm_proj=256, tm_route=256, tm_peer=512, te=512)
    last = utail.reshape(batch, seq // tm_merge, SUBLANES, D_MODEL)[:, -1, SUBLANES - 2:, :]

    xs = x_sample.reshape(n_dec, D_MODEL)
    ps, alrs = _inproj(xs, g1, w_main, w_alr, tm=n_dec, tn=1024)
    os_, s_s = _gla_sample(ps.reshape(n_dec, 1, N_MAIN), alrs.reshape(n_dec, 1, LANES),
                           wup, ba, state_gla[0])
    ms, us = _merge(os_.reshape(n_dec, D_MODEL), ps, state_conv[0][:, 0, :],
                    state_conv[0][:, 1, :], gg, cw, cbias, n_dec, False, 1)
    y_s = _channel_mixer(xs, ms, wts, tm_proj=n_dec, tm_route=n_dec, tm_peer=n_dec, te=512)
    conv_s = jnp.stack([state_conv[0][:, 1, :], us], axis=1)

    return (y_p.reshape(batch, seq, D_MODEL),
            y_s.reshape(n_dec, 1, D_MODEL),
            s_p[None],
            last[None],
            s_s[None],
            conv_s[None])
```

```python
import functools
import math

import jax
import jax.numpy as jnp
from jax import lax
from jax.experimental import pallas as pl
from jax.experimental.pallas import tpu as pltpu

F32 = jnp.float32
BF16 = jnp.bfloat16

D_MODEL = 2048
GLA_HEADS = 4
GLA_DK = 256
GLA_DV = 512
GLA_RANK = 16
GLA_TAU = 16.0
GLA_CHUNK = 64
CONV_WIDTH = 3
PEER_HEADS = 8
N_KEYS = 128
N_EXPERTS = N_KEYS * N_KEYS
PEER_TOPK = 16
DQ_HALF = 128
EPS = 1e-6

LANES = 128
SUBLANES = 8
BF16_ROWS = 16
VMEM_LIMIT_BYTES = 60000 * 1024

N_MAIN = 16384
GLA_SUB = 16
PEER_PARTS = 2
NEG_INF = float("-inf")
INV_SQRT2 = 1.0 / math.sqrt(2.0)


def _params(semantics):
    return pltpu.CompilerParams(dimension_semantics=semantics,
                                vmem_limit_bytes=VMEM_LIMIT_BYTES)


def _rms(x, g):
    return x * lax.rsqrt(jnp.mean(x * x, axis=-1, keepdims=True) + EPS) * g


def _inproj_kernel(x_ref, g_ref, wlo_ref, whi_ref, walr_ref, p_ref, alr_ref, h_ref, *,
                   n_lo):
    j = pl.program_id(1)

    @pl.when(j == 0)
    def _():
        hb = _rms(x_ref[...], g_ref[...]).astype(BF16)
        h_ref[...] = hb
        alr_ref[...] = jnp.dot(hb, walr_ref[...], preferred_element_type=F32)

    @pl.when(j < n_lo)
    def _():
        p_ref[...] = jnp.dot(h_ref[...], wlo_ref[...], preferred_element_type=F32)

    @pl.when(j >= n_lo)
    def _():
        p_ref[...] = jnp.dot(h_ref[...], whi_ref[...], preferred_element_type=F32)


def _inproj(x, g, w_lo, w_hi, w_alr, tm, tn):
    t = x.shape[0]
    n_lo = w_lo.shape[1] // tn
    return pl.pallas_call(
        functools.partial(_inproj_kernel, n_lo=n_lo),
        grid=(t // tm, N_MAIN // tn),
        in_specs=[
            pl.BlockSpec((tm, D_MODEL), lambda i, j: (i, 0)),
            pl.BlockSpec((1, D_MODEL), lambda i, j: (0, 0)),
            pl.BlockSpec((D_MODEL, tn), lambda i, j: (0, jnp.minimum(j, n_lo - 1))),
            pl.BlockSpec((D_MODEL, tn), lambda i, j: (0, jnp.maximum(j - n_lo, 0))),
            pl.BlockSpec((D_MODEL, LANES), lambda i, j: (0, 0)),
        ],
        out_specs=[
            pl.BlockSpec((tm, tn), lambda i, j: (i, j)),
            pl.BlockSpec((tm, LANES), lambda i, j: (i, 0)),
        ],
        out_shape=[
            jax.ShapeDtypeStruct((t, N_MAIN), F32),
            jax.ShapeDtypeStruct((t, LANES), F32),
        ],
        scratch_shapes=[pltpu.VMEM((tm, D_MODEL), BF16)],
        compiler_params=_params(("arbitrary", "arbitrary")),
        name="inproj",
    )(x, g, w_lo, w_hi, w_alr)


def _log_decay(alr, wup, ba):
    xa = jnp.dot(alr.astype(BF16), wup, preferred_element_type=F32) + ba
    return -(jnp.maximum(-xa, 0.0) + jnp.log1p(jnp.exp(-jnp.abs(xa)))) * (1.0 / GLA_TAU)


def _col_scale(row, width):
    col = jnp.transpose(jnp.broadcast_to(row, (LANES, row.shape[1])))
    return jnp.concatenate([col] * (width // LANES), axis=1)


def _gla_prompt_kernel(q_ref, k_ref, v_ref, alr_ref, wup_ref, ba_ref,
                       o_ref, sout_ref, s_ref, b_ref, *, rows):
    c = pl.program_id(2)

    @pl.when(c == 0)
    def _():
        s_ref[...] = jnp.zeros_like(s_ref)

    g = _log_decay(alr_ref[...], wup_ref[...], ba_ref[...])
    rmod = lax.broadcasted_iota(jnp.int32, (rows, GLA_DK), 0) & (GLA_CHUNK - 1)
    step = 1
    while step < GLA_CHUNK:
        g = g + jnp.where(rmod >= step, pltpu.roll(g, step, 0), 0.0)
        step *= 2
    b_ref[...] = g

    row = lax.broadcasted_iota(jnp.int32, (GLA_CHUNK, GLA_DK), 0)
    rsub = row & (GLA_SUB - 1)
    ri = lax.broadcasted_iota(jnp.int32, (GLA_CHUNK, GLA_CHUNK), 0)
    cj = lax.broadcasted_iota(jnp.int32, (GLA_CHUNK, GLA_CHUNK), 1)
    n_sub = GLA_CHUNK // GLA_SUB

    def chunk(ci, carry):
        r0 = pl.multiple_of(ci * GLA_CHUNK, GLA_CHUNK)
        qc = q_ref[pl.ds(r0, GLA_CHUNK), :] * (GLA_DK ** -0.5)
        kc = k_ref[pl.ds(r0, GLA_CHUNK), :]
        vc = v_ref[pl.ds(r0, GLA_CHUNK), :]
        b = b_ref[pl.ds(r0, GLA_CHUNK), :]
        bl = b[GLA_CHUNK - 1:GLA_CHUNK, :]
        s = s_ref[...]
        vb = vc.astype(BF16)

        o = jnp.dot((qc * jnp.exp(b)).astype(BF16), s.astype(BF16),
                    preferred_element_type=F32)

        parts = [jnp.zeros((GLA_SUB, GLA_CHUNK), F32)]
        for blk in range(1, n_sub):
            lo = blk * GLA_SUB
            ref = b[lo - 1:lo, :]
            qs = qc[lo:lo + GLA_SUB, :] * jnp.exp(b[lo:lo + GLA_SUB, :] - ref)
            ks = kc * jnp.exp(jnp.where(row < lo, ref - b, NEG_INF))
            parts.append(lax.dot_general(qs.astype(BF16), ks.astype(BF16),
                                         (((1,), (1,)), ((), ())),
                                         preferred_element_type=F32))
        attn = jnp.concatenate(parts, axis=0)

        for d in range(GLA_SUB):
            if d == 0:
                t = qc * kc
            else:
                kd = pltpu.roll(kc, d, 0)
                bd = pltpu.roll(b, d, 0)
                t = qc * kd * jnp.exp(jnp.where(rsub >= d, b - bd, NEG_INF))
            cs = jnp.sum(t, axis=1, keepdims=True)
            attn = attn + jnp.where(ri - cj == d, cs, 0.0)

        o = o + jnp.dot(attn.astype(BF16), vb, preferred_element_type=F32)
        o_ref[pl.ds(r0, GLA_CHUNK), :] = o

        kdec = (kc * jnp.exp(bl - b)).astype(BF16)
        s_ref[...] = (_col_scale(jnp.exp(bl), GLA_DV) * s
                      + lax.dot_general(kdec, vb, (((0,), (0,)), ((), ())),
                                        preferred_element_type=F32))
        return carry

    lax.fori_loop(0, rows // GLA_CHUNK, chunk, 0)

    @pl.when(c == pl.num_programs(2) - 1)
    def _():
        sout_ref[0, 0] = s_ref[...]


def _gla_prompt(p, alr, wup, ba, batch, seq, rows):
    nsteps = seq // rows
    kq = GLA_HEADS * GLA_DK // GLA_DK
    kv = 2 * GLA_HEADS * GLA_DK // GLA_DV

    def rowmap(b, h, c):
        return b * nsteps + c

    return pl.pallas_call(
        functools.partial(_gla_prompt_kernel, rows=rows),
        grid=(batch, GLA_HEADS, nsteps),
        in_specs=[
            pl.BlockSpec((rows, GLA_DK), lambda b, h, c: (rowmap(b, h, c), h)),
            pl.BlockSpec((rows, GLA_DK), lambda b, h, c: (rowmap(b, h, c), kq + h)),
            pl.BlockSpec((rows, GLA_DV), lambda b, h, c: (rowmap(b, h, c), kv + h)),
            pl.BlockSpec((rows, LANES), lambda b, h, c: (rowmap(b, h, c), 0)),
            pl.BlockSpec((LANES, GLA_DK), lambda b, h, c: (0, h)),
            pl.BlockSpec((1, GLA_DK), lambda b, h, c: (0, h)),
        ],
        out_specs=[
            pl.BlockSpec((rows, GLA_DV), lambda b, h, c: (rowmap(b, h, c), h)),
            pl.BlockSpec((1, 1, GLA_DK, GLA_DV), lambda b, h, c: (b, h, 0, 0)),
        ],
        out_shape=[
            jax.ShapeDtypeStruct((batch * seq, GLA_HEADS * GLA_DV), F32),
            jax.ShapeDtypeStruct((batch, GLA_HEADS, GLA_DK, GLA_DV), F32),
        ],
        scratch_shapes=[pltpu.VMEM((GLA_DK, GLA_DV), F32),
                        pltpu.VMEM((rows, GLA_DK), F32)],
        compiler_params=_params(("arbitrary", "arbitrary", "arbitrary")),
        name="gla_prompt",
    )(p, p, p, alr, wup, ba)


def _gla_sample_kernel(q_ref, k_ref, v_ref, alr_ref, wup_ref, ba_ref, s0_ref,
                       o_ref, s1_ref):
    g_all = _log_decay(jnp.broadcast_to(alr_ref[0], (SUBLANES, LANES)),
                       wup_ref[...], ba_ref[...])
    first = lax.broadcasted_iota(jnp.int32, (SUBLANES, GLA_DK), 0) == 0
    for h in range(GLA_HEADS):
        dk = slice(h * GLA_DK, (h + 1) * GLA_DK)
        dv = slice(h * GLA_DV, (h + 1) * GLA_DV)
        q = jnp.broadcast_to(q_ref[0, :, dk], (SUBLANES, GLA_DK)) * (GLA_DK ** -0.5)
        k = jnp.broadcast_to(k_ref[0, :, dk], (SUBLANES, GLA_DK))
        v = jnp.broadcast_to(v_ref[0, :, dv], (SUBLANES, GLA_DV))
        g = g_all[:, dk]
        s = s0_ref[0, h]
        attn = jnp.sum(q * k, axis=1, keepdims=True)
        o = attn * v + jnp.dot((q * jnp.exp(g)).astype(BF16), s.astype(BF16),
                               preferred_element_type=F32)
        o_ref[0, :, dv] = o[0:1, :]
        k1 = jnp.where(first, k, 0.0).astype(BF16)
        s1_ref[0, h] = (_col_scale(jnp.exp(g[0:1, :]), GLA_DV) * s
                        + lax.dot_general(k1, v.astype(BF16), (((0,), (0,)), ((), ())),
                                          preferred_element_type=F32))


def _gla_sample(p3, alr3, wup, ba, s0):
    n = p3.shape[0]
    qk = GLA_HEADS * GLA_DK
    vw = GLA_HEADS * GLA_DV
    return pl.pallas_call(
        _gla_sample_kernel,
        grid=(n,),
        in_specs=[
            pl.BlockSpec((1, 1, qk), lambda b: (b, 0, 0)),
            pl.BlockSpec((1, 1, qk), lambda b: (b, 0, 1)),
            pl.BlockSpec((1, 1, vw), lambda b: (b, 0, 2 * qk // vw)),
            pl.BlockSpec((1, 1, LANES), lambda b: (b, 0, 0)),
            pl.BlockSpec((LANES, qk), lambda b: (0, 0)),
            pl.BlockSpec((1, qk), lambda b: (0, 0)),
            pl.BlockSpec((1, GLA_HEADS, GLA_DK, GLA_DV), lambda b: (b, 0, 0, 0)),
        ],
        out_specs=[
            pl.BlockSpec((1, 1, vw), lambda b: (b, 0, 0)),
            pl.BlockSpec((1, GLA_HEADS, GLA_DK, GLA_DV), lambda b: (b, 0, 0, 0)),
        ],
        out_shape=[
            jax.ShapeDtypeStruct((n, 1, vw), F32),
            jax.ShapeDtypeStruct((n, GLA_HEADS, GLA_DK, GLA_DV), F32),
        ],
        compiler_params=_params(("arbitrary",)),
        name="gla_sample",
    )(p3, p3, p3, alr3, wup, ba, s0)


def _merge_kernel(o_ref, r_ref, cb_ref, cc_ref, cx_ref, ga_ref, gb_ref, pa_ref, pb_ref,
                  gg_ref, cw_ref, cbias_ref, m_ref, u_ref, *, prompt, tiles_per_seq):
    u = cc_ref[...] * cx_ref[...]
    tm = u.shape[0]
    if prompt:
        start = (pl.program_id(0) % tiles_per_seq) == 0
        prev = jnp.where(start, 0.0, pa_ref[...] * pb_ref[...])
        row = lax.broadcasted_iota(jnp.int32, (tm, D_MODEL), 0)
        um1 = jnp.where(row == 0, prev[7:8, :], pltpu.roll(u, 1, 0))
        um2 = jnp.where(row == 0, prev[6:7, :],
                        jnp.where(row == 1, prev[7:8, :], pltpu.roll(u, 2, 0)))
        u_ref[0] = u[tm - SUBLANES:, :]
    else:
        um2 = pa_ref[...]
        um1 = pb_ref[...]
        u_ref[...] = u
    z = cbias_ref[...] + um2 * cw_ref[0:1, :] + um1 * cw_ref[1:2, :] + u * cw_ref[2:3, :]
    y_b = cb_ref[...] * z
    r = r_ref[...]
    sr = r * jax.nn.sigmoid(r)
    m = jax.nn.sigmoid(gb_ref[...]) * y_b
    sga = jax.nn.sigmoid(ga_ref[...])
    for h in range(GLA_HEADS):
        dv = slice(h * GLA_DV, (h + 1) * GLA_DV)
        on = _rms(o_ref[:, dv], gg_ref[:, dv])
        m_ref[:, dv] = (m[:, dv] + sga[:, dv] * (sr[:, dv] * on)).astype(BF16)


def _merge(o, p, pa, pb, gg, cw, cbias, tm, prompt, seq):
    t = o.shape[0]
    n_tiles = t // tm
    col = lambda c: pl.BlockSpec((tm, D_MODEL), lambda i: (i, c))
    if prompt:
        prev_rows = tm // SUBLANES
        prev = lambda c: pl.BlockSpec(
            (SUBLANES, D_MODEL), lambda i: (jnp.maximum(i * prev_rows - 1, 0), c))
        pspecs = [prev(4), prev(5)]
        uspec = pl.BlockSpec((1, SUBLANES, D_MODEL), lambda i: (i, 0, 0))
        ushape = jax.ShapeDtypeStruct((n_tiles, SUBLANES, D_MODEL), F32)
    else:
        pspecs = [pl.BlockSpec((tm, D_MODEL), lambda i: (i, 0))] * 2
        uspec = pl.BlockSpec((tm, D_MODEL), lambda i: (i, 0))
        ushape = jax.ShapeDtypeStruct((t, D_MODEL), F32)
    vec = lambda r: pl.BlockSpec((r, D_MODEL), lambda i: (0, 0))
    return pl.pallas_call(
        functools.partial(_merge_kernel, prompt=prompt,
                          tiles_per_seq=max(seq // tm, 1)),
        grid=(n_tiles,),
        in_specs=[pl.BlockSpec((tm, D_MODEL), lambda i: (i, 0)),
                  col(2), col(3), col(4), col(5), col(6), col(7)] + pspecs
                 + [vec(1), vec(CONV_WIDTH), vec(1)],
        out_specs=[pl.BlockSpec((tm, D_MODEL), lambda i: (i, 0)), uspec],
        out_shape=[jax.ShapeDtypeStruct((t, D_MODEL), BF16), ushape],
        compiler_params=_params(("arbitrary",)),
        name="merge_prompt" if prompt else "merge_sample",
    )(o, p, p, p, p, p, p, pa, pb, gg, cw, cbias)


def _outproj_kernel(x_ref, m_ref, wo_ref, g2_ref, wq_ref, xm_ref, h2t_ref, qp_ref):
    xm = x_ref[...] + jnp.dot(m_ref[...], wo_ref[...], preferred_element_type=F32)
    xm_ref[...] = xm
    h2 = _rms(xm, g2_ref[...])
    h2t_ref[...] = jnp.transpose(h2).astype(BF16)
    qp_ref[...] = jnp.dot(h2.astype(BF16), wq_ref[...], preferred_element_type=F32)


def _outproj(x, m, wo, g2, wq, tm):
    t = x.shape[0]
    tile = pl.BlockSpec((tm, D_MODEL), lambda i: (i, 0))
    whole = pl.BlockSpec((D_MODEL, D_MODEL), lambda i: (0, 0))
    return pl.pallas_call(
        _outproj_kernel,
        grid=(t // tm,),
        in_specs=[tile, tile, whole, pl.BlockSpec((1, D_MODEL), lambda i: (0, 0)), whole],
        out_specs=[tile, pl.BlockSpec((D_MODEL, tm), lambda i: (0, i)), tile],
        out_shape=[jax.ShapeDtypeStruct((t, D_MODEL), F32),
                   jax.ShapeDtypeStruct((D_MODEL, t), BF16),
                   jax.ShapeDtypeStruct((t, D_MODEL), F32)],
        compiler_params=_params(("arbitrary",)),
        name="outproj",
    )(x, m, wo, g2, wq)


def _top_values(work, n):
    vals = []
    rank = jnp.full(work.shape, float(n), F32)
    for r in range(n):
        mx = jnp.max(work, axis=0, keepdims=True)
        hit = work == mx
        vals.append(mx)
        rank = jnp.where(hit, float(r), rank)
        work = jnp.where(hit, NEG_INF, work)
    return vals, rank


def _route_kernel(qp_ref, keys_ref, n1_ref, p1_ref, rank2_ref, p2_ref):
    for h in range(PEER_HEADS):
        st = []
        for side in range(2):
            c0 = (2 * h + side) * DQ_HALF
            st.append(lax.dot_general(keys_ref[side], qp_ref[:, c0:c0 + DQ_HALF],
                                      (((1,), (1,)), ((), ())),
                                      precision=lax.Precision.HIGHEST,
                                      preferred_element_type=F32))
        v1, rank1 = _top_values(st[0], PEER_TOPK)
        v2, rank2 = _top_values(st[1], PEER_TOPK)
        v2 = jnp.concatenate(v2, axis=0)
        tm = v2.shape[1]
        bidx = lax.broadcasted_iota(jnp.int32, (SUBLANES, tm), 0)
        cand = []
        for a in range(PEER_TOPK // 2):
            nb = PEER_TOPK // (a + 1)
            if nb >= SUBLANES:
                cand.append(v1[a] + v2[0:nb, :])
            else:
                cand.append(jnp.where(bidx < nb, v1[a] + v2[0:SUBLANES, :], NEG_INF))
        tail = jnp.concatenate(v1[PEER_TOPK // 2:], axis=0) + v2[0:1, :]
        cand.append(tail)
        thr = _top_values(jnp.concatenate(cand, axis=0), PEER_TOPK)[0][-1]
        m1 = v1[0]
        m2 = v2[0:1, :]
        z = jnp.zeros_like(m1)
        n1 = jnp.zeros_like(rank1)
        for a, ca in enumerate(cand):
            keep = ca >= thr
            z = z + jnp.sum(jnp.where(keep, jnp.exp(ca - (m1 + m2)), 0.0),
                            axis=0, keepdims=True)
            kept = jnp.where(keep, 1.0, 0.0)
            if a < PEER_TOPK // 2:
                n1 = jnp.where(rank1 == float(a), jnp.sum(kept, axis=0, keepdims=True), n1)
            else:
                for r in range(PEER_TOPK // 2):
                    n1 = jnp.where(rank1 == float(a + r), kept[r:r + 1, :], n1)
        n1_ref[h] = n1
        p1_ref[h] = jnp.exp(st[0] - m1)
        rank2_ref[h] = rank2.astype(BF16)
        p2_ref[h] = (jnp.exp(st[1] - m2) / z).astype(BF16)


def _route(qp, keys, tm):
    t = qp.shape[0]
    big = pl.BlockSpec((PEER_HEADS, N_KEYS, tm), lambda i: (0, 0, i))
    f32s = jax.ShapeDtypeStruct((PEER_HEADS, N_KEYS, t), F32)
    bf16s = jax.ShapeDtypeStruct((PEER_HEADS, N_KEYS, t), BF16)
    return pl.pallas_call(
        _route_kernel,
        grid=(t // tm,),
        in_specs=[pl.BlockSpec((tm, D_MODEL), lambda i: (i, 0)),
                  pl.BlockSpec((2, N_KEYS, DQ_HALF), lambda i: (0, 0, 0))],
        out_specs=[big, big, big, big],
        out_shape=[f32s, f32s, bf16s, bf16s],
        compiler_params=_params(("arbitrary",)),
        name="route",
    )(qp, keys)


def _row_bf16(ref, h, e1, tm):
    row = jnp.broadcast_to(ref[h, pl.ds(e1, 1), :], (BF16_ROWS, tm)).astype(BF16)
    return jnp.concatenate([row] * (N_KEYS // BF16_ROWS), axis=0)


def _peer_kernel(h2t_ref, u_ref, v_ref, n1_ref, p1_ref, rank2_ref, p2_ref,
                 xm_ref, gf_ref, out_ref, *, te):
    j = pl.program_id(1)

    @pl.when(j == 0)
    def _():
        out_ref[...] = jnp.zeros_like(out_ref)

    tm = h2t_ref.shape[1]
    rows_per_part = te // PEER_PARTS
    acc = None
    for part in range(PEER_PARTS):
        lo = part * rows_per_part
        a = jnp.dot(u_ref[lo:lo + rows_per_part, :], h2t_ref[...],
                    preferred_element_type=F32)
        blocks = []
        for l in range(rows_per_part // N_KEYS):
            e1 = j * (te // N_KEYS) + part * (rows_per_part // N_KEYS) + l
            g = jnp.zeros((N_KEYS, tm), BF16)
            for h in range(PEER_HEADS):
                keep = rank2_ref[h] < _row_bf16(n1_ref, h, e1, tm)
                g = g + jnp.where(keep, p2_ref[h], 0.0) * _row_bf16(p1_ref, h, e1, tm)
            al = a[l * N_KEYS:(l + 1) * N_KEYS, :]
            gelu = 0.5 * al * (1.0 + lax.erf(al * INV_SQRT2))
            blocks.append(gelu.astype(BF16) * g)
        wt = jnp.concatenate(blocks, axis=0) if len(blocks) > 1 else blocks[0]
        d = jnp.dot(jnp.transpose(wt), v_ref[lo:lo + rows_per_part, :],
                    preferred_element_type=F32)
        acc = d if acc is None else acc + d
    out_ref[...] += acc

    @pl.when(j == pl.num_programs(1) - 1)
    def _():
        out_ref[...] = _rms(xm_ref[...] + out_ref[...], gf_ref[...])


def _peer(h2t, u_bf, v_bf, n1, p1, rank2, p2, xm, gf, tm, te):
    t = xm.shape[0]
    big = pl.BlockSpec((PEER_HEADS, N_KEYS, tm), lambda i, j: (0, 0, i))
    return pl.pallas_call(
        functools.partial(_peer_kernel, te=te),
        grid=(t // tm, N_EXPERTS // te),
        in_specs=[pl.BlockSpec((D_MODEL, tm), lambda i, j: (0, i)),
                  pl.BlockSpec((te, D_MODEL), lambda i, j: (j, 0)),
                  pl.BlockSpec((te, D_MODEL), lambda i, j: (j, 0)),
                  big, big, big, big,
                  pl.BlockSpec((tm, D_MODEL), lambda i, j: (i, 0)),
                  pl.BlockSpec((1, D_MODEL), lambda i, j: (0, 0))],
        out_specs=pl.BlockSpec((tm, D_MODEL), lambda i, j: (i, 0)),
        out_shape=jax.ShapeDtypeStruct((t, D_MODEL), F32),
        compiler_params=_params(("arbitrary", "arbitrary")),
        name="peer",
    )(h2t, u_bf, v_bf, n1, p1, rank2, p2, xm, gf)


def _channel_mixer(x, m, wts, tm_proj, tm_route, tm_peer, te):
    xm, h2t, qp = _outproj(x, m, wts["wo"], wts["g2"], wts["wq"], tm_proj)
    n1, p1, rank2, p2 = _route(qp, wts["keys"], tm_route)
    return _peer(h2t, wts["u"], wts["v"], n1, p1, rank2, p2, xm, wts["gf"], tm_peer, te)


def kernel(x_prompt, x_sample, state_gla, state_conv, norm1_g, w_in, w_alpha_up, b_alpha,
           gla_norm_g, conv_w, conv_b, w_out, norm2_g, w_query, sub_keys, expert_u, expert_v,
           norm_f_g):
    batch, seq, _ = x_prompt.shape
    n_dec = x_sample.shape[0]
    assert w_in.shape[0] == 1 and x_sample.shape[1] == 1

    alr0 = 2 * GLA_HEADS * GLA_DK + 2 * GLA_HEADS * GLA_DV
    w_lo = w_in[0][:, :alr0].astype(BF16)
    w_hi = w_in[0][:, alr0 + GLA_RANK:].astype(BF16)
    w_alr = jnp.pad(w_in[0][:, alr0:alr0 + GLA_RANK],
                    ((0, 0), (0, LANES - GLA_RANK))).astype(BF16)
    wup = jnp.pad(w_alpha_up[0], ((0, LANES - GLA_RANK), (0, 0))).astype(BF16)
    ba = b_alpha[0][None, :]
    g1 = norm1_g[0][None, :]
    gg = gla_norm_g[0][None, :]
    cw = conv_w[0]
    cbias = conv_b[0][None, :]
    wts = dict(wo=w_out[0].astype(BF16), g2=norm2_g[0][None, :],
               wq=w_query[0].astype(BF16), keys=sub_keys[0],
               u=expert_u[0].astype(BF16), v=expert_v[0].astype(BF16),
               gf=norm_f_g[None, :])

    xp = x_prompt.reshape(batch * seq, D_MODEL)
    p, alr = _inproj(xp, g1, w_lo, w_hi, w_alr, tm=1024, tn=1024)
    o, s_p = _gla_prompt(p, alr, wup, ba, batch, seq, rows=256)
    tm_merge = 256
    m, utail = _merge(o, p, p, p, gg, cw, cbias, tm_merge, True, seq)
    y_p = _channel_mixer(xp, m, wts, tm_proj=256, tm_route=256, tm_peer=512, te=512)
    last = utail.reshape(batch, seq // tm_merge, SUBLANES, D_MODEL)[:, -1, SUBLANES - 2:, :]

    xs = x_sample.reshape(n_dec, D_MODEL)
    ps, alrs = _inproj(xs, g1, w_lo, w_hi, w_alr, tm=n_dec, tn=1024)
    os_, s_s = _gla_sample(ps.reshape(n_dec, 1, N_MAIN), alrs.reshape(n_dec, 1, LANES),
                           wup, ba, state_gla[0])
    ms, us = _merge(os_.reshape(n_dec, D_MODEL), ps, state_conv[0][:, 0, :],
                    state_conv[0][:, 1, :], gg, cw, cbias, n_dec, False, 1)
    y_s = _channel_mixer(xs, ms, wts, tm_proj=n_dec, tm_route=n_dec, tm_peer=n_dec, te=512)
    conv_s = jnp.stack([state_conv[0][:, 1, :], us], axis=1)

    return (y_p.reshape(batch, seq, D_MODEL),
            y_s.reshape(n_dec, 1, D_MODEL),
            s_p[None],
            last[None],
            s_s[None],
            conv_s[None])
```

```python
import functools
import math

import jax
import jax.numpy as jnp
from jax import lax
from jax.experimental import pallas as pl
from jax.experimental.pallas import tpu as pltpu

F32 = jnp.float32
BF16 = jnp.bfloat16

D_MODEL = 2048
GLA_HEADS = 4
GLA_DK = 256
GLA_DV = 512
GLA_RANK = 16
GLA_TAU = 16.0
GLA_CHUNK = 64
CONV_WIDTH = 3
PEER_HEADS = 8
N_KEYS = 128
N_EXPERTS = N_KEYS * N_KEYS
PEER_TOPK = 16
DQ_HALF = 128
EPS = 1e-6

LANES = 128
SUBLANES = 8
BF16_ROWS = 16
VMEM_LIMIT_BYTES = 60000 * 1024

N_MAIN = 16384
GLA_SUB = 8
GLA_HEADS_PER_STEP = 4
PEER_PART_ROWS = 256
NEG_INF = float("-inf")
INV_SQRT2 = 1.0 / math.sqrt(2.0)


def _params(semantics):
    return pltpu.CompilerParams(dimension_semantics=semantics,
                                vmem_limit_bytes=VMEM_LIMIT_BYTES)


def _rms(x, g):
    return x * lax.rsqrt(jnp.mean(x * x, axis=-1, keepdims=True) + EPS) * g


def _inproj_kernel(x_ref, g_ref, wlo_ref, whi_ref, walr_ref, p_ref, alr_ref, h_ref, *,
                   n_lo):
    j = pl.program_id(1)

    @pl.when(j == 0)
    def _():
        hb = _rms(x_ref[...], g_ref[...]).astype(BF16)
        h_ref[...] = hb
        alr_ref[...] = jnp.dot(hb, walr_ref[...], preferred_element_type=F32)

    @pl.when(j < n_lo)
    def _():
        p_ref[...] = jnp.dot(h_ref[...], wlo_ref[...], preferred_element_type=F32)

    @pl.when(j >= n_lo)
    def _():
        p_ref[...] = jnp.dot(h_ref[...], whi_ref[...], preferred_element_type=F32)


def _inproj(x, g, w_lo, w_hi, w_alr, tm, tn):
    t = x.shape[0]
    n_lo = w_lo.shape[1] // tn
    return pl.pallas_call(
        functools.partial(_inproj_kernel, n_lo=n_lo),
        grid=(t // tm, N_MAIN // tn),
        in_specs=[
            pl.BlockSpec((tm, D_MODEL), lambda i, j: (i, 0)),
            pl.BlockSpec((1, D_MODEL), lambda i, j: (0, 0)),
            pl.BlockSpec((D_MODEL, tn), lambda i, j: (0, jnp.minimum(j, n_lo - 1))),
            pl.BlockSpec((D_MODEL, tn), lambda i, j: (0, jnp.maximum(j - n_lo, 0))),
            pl.BlockSpec((D_MODEL, LANES), lambda i, j: (0, 0)),
        ],
        out_specs=[
            pl.BlockSpec((tm, tn), lambda i, j: (i, j)),
            pl.BlockSpec((tm, LANES), lambda i, j: (i, 0)),
        ],
        out_shape=[
            jax.ShapeDtypeStruct((t, N_MAIN), F32),
            jax.ShapeDtypeStruct((t, LANES), F32),
        ],
        scratch_shapes=[pltpu.VMEM((tm, D_MODEL), BF16)],
        compiler_params=_params(("arbitrary", "arbitrary")),
        name="inproj",
    )(x, g, w_lo, w_hi, w_alr)


def _log_decay(alr, wup, ba):
    xa = jnp.dot(alr.astype(BF16), wup, preferred_element_type=F32) + ba
    return -(jnp.maximum(-xa, 0.0) + jnp.log1p(jnp.exp(-jnp.abs(xa)))) * (1.0 / GLA_TAU)


def _col_scale(row, width):
    col = jnp.transpose(jnp.broadcast_to(row, (LANES, row.shape[1])))
    return jnp.concatenate([col] * (width // LANES), axis=1)


def _gla_prompt_kernel(q_ref, k_ref, v_ref, alr_ref, wup_ref, ba_ref,
                       o_ref, sout_ref, s_ref, b_ref, *, rows):
    c = pl.program_id(2)

    @pl.when(c == 0)
    def _():
        s_ref[...] = jnp.zeros_like(s_ref)

    g = _log_decay(alr_ref[...], wup_ref[...], ba_ref[...])
    rmod = lax.broadcasted_iota(jnp.int32, g.shape, 0) & (GLA_CHUNK - 1)
    step = 1
    while step < GLA_CHUNK:
        g = g + jnp.where(rmod >= step, pltpu.roll(g, step, 0), 0.0)
        step *= 2
    b_ref[...] = g

    row = lax.broadcasted_iota(jnp.int32, (GLA_CHUNK, GLA_DK), 0)
    rsub = row & (GLA_SUB - 1)
    ri = lax.broadcasted_iota(jnp.int32, (GLA_CHUNK, GLA_CHUNK), 0)
    cj = lax.broadcasted_iota(jnp.int32, (GLA_CHUNK, GLA_CHUNK), 1)
    n_sub = GLA_CHUNK // GLA_SUB

    def head_chunk(r0, hh):
        dk = slice(hh * GLA_DK, (hh + 1) * GLA_DK)
        dv = slice(hh * GLA_DV, (hh + 1) * GLA_DV)
        qc = q_ref[pl.ds(r0, GLA_CHUNK), dk] * (GLA_DK ** -0.5)
        kc = k_ref[pl.ds(r0, GLA_CHUNK), dk]
        vc = v_ref[pl.ds(r0, GLA_CHUNK), dv]
        b = b_ref[pl.ds(r0, GLA_CHUNK), dk]
        bl = b[GLA_CHUNK - 1:GLA_CHUNK, :]
        s = s_ref[hh]
        vb = vc.astype(BF16)

        o = jnp.dot((qc * jnp.exp(b)).astype(BF16), s.astype(BF16),
                    preferred_element_type=F32)

        parts = [jnp.zeros((GLA_SUB, GLA_CHUNK), F32)]
        for blk in range(1, n_sub):
            lo = blk * GLA_SUB
            ref = b[lo - 1:lo, :]
            qs = qc[lo:lo + GLA_SUB, :] * jnp.exp(b[lo:lo + GLA_SUB, :] - ref)
            ks = kc * jnp.exp(jnp.where(row < lo, ref - b, NEG_INF))
            parts.append(lax.dot_general(qs.astype(BF16), ks.astype(BF16),
                                         (((1,), (1,)), ((), ())),
                                         preferred_element_type=F32))
        attn = jnp.concatenate(parts, axis=0)

        for d in range(GLA_SUB):
            if d == 0:
                t = qc * kc
            else:
                kd = pltpu.roll(kc, d, 0)
                bd = pltpu.roll(b, d, 0)
                t = qc * kd * jnp.exp(jnp.where(rsub >= d, b - bd, NEG_INF))
            cs = jnp.sum(t, axis=1, keepdims=True)
            attn = attn + jnp.where(ri - cj == d, cs, 0.0)

        o = o + jnp.dot(attn.astype(BF16), vb, preferred_element_type=F32)
        o_ref[pl.ds(r0, GLA_CHUNK), dv] = o

        kdec = (kc * jnp.exp(bl - b)).astype(BF16)
        s_ref[hh] = (_col_scale(jnp.exp(bl), GLA_DV) * s
                     + lax.dot_general(kdec, vb, (((0,), (0,)), ((), ())),
                                       preferred_element_type=F32))

    def chunk(ci, carry):
        r0 = pl.multiple_of(ci * GLA_CHUNK, GLA_CHUNK)
        for hh in range(GLA_HEADS_PER_STEP):
            head_chunk(r0, hh)
        return carry

    lax.fori_loop(0, rows // GLA_CHUNK, chunk, 0)

    @pl.when(c == pl.num_programs(2) - 1)
    def _():
        sout_ref[0] = s_ref[...]


def _gla_prompt(p, alr, wup, ba, batch, seq, rows):
    nsteps = seq // rows
    hps = GLA_HEADS_PER_STEP
    wk = hps * GLA_DK
    wv = hps * GLA_DV
    kq = GLA_HEADS * GLA_DK // wk
    kv = 2 * GLA_HEADS * GLA_DK // wv

    def rowmap(b, h, c):
        return b * nsteps + c

    return pl.pallas_call(
        functools.partial(_gla_prompt_kernel, rows=rows),
        grid=(batch, GLA_HEADS // hps, nsteps),
        in_specs=[
            pl.BlockSpec((rows, wk), lambda b, h, c: (rowmap(b, h, c), h)),
            pl.BlockSpec((rows, wk), lambda b, h, c: (rowmap(b, h, c), kq + h)),
            pl.BlockSpec((rows, wv), lambda b, h, c: (rowmap(b, h, c), kv + h)),
            pl.BlockSpec((rows, LANES), lambda b, h, c: (rowmap(b, h, c), 0)),
            pl.BlockSpec((LANES, wk), lambda b, h, c: (0, h)),
            pl.BlockSpec((1, wk), lambda b, h, c: (0, h)),
        ],
        out_specs=[
            pl.BlockSpec((rows, wv), lambda b, h, c: (rowmap(b, h, c), h)),
            pl.BlockSpec((1, hps, GLA_DK, GLA_DV), lambda b, h, c: (b, h, 0, 0)),
        ],
        out_shape=[
            jax.ShapeDtypeStruct((batch * seq, GLA_HEADS * GLA_DV), F32),
            jax.ShapeDtypeStruct((batch, GLA_HEADS, GLA_DK, GLA_DV), F32),
        ],
        scratch_shapes=[pltpu.VMEM((hps, GLA_DK, GLA_DV), F32),
                        pltpu.VMEM((rows, wk), F32)],
        compiler_params=_params(("arbitrary", "arbitrary", "arbitrary")),
        name="gla_prompt",
    )(p, p, p, alr, wup, ba)


def _gla_sample_kernel(q_ref, k_ref, v_ref, alr_ref, wup_ref, ba_ref, s0_ref,
                       o_ref, s1_ref):
    first = lax.broadcasted_iota(jnp.int32, (SUBLANES, GLA_DK), 0) == 0
    for bb in range(q_ref.shape[0]):
        g_all = _log_decay(jnp.broadcast_to(alr_ref[bb], (SUBLANES, LANES)),
                           wup_ref[...], ba_ref[...])
        for h in range(GLA_HEADS):
            dk = slice(h * GLA_DK, (h + 1) * GLA_DK)
            dv = slice(h * GLA_DV, (h + 1) * GLA_DV)
            q = jnp.broadcast_to(q_ref[bb, :, dk], (SUBLANES, GLA_DK)) * (GLA_DK ** -0.5)
            k = jnp.broadcast_to(k_ref[bb, :, dk], (SUBLANES, GLA_DK))
            v = jnp.broadcast_to(v_ref[bb, :, dv], (SUBLANES, GLA_DV))
            g = g_all[:, dk]
            s = s0_ref[bb, h]
            attn = jnp.sum(q * k, axis=1, keepdims=True)
            o = attn * v + jnp.dot((q * jnp.exp(g)).astype(BF16), s.astype(BF16),
                                   preferred_element_type=F32)
            o_ref[bb, :, dv] = o[0:1, :]
            k1 = jnp.where(first, k, 0.0).astype(BF16)
            s1_ref[bb, h] = (_col_scale(jnp.exp(g[0:1, :]), GLA_DV) * s
                             + lax.dot_general(k1, v.astype(BF16), (((0,), (0,)), ((), ())),
                                               preferred_element_type=F32))


def _gla_sample(p3, alr3, wup, ba, s0, nb):
    n = p3.shape[0]
    qk = GLA_HEADS * GLA_DK
    vw = GLA_HEADS * GLA_DV
    return pl.pallas_call(
        _gla_sample_kernel,
        grid=(n // nb,),
        in_specs=[
            pl.BlockSpec((nb, 1, qk), lambda b: (b, 0, 0)),
            pl.BlockSpec((nb, 1, qk), lambda b: (b, 0, 1)),
            pl.BlockSpec((nb, 1, vw), lambda b: (b, 0, 2 * qk // vw)),
            pl.BlockSpec((nb, 1, LANES), lambda b: (b, 0, 0)),
            pl.BlockSpec((LANES, qk), lambda b: (0, 0)),
            pl.BlockSpec((1, qk), lambda b: (0, 0)),
            pl.BlockSpec((nb, GLA_HEADS, GLA_DK, GLA_DV), lambda b: (b, 0, 0, 0)),
        ],
        out_specs=[
            pl.BlockSpec((nb, 1, vw), lambda b: (b, 0, 0)),
            pl.BlockSpec((nb, GLA_HEADS, GLA_DK, GLA_DV), lambda b: (b, 0, 0, 0)),
        ],
        out_shape=[
            jax.ShapeDtypeStruct((n, 1, vw), F32),
            jax.ShapeDtypeStruct((n, GLA_HEADS, GLA_DK, GLA_DV), F32),
        ],
        compiler_params=_params(("arbitrary",)),
        name="gla_sample",
    )(p3, p3, p3, alr3, wup, ba, s0)


def _merge_kernel(o_ref, r_ref, cb_ref, cc_ref, cx_ref, ga_ref, gb_ref, pa_ref, pb_ref,
                  gg_ref, cw_ref, cbias_ref, m_ref, u_ref, *, prompt, tiles_per_seq):
    u = cc_ref[...] * cx_ref[...]
    tm = u.shape[0]
    if prompt:
        start = (pl.program_id(0) % tiles_per_seq) == 0
        prev = jnp.where(start, 0.0, pa_ref[...] * pb_ref[...])
        row = lax.broadcasted_iota(jnp.int32, (tm, D_MODEL), 0)
        um1 = jnp.where(row == 0, prev[7:8, :], pltpu.roll(u, 1, 0))
        um2 = jnp.where(row == 0, prev[6:7, :],
                        jnp.where(row == 1, prev[7:8, :], pltpu.roll(u, 2, 0)))
        u_ref[0] = u[tm - SUBLANES:, :]
    else:
        um2 = pa_ref[...]
        um1 = pb_ref[...]
        u_ref[...] = u
    z = cbias_ref[...] + um2 * cw_ref[0:1, :] + um1 * cw_ref[1:2, :] + u * cw_ref[2:3, :]
    y_b = cb_ref[...] * z
    r = r_ref[...]
    sr = r * jax.nn.sigmoid(r)
    m = jax.nn.sigmoid(gb_ref[...]) * y_b
    sga = jax.nn.sigmoid(ga_ref[...])
    for h in range(GLA_HEADS):
        dv = slice(h * GLA_DV, (h + 1) * GLA_DV)
        on = _rms(o_ref[:, dv], gg_ref[:, dv])
        m_ref[:, dv] = (m[:, dv] + sga[:, dv] * (sr[:, dv] * on)).astype(BF16)


def _merge(o, p, pa, pb, gg, cw, cbias, tm, prompt, seq):
    t = o.shape[0]
    n_tiles = t // tm
    col = lambda c: pl.BlockSpec((tm, D_MODEL), lambda i: (i, c))
    if prompt:
        prev_rows = tm // SUBLANES
        prev = lambda c: pl.BlockSpec(
            (SUBLANES, D_MODEL), lambda i: (jnp.maximum(i * prev_rows - 1, 0), c))
        pspecs = [prev(4), prev(5)]
        uspec = pl.BlockSpec((1, SUBLANES, D_MODEL), lambda i: (i, 0, 0))
        ushape = jax.ShapeDtypeStruct((n_tiles, SUBLANES, D_MODEL), F32)
    else:
        pspecs = [pl.BlockSpec((tm, D_MODEL), lambda i: (i, 0))] * 2
        uspec = pl.BlockSpec((tm, D_MODEL), lambda i: (i, 0))
        ushape = jax.ShapeDtypeStruct((t, D_MODEL), F32)
    vec = lambda r: pl.BlockSpec((r, D_MODEL), lambda i: (0, 0))
    return pl.pallas_call(
        functools.partial(_merge_kernel, prompt=prompt,
                          tiles_per_seq=max(seq // tm, 1)),
        grid=(n_tiles,),
        in_specs=[pl.BlockSpec((tm, D_MODEL), lambda i: (i, 0)),
                  col(2), col(3), col(4), col(5), col(6), col(7)] + pspecs
                 + [vec(1), vec(CONV_WIDTH), vec(1)],
        out_specs=[pl.BlockSpec((tm, D_MODEL), lambda i: (i, 0)), uspec],
        out_shape=[jax.ShapeDtypeStruct((t, D_MODEL), BF16), ushape],
        compiler_params=_params(("arbitrary",)),
        name="merge_prompt" if prompt else "merge_sample",
    )(o, p, p, p, p, p, p, pa, pb, gg, cw, cbias)


def _outproj_kernel(x_ref, m_ref, wo_ref, g2_ref, wq_ref, xm_ref, h2t_ref, qp_ref):
    xm = x_ref[...] + jnp.dot(m_ref[...], wo_ref[...], preferred_element_type=F32)
    xm_ref[...] = xm
    h2 = _rms(xm, g2_ref[...])
    h2t_ref[...] = jnp.transpose(h2).astype(BF16)
    qp_ref[...] = jnp.dot(h2.astype(BF16), wq_ref[...], preferred_element_type=F32)


def _outproj(x, m, wo, g2, wq, tm):
    t = x.shape[0]
    tile = pl.BlockSpec((tm, D_MODEL), lambda i: (i, 0))
    whole = pl.BlockSpec((D_MODEL, D_MODEL), lambda i: (0, 0))
    return pl.pallas_call(
        _outproj_kernel,
        grid=(t // tm,),
        in_specs=[tile, tile, whole, pl.BlockSpec((1, D_MODEL), lambda i: (0, 0)), whole],
        out_specs=[tile, pl.BlockSpec((D_MODEL, tm), lambda i: (0, i)), tile],
        out_shape=[jax.ShapeDtypeStruct((t, D_MODEL), F32),
                   jax.ShapeDtypeStruct((D_MODEL, t), BF16),
                   jax.ShapeDtypeStruct((t, D_MODEL), F32)],
        compiler_params=_params(("arbitrary",)),
        name="outproj",
    )(x, m, wo, g2, wq)


def _top_values(work, n):
    vals = []
    rank = jnp.full(work.shape, float(n), F32)
    for r in range(n):
        mx = jnp.max(work, axis=0, keepdims=True)
        hit = work == mx
        vals.append(mx)
        rank = jnp.where(hit, float(r), rank)
        work = jnp.where(hit, NEG_INF, work)
    return vals, rank


def _route_kernel(qp_ref, keys_ref, n1_ref, p1_ref, rank2_ref, p2_ref):
    for h in range(PEER_HEADS):
        st = []
        for side in range(2):
            c0 = (2 * h + side) * DQ_HALF
            st.append(lax.dot_general(keys_ref[side], qp_ref[:, c0:c0 + DQ_HALF],
                                      (((1,), (1,)), ((), ())),
                                      precision=lax.Precision.HIGHEST,
                                      preferred_element_type=F32))
        v1, rank1 = _top_values(st[0], PEER_TOPK)
        v2, rank2 = _top_values(st[1], PEER_TOPK)
        v2 = jnp.concatenate(v2, axis=0)
        tm = v2.shape[1]
        bidx = lax.broadcasted_iota(jnp.int32, (SUBLANES, tm), 0)
        cand = []
        for a in range(PEER_TOPK // 2):
            nb = PEER_TOPK // (a + 1)
            if nb >= SUBLANES:
                cand.append(v1[a] + v2[0:nb, :])
            else:
                cand.append(jnp.where(bidx < nb, v1[a] + v2[0:SUBLANES, :], NEG_INF))
        tail = jnp.concatenate(v1[PEER_TOPK // 2:], axis=0) + v2[0:1, :]
        cand.append(tail)
        thr = _top_values(jnp.concatenate(cand, axis=0), PEER_TOPK)[0][-1]
        m1 = v1[0]
        m2 = v2[0:1, :]
        z = jnp.zeros_like(m1)
        n1 = jnp.zeros_like(rank1)
        for a, ca in enumerate(cand):
            keep = ca >= thr
            z = z + jnp.sum(jnp.where(keep, jnp.exp(ca - (m1 + m2)), 0.0),
                            axis=0, keepdims=True)
            kept = jnp.where(keep, 1.0, 0.0)
            if a < PEER_TOPK // 2:
                n1 = jnp.where(rank1 == float(a), jnp.sum(kept, axis=0, keepdims=True), n1)
            else:
                for r in range(PEER_TOPK // 2):
                    n1 = jnp.where(rank1 == float(a + r), kept[r:r + 1, :], n1)
        n1_ref[h] = n1
        p1_ref[h] = jnp.exp(st[0] - m1)
        rank2_ref[h] = rank2.astype(BF16)
        p2_ref[h] = (jnp.exp(st[1] - m2) / z).astype(BF16)


def _route(qp, keys, tm):
    t = qp.shape[0]
    big = pl.BlockSpec((PEER_HEADS, N_KEYS, tm), lambda i: (0, 0, i))
    f32s = jax.ShapeDtypeStruct((PEER_HEADS, N_KEYS, t), F32)
    bf16s = jax.ShapeDtypeStruct((PEER_HEADS, N_KEYS, t), BF16)
    return pl.pallas_call(
        _route_kernel,
        grid=(t // tm,),
        in_specs=[pl.BlockSpec((tm, D_MODEL), lambda i: (i, 0)),
                  pl.BlockSpec((2, N_KEYS, DQ_HALF), lambda i: (0, 0, 0))],
        out_specs=[big, big, big, big],
        out_shape=[f32s, f32s, bf16s, bf16s],
        compiler_params=_params(("arbitrary",)),
        name="route",
    )(qp, keys)


def _row_bf16(ref, h, e1, tm):
    row = jnp.broadcast_to(ref[h, pl.ds(e1, 1), :], (BF16_ROWS, tm)).astype(BF16)
    return jnp.concatenate([row] * (N_KEYS // BF16_ROWS), axis=0)


def _peer_kernel(h2t_ref, u_ref, v_ref, n1_ref, p1_ref, rank2_ref, p2_ref,
                 xm_ref, gf_ref, out_ref, *, te):
    j = pl.program_id(1)

    @pl.when(j == 0)
    def _():
        out_ref[...] = jnp.zeros_like(out_ref)

    tm = h2t_ref.shape[1]
    rows_per_part = PEER_PART_ROWS
    acc = None
    for part in range(te // rows_per_part):
        lo = part * rows_per_part
        a = jnp.dot(u_ref[lo:lo + rows_per_part, :], h2t_ref[...],
                    preferred_element_type=F32)
        blocks = []
        for l in range(rows_per_part // N_KEYS):
            e1 = j * (te // N_KEYS) + part * (rows_per_part // N_KEYS) + l
            g = jnp.zeros((N_KEYS, tm), BF16)
            for h in range(PEER_HEADS):
                keep = rank2_ref[h] < _row_bf16(n1_ref, h, e1, tm)
                g = g + jnp.where(keep, p2_ref[h], 0.0) * _row_bf16(p1_ref, h, e1, tm)
            al = a[l * N_KEYS:(l + 1) * N_KEYS, :]
            gelu = 0.5 * al * (1.0 + lax.erf(al * INV_SQRT2))
            blocks.append(gelu.astype(BF16) * g)
        wt = jnp.concatenate(blocks, axis=0) if len(blocks) > 1 else blocks[0]
        d = jnp.dot(jnp.transpose(wt), v_ref[lo:lo + rows_per_part, :],
                    preferred_element_type=F32)
        acc = d if acc is None else acc + d
    out_ref[...] += acc

    @pl.when(j == pl.num_programs(1) - 1)
    def _():
        out_ref[...] = _rms(xm_ref[...] + out_ref[...], gf_ref[...])


def _peer(h2t, u_bf, v_bf, n1, p1, rank2, p2, xm, gf, tm, te):
    t = xm.shape[0]
    big = pl.BlockSpec((PEER_HEADS, N_KEYS, tm), lambda i, j: (0, 0, i))
    return pl.pallas_call(
        functools.partial(_peer_kernel, te=te),
        grid=(t // tm, N_EXPERTS // te),
        in_specs=[pl.BlockSpec((D_MODEL, tm), lambda i, j: (0, i)),
                  pl.BlockSpec((te, D_MODEL), lambda i, j: (j, 0)),
                  pl.BlockSpec((te, D_MODEL), lambda i, j: (j, 0)),
                  big, big, big, big,
                  pl.BlockSpec((tm, D_MODEL), lambda i, j: (i, 0)),
                  pl.BlockSpec((1, D_MODEL), lambda i, j: (0, 0))],
        out_specs=pl.BlockSpec((tm, D_MODEL), lambda i, j: (i, 0)),
        out_shape=jax.ShapeDtypeStruct((t, D_MODEL), F32),
        compiler_params=_params(("arbitrary", "arbitrary")),
        name="peer",
    )(h2t, u_bf, v_bf, n1, p1, rank2, p2, xm, gf)


def _channel_mixer(x, m, wts, tm_proj, tm_route, tm_peer, te):
    xm, h2t, qp = _outproj(x, m, wts["wo"], wts["g2"], wts["wq"], tm_proj)
    n1, p1, rank2, p2 = _route(qp, wts["keys"], tm_route)
    return _peer(h2t, wts["u"], wts["v"], n1, p1, rank2, p2, xm, wts["gf"], tm_peer, te)


def kernel(x_prompt, x_sample, state_gla, state_conv, norm1_g, w_in, w_alpha_up, b_alpha,
           gla_norm_g, conv_w, conv_b, w_out, norm2_g, w_query, sub_keys, expert_u, expert_v,
           norm_f_g):
    batch, seq, _ = x_prompt.shape
    n_dec = x_sample.shape[0]
    assert w_in.shape[0] == 1 and x_sample.shape[1] == 1

    alr0 = 2 * GLA_HEADS * GLA_DK + 2 * GLA_HEADS * GLA_DV
    w_lo = w_in[0][:, :alr0].astype(BF16)
    w_hi = w_in[0][:, alr0 + GLA_RANK:].astype(BF16)
    w_alr = jnp.pad(w_in[0][:, alr0:alr0 + GLA_RANK],
                    ((0, 0), (0, LANES - GLA_RANK))).astype(BF16)
    wup = jnp.pad(w_alpha_up[0], ((0, LANES - GLA_RANK), (0, 0))).astype(BF16)
    ba = b_alpha[0][None, :]
    g1 = norm1_g[0][None, :]
    gg = gla_norm_g[0][None, :]
    cw = conv_w[0]
    cbias = conv_b[0][None, :]
    wts = dict(wo=w_out[0].astype(BF16), g2=norm2_g[0][None, :],
               wq=w_query[0].astype(BF16), keys=sub_keys[0],
               u=expert_u[0].astype(BF16), v=expert_v[0].astype(BF16),
               gf=norm_f_g[None, :])

    xp = x_prompt.reshape(batch * seq, D_MODEL)
    p, alr = _inproj(xp, g1, w_lo, w_hi, w_alr, tm=1024, tn=1024)
    o, s_p = _gla_prompt(p, alr, wup, ba, batch, seq, rows=256)
    tm_merge = 256
    m, utail = _merge(o, p, p, p, gg, cw, cbias, tm_merge, True, seq)
    y_p = _channel_mixer(xp, m, wts, tm_proj=256, tm_route=256, tm_peer=512, te=1024)
    last = utail.reshape(batch, seq // tm_merge, SUBLANES, D_MODEL)[:, -1, SUBLANES - 2:, :]

    xs = x_sample.reshape(n_dec, D_MODEL)
    ps, alrs = _inproj(xs, g1, w_lo, w_hi, w_alr, tm=n_dec, tn=1024)
    os_, s_s = _gla_sample(ps.reshape(n_dec, 1, N_MAIN), alrs.reshape(n_dec, 1, LANES),
                           wup, ba, state_gla[0], nb=2)
    ms, us = _merge(os_.reshape(n_dec, D_MODEL), ps, state_conv[0][:, 0, :],
                    state_conv[0][:, 1, :], gg, cw, cbias, n_dec, False, 1)
    y_s = _channel_mixer(xs, ms, wts, tm_proj=n_dec, tm_route=n_dec, tm_peer=n_dec, te=512)
    conv_s = jnp.stack([state_conv[0][:, 1, :], us], axis=1)

    return (y_p.reshape(batch, seq, D_MODEL),
            y_s.reshape(n_dec, 1, D_MODEL),
            s_p[None],
            last[None],
            s_s[None],
            conv_s[None])
```

```python
import functools
import math

import jax
import jax.numpy as jnp
from jax import lax
from jax.experimental import pallas as pl
from jax.experimental.pallas import tpu as pltpu

F32 = jnp.float32
BF16 = jnp.bfloat16

D_MODEL = 2048
GLA_HEADS = 4
GLA_DK = 256
GLA_DV = 512
GLA_RANK = 16
GLA_TAU = 16.0
GLA_CHUNK = 64
CONV_WIDTH = 3
PEER_HEADS = 8
N_KEYS = 128
N_EXPERTS = N_KEYS * N_KEYS
PEER_TOPK = 16
DQ_HALF = 128
EPS = 1e-6

LANES = 128
SUBLANES = 8
BF16_ROWS = 16
VMEM_LIMIT_BYTES = 60000 * 1024

N_MAIN = 16384
P_SEGMENT = 4096
GLA_SUB = 8
GLA_HEADS_PER_STEP = 4
PEER_PART_ROWS = 256
NEG_INF = float("-inf")
INV_SQRT2 = 1.0 / math.sqrt(2.0)


def _params(semantics):
    return pltpu.CompilerParams(dimension_semantics=semantics,
                                vmem_limit_bytes=VMEM_LIMIT_BYTES)


def _rms(x, g):
    return x * lax.rsqrt(jnp.mean(x * x, axis=-1, keepdims=True) + EPS) * g


def _inproj_kernel(x_ref, g_ref, wlo_ref, whi_ref, walr_ref, p32_ref, p16_ref, alr_ref,
                   h_ref, *, n_lo, tiles_per_seg):
    j = pl.program_id(1)

    @pl.when(j == 0)
    def _():
        hb = _rms(x_ref[...], g_ref[...]).astype(BF16)
        h_ref[...] = hb
        alr_ref[...] = jnp.dot(hb, walr_ref[...], preferred_element_type=F32)

    wide = (j // tiles_per_seg) % 2 == 0
    low = j < n_lo
    for w_ref, from_low in ((wlo_ref, True), (whi_ref, False)):
        for out_ref, is_wide in ((p32_ref, True), (p16_ref, False)):
            @pl.when((low == from_low) & (wide == is_wide))
            def _():
                out_ref[...] = jnp.dot(h_ref[...], w_ref[...],
                                       preferred_element_type=F32).astype(out_ref.dtype)


def _inproj(x, g, w_lo, w_hi, w_alr, tm, tn):
    t = x.shape[0]
    n_lo = w_lo.shape[1] // tn
    per = P_SEGMENT // tn

    def idx32(j):
        seg, r = j // per, j % per
        return jnp.where(seg % 2 == 0, (seg // 2) * per + r, (seg // 2 + 1) * per - 1)

    def idx16(j):
        seg, r = j // per, j % per
        return jnp.where(seg % 2 == 1, (seg // 2) * per + r,
                         jnp.maximum((seg // 2) * per - 1, 0))

    return pl.pallas_call(
        functools.partial(_inproj_kernel, n_lo=n_lo, tiles_per_seg=per),
        grid=(t // tm, N_MAIN // tn),
        in_specs=[
            pl.BlockSpec((tm, D_MODEL), lambda i, j: (i, 0)),
            pl.BlockSpec((1, D_MODEL), lambda i, j: (0, 0)),
            pl.BlockSpec((D_MODEL, tn), lambda i, j: (0, jnp.minimum(j, n_lo - 1))),
            pl.BlockSpec((D_MODEL, tn), lambda i, j: (0, jnp.maximum(j - n_lo, 0))),
            pl.BlockSpec((D_MODEL, LANES), lambda i, j: (0, 0)),
        ],
        out_specs=[
            pl.BlockSpec((tm, tn), lambda i, j: (i, idx32(j))),
            pl.BlockSpec((tm, tn), lambda i, j: (i, idx16(j))),
            pl.BlockSpec((tm, LANES), lambda i, j: (i, 0)),
        ],
        out_shape=[
            jax.ShapeDtypeStruct((t, N_MAIN // 2), F32),
            jax.ShapeDtypeStruct((t, N_MAIN // 2), BF16),
            jax.ShapeDtypeStruct((t, LANES), F32),
        ],
        scratch_shapes=[pltpu.VMEM((tm, D_MODEL), BF16)],
        compiler_params=_params(("arbitrary", "arbitrary")),
        name="inproj",
    )(x, g, w_lo, w_hi, w_alr)


def _log_decay(alr, wup, ba):
    xa = jnp.dot(alr.astype(BF16), wup, preferred_element_type=F32) + ba
    return -(jnp.maximum(-xa, 0.0) + jnp.log1p(jnp.exp(-jnp.abs(xa)))) * (1.0 / GLA_TAU)


def _col_scale(row, width):
    col = jnp.transpose(jnp.broadcast_to(row, (LANES, row.shape[1])))
    return jnp.concatenate([col] * (width // LANES), axis=1)


def _gla_prompt_kernel(q_ref, k_ref, v_ref, alr_ref, wup_ref, ba_ref,
                       o_ref, sout_ref, s_ref, b_ref, *, rows):
    c = pl.program_id(2)

    @pl.when(c == 0)
    def _():
        s_ref[...] = jnp.zeros_like(s_ref)

    g = _log_decay(alr_ref[...], wup_ref[...], ba_ref[...])
    rmod = lax.broadcasted_iota(jnp.int32, g.shape, 0) & (GLA_CHUNK - 1)
    step = 1
    while step < GLA_CHUNK:
        g = g + jnp.where(rmod >= step, pltpu.roll(g, step, 0), 0.0)
        step *= 2
    b_ref[...] = g

    row = lax.broadcasted_iota(jnp.int32, (GLA_CHUNK, GLA_DK), 0)
    rsub = row & (GLA_SUB - 1)
    ri = lax.broadcasted_iota(jnp.int32, (GLA_CHUNK, GLA_CHUNK), 0)
    cj = lax.broadcasted_iota(jnp.int32, (GLA_CHUNK, GLA_CHUNK), 1)
    n_sub = GLA_CHUNK // GLA_SUB

    def head_chunk(r0, hh):
        dk = slice(hh * GLA_DK, (hh + 1) * GLA_DK)
        dv = slice(hh * GLA_DV, (hh + 1) * GLA_DV)
        qc = q_ref[pl.ds(r0, GLA_CHUNK), dk] * (GLA_DK ** -0.5)
        kc = k_ref[pl.ds(r0, GLA_CHUNK), dk]
        vc = v_ref[pl.ds(r0, GLA_CHUNK), dv]
        b = b_ref[pl.ds(r0, GLA_CHUNK), dk]
        bl = b[GLA_CHUNK - 1:GLA_CHUNK, :]
        s = s_ref[hh]
        vb = vc.astype(BF16)

        o = jnp.dot((qc * jnp.exp(b)).astype(BF16), s.astype(BF16),
                    preferred_element_type=F32)

        parts = [jnp.zeros((GLA_SUB, GLA_CHUNK), F32)]
        for blk in range(1, n_sub):
            lo = blk * GLA_SUB
            ref = b[lo - 1:lo, :]
            qs = qc[lo:lo + GLA_SUB, :] * jnp.exp(b[lo:lo + GLA_SUB, :] - ref)
            ks = kc * jnp.exp(jnp.where(row < lo, ref - b, NEG_INF))
            parts.append(lax.dot_general(qs.astype(BF16), ks.astype(BF16),
                                         (((1,), (1,)), ((), ())),
                                         preferred_element_type=F32))
        attn = jnp.concatenate(parts, axis=0)

        for d in range(GLA_SUB):
            if d == 0:
                t = qc * kc
            else:
                kd = pltpu.roll(kc, d, 0)
                bd = pltpu.roll(b, d, 0)
                t = qc * kd * jnp.exp(jnp.where(rsub >= d, b - bd, NEG_INF))
            cs = jnp.sum(t, axis=1, keepdims=True)
            attn = attn + jnp.where(ri - cj == d, cs, 0.0)

        o = o + jnp.dot(attn.astype(BF16), vb, preferred_element_type=F32)
        o_ref[pl.ds(r0, GLA_CHUNK), dv] = o

        kdec = (kc * jnp.exp(bl - b)).astype(BF16)
        s_ref[hh] = (_col_scale(jnp.exp(bl), GLA_DV) * s
                     + lax.dot_general(kdec, vb, (((0,), (0,)), ((), ())),
                                       preferred_element_type=F32))

    def chunk(ci, carry):
        r0 = pl.multiple_of(ci * GLA_CHUNK, GLA_CHUNK)
        for hh in range(GLA_HEADS_PER_STEP):
            head_chunk(r0, hh)
        return carry

    lax.fori_loop(0, rows // GLA_CHUNK, chunk, 0)

    @pl.when(c == pl.num_programs(2) - 1)
    def _():
        sout_ref[0] = s_ref[...]


def _gla_prompt(p, alr, wup, ba, batch, seq, rows):
    nsteps = seq // rows
    hps = GLA_HEADS_PER_STEP
    wk = hps * GLA_DK
    wv = hps * GLA_DV
    kq = GLA_HEADS * GLA_DK // wk
    kv = 2 * GLA_HEADS * GLA_DK // wv

    def rowmap(b, h, c):
        return b * nsteps + c

    return pl.pallas_call(
        functools.partial(_gla_prompt_kernel, rows=rows),
        grid=(batch, GLA_HEADS // hps, nsteps),
        in_specs=[
            pl.BlockSpec((rows, wk), lambda b, h, c: (rowmap(b, h, c), h)),
            pl.BlockSpec((rows, wk), lambda b, h, c: (rowmap(b, h, c), kq + h)),
            pl.BlockSpec((rows, wv), lambda b, h, c: (rowmap(b, h, c), kv + h)),
            pl.BlockSpec((rows, LANES), lambda b, h, c: (rowmap(b, h, c), 0)),
            pl.BlockSpec((LANES, wk), lambda b, h, c: (0, h)),
            pl.BlockSpec((1, wk), lambda b, h, c: (0, h)),
        ],
        out_specs=[
            pl.BlockSpec((rows, wv), lambda b, h, c: (rowmap(b, h, c), h)),
            pl.BlockSpec((1, hps, GLA_DK, GLA_DV), lambda b, h, c: (b, h, 0, 0)),
        ],
        out_shape=[
            jax.ShapeDtypeStruct((batch * seq, GLA_HEADS * GLA_DV), F32),
            jax.ShapeDtypeStruct((batch, GLA_HEADS, GLA_DK, GLA_DV), F32),
        ],
        scratch_shapes=[pltpu.VMEM((hps, GLA_DK, GLA_DV), F32),
                        pltpu.VMEM((rows, wk), F32)],
        compiler_params=_params(("arbitrary", "arbitrary", "arbitrary")),
        name="gla_prompt",
    )(p, p, p, alr, wup, ba)


def _gla_sample_kernel(q_ref, k_ref, v_ref, alr_ref, wup_ref, ba_ref, s0_ref,
                       o_ref, s1_ref):
    first = lax.broadcasted_iota(jnp.int32, (SUBLANES, GLA_DK), 0) == 0
    for bb in range(q_ref.shape[0]):
        g_all = _log_decay(jnp.broadcast_to(alr_ref[bb], (SUBLANES, LANES)),
                           wup_ref[...], ba_ref[...])
        for h in range(GLA_HEADS):
            dk = slice(h * GLA_DK, (h + 1) * GLA_DK)
            dv = slice(h * GLA_DV, (h + 1) * GLA_DV)
            q = jnp.broadcast_to(q_ref[bb, :, dk], (SUBLANES, GLA_DK)) * (GLA_DK ** -0.5)
            k = jnp.broadcast_to(k_ref[bb, :, dk], (SUBLANES, GLA_DK))
            v = jnp.broadcast_to(v_ref[bb, :, dv], (SUBLANES, GLA_DV))
            g = g_all[:, dk]
            s = s0_ref[bb, h]
            attn = jnp.sum(q * k, axis=1, keepdims=True)
            o = attn * v + jnp.dot((q * jnp.exp(g)).astype(BF16), s.astype(BF16),
                                   preferred_element_type=F32)
            o_ref[bb, :, dv] = o[0:1, :]
            k1 = jnp.where(first, k, 0.0).astype(BF16)
            s1_ref[bb, h] = (_col_scale(jnp.exp(g[0:1, :]), GLA_DV) * s
                             + lax.dot_general(k1, v.astype(BF16), (((0,), (0,)), ((), ())),
                                               preferred_element_type=F32))


def _gla_sample(p3, alr3, wup, ba, s0, nb):
    n = p3.shape[0]
    qk = GLA_HEADS * GLA_DK
    vw = GLA_HEADS * GLA_DV
    return pl.pallas_call(
        _gla_sample_kernel,
        grid=(n // nb,),
        in_specs=[
            pl.BlockSpec((nb, 1, qk), lambda b: (b, 0, 0)),
            pl.BlockSpec((nb, 1, qk), lambda b: (b, 0, 1)),
            pl.BlockSpec((nb, 1, vw), lambda b: (b, 0, 2 * qk // vw)),
            pl.BlockSpec((nb, 1, LANES), lambda b: (b, 0, 0)),
            pl.BlockSpec((LANES, qk), lambda b: (0, 0)),
            pl.BlockSpec((1, qk), lambda b: (0, 0)),
            pl.BlockSpec((nb, GLA_HEADS, GLA_DK, GLA_DV), lambda b: (b, 0, 0, 0)),
        ],
        out_specs=[
            pl.BlockSpec((nb, 1, vw), lambda b: (b, 0, 0)),
            pl.BlockSpec((nb, GLA_HEADS, GLA_DK, GLA_DV), lambda b: (b, 0, 0, 0)),
        ],
        out_shape=[
            jax.ShapeDtypeStruct((n, 1, vw), F32),
            jax.ShapeDtypeStruct((n, GLA_HEADS, GLA_DK, GLA_DV), F32),
        ],
        compiler_params=_params(("arbitrary",)),
        name="gla_sample",
    )(p3, p3, p3, alr3, wup, ba, s0)


def _mix_kernel(o_ref, r_ref, cb_ref, cc_ref, cx_ref, ga_ref, gb_ref, pa_ref, pb_ref,
                gg_ref, cw_ref, cbias_ref, x_ref, wo_ref, g2_ref, wq_ref,
                xm_ref, h2t_ref, qp_ref, u_ref, m_ref, *, prompt, tiles_per_seq):
    tm = x_ref.shape[0]
    if prompt:
        start = (pl.program_id(0) % tiles_per_seq) == 0
        row = lax.broadcasted_iota(jnp.int32, (tm, GLA_DV), 0)
    for h in range(GLA_HEADS):
        dv = slice(h * GLA_DV, (h + 1) * GLA_DV)
        u = cc_ref[:, dv] * cx_ref[:, dv]
        if prompt:
            prev = jnp.where(start, 0.0, pa_ref[:, dv] * pb_ref[:, dv])
            um1 = jnp.where(row == 0, prev[7:8, :], pltpu.roll(u, 1, 0))
            um2 = jnp.where(row == 0, prev[6:7, :],
                            jnp.where(row == 1, prev[7:8, :], pltpu.roll(u, 2, 0)))
            u_ref[0, :, dv] = u[tm - SUBLANES:, :]
        else:
            um2 = pa_ref[:, dv]
            um1 = pb_ref[:, dv]
            u_ref[:, dv] = u
        z = (cbias_ref[:, dv] + um2 * cw_ref[0:1, dv] + um1 * cw_ref[1:2, dv]
             + u * cw_ref[2:3, dv])
        y_b = cb_ref[:, dv].astype(F32) * z
        r = r_ref[:, dv].astype(F32)
        on = _rms(o_ref[:, dv], gg_ref[:, dv])
        m = (jax.nn.sigmoid(gb_ref[:, dv].astype(F32)) * y_b
             + jax.nn.sigmoid(ga_ref[:, dv].astype(F32)) * (r * jax.nn.sigmoid(r) * on))
        m_ref[:, dv] = m.astype(BF16)
    xm = x_ref[...] + jnp.dot(m_ref[...], wo_ref[...], preferred_element_type=F32)
    xm_ref[...] = xm
    h2 = _rms(xm, g2_ref[...])
    h2t_ref[...] = jnp.transpose(h2).astype(BF16)
    qp_ref[...] = jnp.dot(h2.astype(BF16), wq_ref[...], preferred_element_type=F32)


def _mix(o, p32, p16, pa, pb, x, wts, tm, prompt, seq):
    t = o.shape[0]
    n_tiles = t // tm
    tile = pl.BlockSpec((tm, D_MODEL), lambda i: (i, 0))
    col = lambda c: pl.BlockSpec((tm, D_MODEL), lambda i: (i, c))
    cc_col = P_SEGMENT // D_MODEL
    if prompt:
        prev_rows = tm // SUBLANES
        prev = lambda c: pl.BlockSpec(
            (SUBLANES, D_MODEL), lambda i: (jnp.maximum(i * prev_rows - 1, 0), c))
        pspecs = [prev(cc_col), prev(cc_col + 1)]
        uspec = pl.BlockSpec((1, SUBLANES, D_MODEL), lambda i: (i, 0, 0))
        ushape = jax.ShapeDtypeStruct((n_tiles, SUBLANES, D_MODEL), F32)
    else:
        pspecs = [tile, tile]
        uspec = tile
        ushape = jax.ShapeDtypeStruct((t, D_MODEL), F32)
    vec = lambda r: pl.BlockSpec((r, D_MODEL), lambda i: (0, 0))
    whole = pl.BlockSpec((D_MODEL, D_MODEL), lambda i: (0, 0),
                         pipeline_mode=pl.Buffered(1))
    return pl.pallas_call(
        functools.partial(_mix_kernel, prompt=prompt, tiles_per_seq=max(seq // tm, 1)),
        grid=(n_tiles,),
        in_specs=[tile, col(0), col(1), col(cc_col), col(cc_col + 1), col(2), col(3)]
                 + pspecs + [vec(1), vec(CONV_WIDTH), vec(1), tile, whole, vec(1), whole],
        out_specs=[tile, pl.BlockSpec((D_MODEL, tm), lambda i: (0, i)), tile, uspec],
        out_shape=[jax.ShapeDtypeStruct((t, D_MODEL), F32),
                   jax.ShapeDtypeStruct((D_MODEL, t), BF16),
                   jax.ShapeDtypeStruct((t, D_MODEL), F32),
                   ushape],
        scratch_shapes=[pltpu.VMEM((tm, D_MODEL), BF16)],
        compiler_params=_params(("arbitrary",)),
        name="mix_prompt" if prompt else "mix_sample",
    )(o, p16, p16, p32, p32, p16, p16, pa, pb, wts["gg"], wts["cw"], wts["cbias"],
      x, wts["wo"], wts["g2"], wts["wq"])


def _top_values(work, n):
    vals = []
    rank = jnp.full(work.shape, float(n), F32)
    for r in range(n):
        mx = jnp.max(work, axis=0, keepdims=True)
        hit = work == mx
        vals.append(mx)
        rank = jnp.where(hit, float(r), rank)
        work = jnp.where(hit, NEG_INF, work)
    return vals, rank


def _route_kernel(qp_ref, keys_ref, n1_ref, p1_ref, rank2_ref, p2_ref):
    for h in range(PEER_HEADS):
        st = []
        for side in range(2):
            c0 = (2 * h + side) * DQ_HALF
            st.append(lax.dot_general(keys_ref[side], qp_ref[:, c0:c0 + DQ_HALF],
                                      (((1,), (1,)), ((), ())),
                                      precision=lax.Precision.HIGHEST,
                                      preferred_element_type=F32))
        v1, rank1 = _top_values(st[0], PEER_TOPK)
        v2, rank2 = _top_values(st[1], PEER_TOPK)
        v2 = jnp.concatenate(v2, axis=0)
        tm = v2.shape[1]
        bidx = lax.broadcasted_iota(jnp.int32, (SUBLANES, tm), 0)
        cand = []
        for a in range(PEER_TOPK // 2):
            nb = PEER_TOPK // (a + 1)
            if nb >= SUBLANES:
                cand.append(v1[a] + v2[0:nb, :])
            else:
                cand.append(jnp.where(bidx < nb, v1[a] + v2[0:SUBLANES, :], NEG_INF))
        tail = jnp.concatenate(v1[PEER_TOPK // 2:], axis=0) + v2[0:1, :]
        cand.append(tail)
        thr = _top_values(jnp.concatenate(cand, axis=0), PEER_TOPK)[0][-1]
        m1 = v1[0]
        m2 = v2[0:1, :]
        z = jnp.zeros_like(m1)
        n1 = jnp.zeros_like(rank1)
        for a, ca in enumerate(cand):
            keep = ca >= thr
            z = z + jnp.sum(jnp.where(keep, jnp.exp(ca - (m1 + m2)), 0.0),
                            axis=0, keepdims=True)
            kept = jnp.where(keep, 1.0, 0.0)
            if a < PEER_TOPK // 2:
                n1 = jnp.where(rank1 == float(a), jnp.sum(kept, axis=0, keepdims=True), n1)
            else:
                for r in range(PEER_TOPK // 2):
                    n1 = jnp.where(rank1 == float(a + r), kept[r:r + 1, :], n1)
        n1_ref[h] = n1
        p1_ref[h] = jnp.exp(st[0] - m1)
        rank2_ref[h] = rank2.astype(BF16)
        p2_ref[h] = (jnp.exp(st[1] - m2) / z).astype(BF16)


def _route(qp, keys, tm):
    t = qp.shape[0]
    big = pl.BlockSpec((PEER_HEADS, N_KEYS, tm), lambda i: (0, 0, i))
    f32s = jax.ShapeDtypeStruct((PEER_HEADS, N_KEYS, t), F32)
    bf16s = jax.ShapeDtypeStruct((PEER_HEADS, N_KEYS, t), BF16)
    return pl.pallas_call(
        _route_kernel,
        grid=(t // tm,),
        in_specs=[pl.BlockSpec((tm, D_MODEL), lambda i: (i, 0)),
                  pl.BlockSpec((2, N_KEYS, DQ_HALF), lambda i: (0, 0, 0))],
        out_specs=[big, big, big, big],
        out_shape=[f32s, f32s, bf16s, bf16s],
        compiler_params=_params(("arbitrary",)),
        name="route",
    )(qp, keys)


def _row_bf16(ref, h, e1, tm):
    row = jnp.broadcast_to(ref[h, pl.ds(e1, 1), :], (BF16_ROWS, tm)).astype(BF16)
    return jnp.concatenate([row] * (N_KEYS // BF16_ROWS), axis=0)


def _peer_kernel(h2t_ref, u_ref, v_ref, n1_ref, p1_ref, rank2_ref, p2_ref,
                 xm_ref, gf_ref, out_ref, *, te):
    j = pl.program_id(1)

    @pl.when(j == 0)
    def _():
        out_ref[...] = jnp.zeros_like(out_ref)

    tm = h2t_ref.shape[1]
    rows_per_part = PEER_PART_ROWS
    acc = None
    for part in range(te // rows_per_part):
        lo = part * rows_per_part
        a = jnp.dot(u_ref[lo:lo + rows_per_part, :], h2t_ref[...],
                    preferred_element_type=F32)
        blocks = []
        for l in range(rows_per_part // N_KEYS):
            e1 = j * (te // N_KEYS) + part * (rows_per_part // N_KEYS) + l
            g = jnp.zeros((N_KEYS, tm), BF16)
            for h in range(PEER_HEADS):
                keep = rank2_ref[h] < _row_bf16(n1_ref, h, e1, tm)
                g = g + jnp.where(keep, p2_ref[h], 0.0) * _row_bf16(p1_ref, h, e1, tm)
            al = a[l * N_KEYS:(l + 1) * N_KEYS, :]
            gelu = 0.5 * al * (1.0 + lax.erf(al * INV_SQRT2))
            blocks.append(gelu.astype(BF16) * g)
        wt = jnp.concatenate(blocks, axis=0) if len(blocks) > 1 else blocks[0]
        d = jnp.dot(jnp.transpose(wt), v_ref[lo:lo + rows_per_part, :],
                    preferred_element_type=F32)
        acc = d if acc is None else acc + d
    out_ref[...] += acc

    @pl.when(j == pl.num_programs(1) - 1)
    def _():
        out_ref[...] = _rms(xm_ref[...] + out_ref[...], gf_ref[...])


def _peer(h2t, u_bf, v_bf, n1, p1, rank2, p2, xm, gf, tm, te):
    t = xm.shape[0]
    big = pl.BlockSpec((PEER_HEADS, N_KEYS, tm), lambda i, j: (0, 0, i))
    return pl.pallas_call(
        functools.partial(_peer_kernel, te=te),
        grid=(t // tm, N_EXPERTS // te),
        in_specs=[pl.BlockSpec((D_MODEL, tm), lambda i, j: (0, i)),
                  pl.BlockSpec((te, D_MODEL), lambda i, j: (j, 0)),
                  pl.BlockSpec((te, D_MODEL), lambda i, j: (j, 0)),
                  big, big, big, big,
                  pl.BlockSpec((tm, D_MODEL), lambda i, j: (i, 0)),
                  pl.BlockSpec((1, D_MODEL), lambda i, j: (0, 0))],
        out_specs=pl.BlockSpec((tm, D_MODEL), lambda i, j: (i, 0)),
        out_shape=jax.ShapeDtypeStruct((t, D_MODEL), F32),
        compiler_params=_params(("arbitrary", "arbitrary")),
        name="peer",
    )(h2t, u_bf, v_bf, n1, p1, rank2, p2, xm, gf)


def _channel_mixer(o, p32, p16, pa, pb, x, wts, tm_mix, tm_route, tm_peer, te, prompt, seq):
    xm, h2t, qp, utail = _mix(o, p32, p16, pa, pb, x, wts, tm_mix, prompt, seq)
    n1, p1, rank2, p2 = _route(qp, wts["keys"], tm_route)
    y = _peer(h2t, wts["u"], wts["v"], n1, p1, rank2, p2, xm, wts["gf"], tm_peer, te)
    return y, utail


def kernel(x_prompt, x_sample, state_gla, state_conv, norm1_g, w_in, w_alpha_up, b_alpha,
           gla_norm_g, conv_w, conv_b, w_out, norm2_g, w_query, sub_keys, expert_u, expert_v,
           norm_f_g):
    batch, seq, _ = x_prompt.shape
    n_dec = x_sample.shape[0]
    assert w_in.shape[0] == 1 and x_sample.shape[1] == 1

    alr0 = 2 * GLA_HEADS * GLA_DK + 2 * GLA_HEADS * GLA_DV
    w_lo = w_in[0][:, :alr0].astype(BF16)
    w_hi = w_in[0][:, alr0 + GLA_RANK:].astype(BF16)
    w_alr = jnp.pad(w_in[0][:, alr0:alr0 + GLA_RANK],
                    ((0, 0), (0, LANES - GLA_RANK))).astype(BF16)
    wup = jnp.pad(w_alpha_up[0], ((0, LANES - GLA_RANK), (0, 0))).astype(BF16)
    ba = b_alpha[0][None, :]
    g1 = norm1_g[0][None, :]
    wts = dict(gg=gla_norm_g[0][None, :], cw=conv_w[0], cbias=conv_b[0][None, :],
               wo=w_out[0].astype(BF16), g2=norm2_g[0][None, :],
               wq=w_query[0].astype(BF16), keys=sub_keys[0],
               u=expert_u[0].astype(BF16), v=expert_v[0].astype(BF16),
               gf=norm_f_g[None, :])

    xp = x_prompt.reshape(batch * seq, D_MODEL)
    p32, p16, alr = _inproj(xp, g1, w_lo, w_hi, w_alr, tm=1024, tn=1024)
    o, s_p = _gla_prompt(p32, alr, wup, ba, batch, seq, rows=256)
    tm_mix = 256
    y_p, utail = _channel_mixer(o, p32, p16, p32, p32, xp, wts, tm_mix=tm_mix, tm_route=256,
                                tm_peer=512, te=1024, prompt=True, seq=seq)
    last = utail.reshape(batch, seq // tm_mix, SUBLANES, D_MODEL)[:, -1, SUBLANES - 2:, :]

    xs = x_sample.reshape(n_dec, D_MODEL)
    ps32, ps16, alrs = _inproj(xs, g1, w_lo, w_hi, w_alr, tm=n_dec, tn=1024)
    os_, s_s = _gla_sample(ps32.reshape(n_dec, 1, N_MAIN // 2),
                           alrs.reshape(n_dec, 1, LANES), wup, ba, state_gla[0], nb=2)
    y_s, us = _channel_mixer(os_.reshape(n_dec, D_MODEL), ps32, ps16, state_conv[0][:, 0, :],
                             state_conv[0][:, 1, :], xs, wts, tm_mix=n_dec, tm_route=n_dec,
                             tm_peer=n_dec, te=512, prompt=False, seq=1)
    conv_s = jnp.stack([state_conv[0][:, 1, :], us], axis=1)

    return (y_p.reshape(batch, seq, D_MODEL),
            y_s.reshape(n_dec, 1, D_MODEL),
            s_p[None],
            last[None],
            s_s[None],
            conv_s[None])
```

```python
import functools
import math

import jax
import jax.numpy as jnp
from jax import lax
from jax.experimental import pallas as pl
from jax.experimental.pallas import tpu as pltpu

F32 = jnp.float32
BF16 = jnp.bfloat16

D_MODEL = 2048
GLA_HEADS = 4
GLA_DK = 256
GLA_DV = 512
GLA_RANK = 16
GLA_TAU = 16.0
GLA_CHUNK = 64
CONV_WIDTH = 3
PEER_HEADS = 8
N_KEYS = 128
N_EXPERTS = N_KEYS * N_KEYS
PEER_TOPK = 16
DQ_HALF = 128
EPS = 1e-6

LANES = 128
SUBLANES = 8
BF16_ROWS = 16
VMEM_LIMIT_BYTES = 60000 * 1024

N_MAIN = 16384
P_SEGMENT = 4096
GLA_SUB = 8
GLA_HEADS_PER_STEP = 4
PEER_PART_ROWS = 256
NEG_INF = float("-inf")
INV_SQRT2 = 1.0 / math.sqrt(2.0)


def _params(semantics):
    return pltpu.CompilerParams(dimension_semantics=semantics,
                                vmem_limit_bytes=VMEM_LIMIT_BYTES)


def _rms(x, g):
    return x * lax.rsqrt(jnp.mean(x * x, axis=-1, keepdims=True) + EPS) * g


def _inproj_kernel(*refs, n_lo, tiles_per_seg, n_tables):
    x_ref, g_ref, wlo_ref, whi_ref, walr_ref = refs[:5]
    tables_in = refs[5:5 + n_tables]
    p32_ref, p16_ref, alr_ref = refs[5 + n_tables:8 + n_tables]
    tables_out = refs[8 + n_tables:8 + 2 * n_tables]
    h_ref = refs[-1]
    j = pl.program_id(1)

    @pl.when(j == 0)
    def _():
        hb = _rms(x_ref[...], g_ref[...]).astype(BF16)
        h_ref[...] = hb
        alr_ref[...] = jnp.dot(hb, walr_ref[...], preferred_element_type=F32)

    wide = (j // tiles_per_seg) % 2 == 0
    low = j < n_lo
    for w_ref, from_low in ((wlo_ref, True), (whi_ref, False)):
        for out_ref, is_wide in ((p32_ref, True), (p16_ref, False)):
            @pl.when((low == from_low) & (wide == is_wide))
            def _():
                out_ref[...] = jnp.dot(h_ref[...], w_ref[...],
                                       preferred_element_type=F32).astype(out_ref.dtype)

    for src_ref, dst_ref in zip(tables_in, tables_out):
        dst_ref[...] = src_ref[...].astype(BF16)


def _inproj(x, g, w_lo, w_hi, w_alr, tm, tn, tables=()):
    t = x.shape[0]
    n_lo = w_lo.shape[1] // tn
    per = P_SEGMENT // tn
    nj = N_MAIN // tn
    steps = (t // tm) * nj

    def idx32(j):
        seg, r = j // per, j % per
        return jnp.where(seg % 2 == 0, (seg // 2) * per + r, (seg // 2 + 1) * per - 1)

    def idx16(j):
        seg, r = j // per, j % per
        return jnp.where(seg % 2 == 1, (seg // 2) * per + r,
                         jnp.maximum((seg // 2) * per - 1, 0))

    slabs = [pl.BlockSpec((tb.shape[0] // steps, D_MODEL), lambda i, j: (i * nj + j, 0))
             for tb in tables]
    return pl.pallas_call(
        functools.partial(_inproj_kernel, n_lo=n_lo, tiles_per_seg=per,
                          n_tables=len(tables)),
        grid=(t // tm, nj),
        in_specs=[
            pl.BlockSpec((tm, D_MODEL), lambda i, j: (i, 0), pipeline_mode=pl.Buffered(1)),
            pl.BlockSpec((1, D_MODEL), lambda i, j: (0, 0)),
            pl.BlockSpec((D_MODEL, tn), lambda i, j: (0, jnp.minimum(j, n_lo - 1))),
            pl.BlockSpec((D_MODEL, tn), lambda i, j: (0, jnp.maximum(j - n_lo, 0))),
            pl.BlockSpec((D_MODEL, LANES), lambda i, j: (0, 0)),
        ] + slabs,
        out_specs=[
            pl.BlockSpec((tm, tn), lambda i, j: (i, idx32(j))),
            pl.BlockSpec((tm, tn), lambda i, j: (i, idx16(j))),
            pl.BlockSpec((tm, LANES), lambda i, j: (i, 0)),
        ] + slabs,
        out_shape=[
            jax.ShapeDtypeStruct((t, N_MAIN // 2), F32),
            jax.ShapeDtypeStruct((t, N_MAIN // 2), BF16),
            jax.ShapeDtypeStruct((t, LANES), F32),
        ] + [jax.ShapeDtypeStruct(tb.shape, BF16) for tb in tables],
        scratch_shapes=[pltpu.VMEM((tm, D_MODEL), BF16)],
        compiler_params=_params(("arbitrary", "arbitrary")),
        name="inproj",
    )(x, g, w_lo, w_hi, w_alr, *tables)


def _log_decay(alr, wup, ba):
    xa = jnp.dot(alr.astype(BF16), wup, preferred_element_type=F32) + ba
    return -(jnp.maximum(-xa, 0.0) + jnp.log1p(jnp.exp(-jnp.abs(xa)))) * (1.0 / GLA_TAU)


def _col_scale(row, width):
    col = jnp.transpose(jnp.broadcast_to(row, (LANES, row.shape[1])))
    return jnp.concatenate([col] * (width // LANES), axis=1)


def _gla_prompt_kernel(q_ref, k_ref, v_ref, alr_ref, wup_ref, ba_ref,
                       o_ref, sout_ref, s_ref, b_ref, *, rows):
    c = pl.program_id(2)

    @pl.when(c == 0)
    def _():
        s_ref[...] = jnp.zeros_like(s_ref)

    g = _log_decay(alr_ref[...], wup_ref[...], ba_ref[...])
    rmod = lax.broadcasted_iota(jnp.int32, g.shape, 0) & (GLA_CHUNK - 1)
    step = 1
    while step < GLA_CHUNK:
        g = g + jnp.where(rmod >= step, pltpu.roll(g, step, 0), 0.0)
        step *= 2
    b_ref[...] = g

    row = lax.broadcasted_iota(jnp.int32, (GLA_CHUNK, GLA_DK), 0)
    rsub = row & (GLA_SUB - 1)
    ri = lax.broadcasted_iota(jnp.int32, (GLA_CHUNK, GLA_CHUNK), 0)
    cj = lax.broadcasted_iota(jnp.int32, (GLA_CHUNK, GLA_CHUNK), 1)
    n_sub = GLA_CHUNK // GLA_SUB

    def head_chunk(r0, hh):
        dk = slice(hh * GLA_DK, (hh + 1) * GLA_DK)
        dv = slice(hh * GLA_DV, (hh + 1) * GLA_DV)
        qc = q_ref[pl.ds(r0, GLA_CHUNK), dk] * (GLA_DK ** -0.5)
        kc = k_ref[pl.ds(r0, GLA_CHUNK), dk]
        vc = v_ref[pl.ds(r0, GLA_CHUNK), dv]
        b = b_ref[pl.ds(r0, GLA_CHUNK), dk]
        bl = b[GLA_CHUNK - 1:GLA_CHUNK, :]
        s = s_ref[hh]
        vb = vc.astype(BF16)

        o = jnp.dot((qc * jnp.exp(b)).astype(BF16), s.astype(BF16),
                    preferred_element_type=F32)

        parts = [jnp.zeros((GLA_SUB, GLA_CHUNK), F32)]
        for blk in range(1, n_sub):
            lo = blk * GLA_SUB
            ref = b[lo - 1:lo, :]
            qs = qc[lo:lo + GLA_SUB, :] * jnp.exp(b[lo:lo + GLA_SUB, :] - ref)
            ks = kc * jnp.exp(jnp.where(row < lo, ref - b, NEG_INF))
            parts.append(lax.dot_general(qs.astype(BF16), ks.astype(BF16),
                                         (((1,), (1,)), ((), ())),
                                         preferred_element_type=F32))
        attn = jnp.concatenate(parts, axis=0)

        for d in range(GLA_SUB):
            if d == 0:
                t = qc * kc
            else:
                kd = pltpu.roll(kc, d, 0)
                bd = pltpu.roll(b, d, 0)
                t = qc * kd * jnp.exp(jnp.where(rsub >= d, b - bd, NEG_INF))
            cs = jnp.sum(t, axis=1, keepdims=True)
            attn = attn + jnp.where(ri - cj == d, cs, 0.0)

        o = o + jnp.dot(attn.astype(BF16), vb, preferred_element_type=F32)
        o_ref[pl.ds(r0, GLA_CHUNK), dv] = o

        kdec = (kc * jnp.exp(bl - b)).astype(BF16)
        s_ref[hh] = (_col_scale(jnp.exp(bl), GLA_DV) * s
                     + lax.dot_general(kdec, vb, (((0,), (0,)), ((), ())),
                                       preferred_element_type=F32))

    def chunk(ci, carry):
        r0 = pl.multiple_of(ci * GLA_CHUNK, GLA_CHUNK)
        for hh in range(GLA_HEADS_PER_STEP):
            head_chunk(r0, hh)
        return carry

    lax.fori_loop(0, rows // GLA_CHUNK, chunk, 0)

    @pl.when(c == pl.num_programs(2) - 1)
    def _():
        sout_ref[0] = s_ref[...]


def _gla_prompt(p, alr, wup, ba, batch, seq, rows):
    nsteps = seq // rows
    hps = GLA_HEADS_PER_STEP
    wk = hps * GLA_DK
    wv = hps * GLA_DV
    kq = GLA_HEADS * GLA_DK // wk
    kv = 2 * GLA_HEADS * GLA_DK // wv

    def rowmap(b, h, c):
        return b * nsteps + c

    return pl.pallas_call(
        functools.partial(_gla_prompt_kernel, rows=rows),
        grid=(batch, GLA_HEADS // hps, nsteps),
        in_specs=[
            pl.BlockSpec((rows, wk), lambda b, h, c: (rowmap(b, h, c), h)),
            pl.BlockSpec((rows, wk), lambda b, h, c: (rowmap(b, h, c), kq + h)),
            pl.BlockSpec((rows, wv), lambda b, h, c: (rowmap(b, h, c), kv + h)),
            pl.BlockSpec((rows, LANES), lambda b, h, c: (rowmap(b, h, c), 0)),
            pl.BlockSpec((LANES, wk), lambda b, h, c: (0, h)),
            pl.BlockSpec((1, wk), lambda b, h, c: (0, h)),
        ],
        out_specs=[
            pl.BlockSpec((rows, wv), lambda b, h, c: (rowmap(b, h, c), h)),
            pl.BlockSpec((1, hps, GLA_DK, GLA_DV), lambda b, h, c: (b, h, 0, 0)),
        ],
        out_shape=[
            jax.ShapeDtypeStruct((batch * seq, GLA_HEADS * GLA_DV), F32),
            jax.ShapeDtypeStruct((batch, GLA_HEADS, GLA_DK, GLA_DV), F32),
        ],
        scratch_shapes=[pltpu.VMEM((hps, GLA_DK, GLA_DV), F32),
                        pltpu.VMEM((rows, wk), F32)],
        compiler_params=_params(("arbitrary", "arbitrary", "arbitrary")),
        name="gla_prompt",
    )(p, p, p, alr, wup, ba)


def _gla_sample_kernel(q_ref, k_ref, v_ref, alr_ref, wup_ref, ba_ref, s0_ref,
                       o_ref, s1_ref):
    first = lax.broadcasted_iota(jnp.int32, (SUBLANES, GLA_DK), 0) == 0
    for bb in range(q_ref.shape[0]):
        g_all = _log_decay(jnp.broadcast_to(alr_ref[bb], (SUBLANES, LANES)),
                           wup_ref[...], ba_ref[...])
        for h in range(GLA_HEADS):
            dk = slice(h * GLA_DK, (h + 1) * GLA_DK)
            dv = slice(h * GLA_DV, (h + 1) * GLA_DV)
            q = jnp.broadcast_to(q_ref[bb, :, dk], (SUBLANES, GLA_DK)) * (GLA_DK ** -0.5)
            k = jnp.broadcast_to(k_ref[bb, :, dk], (SUBLANES, GLA_DK))
            v = jnp.broadcast_to(v_ref[bb, :, dv], (SUBLANES, GLA_DV))
            g = g_all[:, dk]
            s = s0_ref[bb, h]
            attn = jnp.sum(q * k, axis=1, keepdims=True)
            o = attn * v + jnp.dot((q * jnp.exp(g)).astype(BF16), s.astype(BF16),
                                   preferred_element_type=F32)
            o_ref[bb, :, dv] = o[0:1, :]
            k1 = jnp.where(first, k, 0.0).astype(BF16)
            s1_ref[bb, h] = (_col_scale(jnp.exp(g[0:1, :]), GLA_DV) * s
                             + lax.dot_general(k1, v.astype(BF16), (((0,), (0,)), ((), ())),
                                               preferred_element_type=F32))


def _gla_sample(p3, alr3, wup, ba, s0, nb):
    n = p3.shape[0]
    qk = GLA_HEADS * GLA_DK
    vw = GLA_HEADS * GLA_DV
    return pl.pallas_call(
        _gla_sample_kernel,
        grid=(n // nb,),
        in_specs=[
            pl.BlockSpec((nb, 1, qk), lambda b: (b, 0, 0)),
            pl.BlockSpec((nb, 1, qk), lambda b: (b, 0, 1)),
            pl.BlockSpec((nb, 1, vw), lambda b: (b, 0, 2 * qk // vw)),
            pl.BlockSpec((nb, 1, LANES), lambda b: (b, 0, 0)),
            pl.BlockSpec((LANES, qk), lambda b: (0, 0)),
            pl.BlockSpec((1, qk), lambda b: (0, 0)),
            pl.BlockSpec((nb, GLA_HEADS, GLA_DK, GLA_DV), lambda b: (b, 0, 0, 0)),
        ],
        out_specs=[
            pl.BlockSpec((nb, 1, vw), lambda b: (b, 0, 0)),
            pl.BlockSpec((nb, GLA_HEADS, GLA_DK, GLA_DV), lambda b: (b, 0, 0, 0)),
        ],
        out_shape=[
            jax.ShapeDtypeStruct((n, 1, vw), F32),
            jax.ShapeDtypeStruct((n, GLA_HEADS, GLA_DK, GLA_DV), F32),
        ],
        compiler_params=_params(("arbitrary",)),
        name="gla_sample",
    )(p3, p3, p3, alr3, wup, ba, s0)


def _mix_kernel(o_ref, r_ref, cb_ref, cc_ref, cx_ref, ga_ref, gb_ref, pa_ref, pb_ref,
                gg_ref, cw_ref, cbias_ref, x_ref, wo_ref, g2_ref, wq_ref,
                xm_ref, h2t_ref, qp_ref, u_ref, m_ref, *, prompt, tiles_per_seq):
    tm = x_ref.shape[0]
    if prompt:
        start = (pl.program_id(0) % tiles_per_seq) == 0
        row = lax.broadcasted_iota(jnp.int32, (tm, GLA_DV), 0)
    for h in range(GLA_HEADS):
        dv = slice(h * GLA_DV, (h + 1) * GLA_DV)
        u = cc_ref[:, dv] * cx_ref[:, dv]
        if prompt:
            prev = jnp.where(start, 0.0, pa_ref[:, dv] * pb_ref[:, dv])
            um1 = jnp.where(row == 0, prev[7:8, :], pltpu.roll(u, 1, 0))
            um2 = jnp.where(row == 0, prev[6:7, :],
                            jnp.where(row == 1, prev[7:8, :], pltpu.roll(u, 2, 0)))
            u_ref[0, :, dv] = u[tm - SUBLANES:, :]
        else:
            um2 = pa_ref[:, dv]
            um1 = pb_ref[:, dv]
            u_ref[:, dv] = u
        z = (cbias_ref[:, dv] + um2 * cw_ref[0:1, dv] + um1 * cw_ref[1:2, dv]
             + u * cw_ref[2:3, dv])
        y_b = cb_ref[:, dv].astype(F32) * z
        r = r_ref[:, dv].astype(F32)
        on = _rms(o_ref[:, dv], gg_ref[:, dv])
        m = (jax.nn.sigmoid(gb_ref[:, dv].astype(F32)) * y_b
             + jax.nn.sigmoid(ga_ref[:, dv].astype(F32)) * (r * jax.nn.sigmoid(r) * on))
        m_ref[:, dv] = m.astype(BF16)
    xm = x_ref[...] + jnp.dot(m_ref[...], wo_ref[...], preferred_element_type=F32)
    xm_ref[...] = xm
    h2 = _rms(xm, g2_ref[...])
    h2t_ref[...] = jnp.transpose(h2).astype(BF16)
    qp_ref[...] = jnp.dot(h2.astype(BF16), wq_ref[...], preferred_element_type=F32)


def _mix(o, p32, p16, pa, pb, x, wts, tm, prompt, seq):
    t = o.shape[0]
    n_tiles = t // tm
    tile = pl.BlockSpec((tm, D_MODEL), lambda i: (i, 0))
    col = lambda c: pl.BlockSpec((tm, D_MODEL), lambda i: (i, c))
    cc_col = P_SEGMENT // D_MODEL
    if prompt:
        prev_rows = tm // SUBLANES
        prev = lambda c: pl.BlockSpec(
            (SUBLANES, D_MODEL), lambda i: (jnp.maximum(i * prev_rows - 1, 0), c))
        pspecs = [prev(cc_col), prev(cc_col + 1)]
        uspec = pl.BlockSpec((1, SUBLANES, D_MODEL), lambda i: (i, 0, 0))
        ushape = jax.ShapeDtypeStruct((n_tiles, SUBLANES, D_MODEL), F32)
    else:
        pspecs = [tile, tile]
        uspec = tile
        ushape = jax.ShapeDtypeStruct((t, D_MODEL), F32)
    vec = lambda r: pl.BlockSpec((r, D_MODEL), lambda i: (0, 0))
    whole = pl.BlockSpec((D_MODEL, D_MODEL), lambda i: (0, 0),
                         pipeline_mode=pl.Buffered(1))
    return pl.pallas_call(
        functools.partial(_mix_kernel, prompt=prompt, tiles_per_seq=max(seq // tm, 1)),
        grid=(n_tiles,),
        in_specs=[tile, col(0), col(1), col(cc_col), col(cc_col + 1), col(2), col(3)]
                 + pspecs + [vec(1), vec(CONV_WIDTH), vec(1), tile, whole, vec(1), whole],
        out_specs=[tile, pl.BlockSpec((D_MODEL, tm), lambda i: (0, i)), tile, uspec],
        out_shape=[jax.ShapeDtypeStruct((t, D_MODEL), F32),
                   jax.ShapeDtypeStruct((D_MODEL, t), BF16),
                   jax.ShapeDtypeStruct((t, D_MODEL), F32),
                   ushape],
        scratch_shapes=[pltpu.VMEM((tm, D_MODEL), BF16)],
        compiler_params=_params(("arbitrary",)),
        name="mix_prompt" if prompt else "mix_sample",
    )(o, p16, p16, p32, p32, p16, p16, pa, pb, wts["gg"], wts["cw"], wts["cbias"],
      x, wts["wo"], wts["g2"], wts["wq"])


def _top_values(work, n):
    vals = []
    rank = jnp.full(work.shape, float(n), F32)
    for r in range(n):
        mx = jnp.max(work, axis=0, keepdims=True)
        hit = work == mx
        vals.append(mx)
        rank = jnp.where(hit, float(r), rank)
        work = jnp.where(hit, NEG_INF, work)
    return vals, rank


def _route_kernel(qp_ref, keys_ref, n1_ref, p1_ref, rank2_ref, p2_ref):
    for h in range(PEER_HEADS):
        st = []
        for side in range(2):
            c0 = (2 * h + side) * DQ_HALF
            st.append(lax.dot_general(keys_ref[side], qp_ref[:, c0:c0 + DQ_HALF],
                                      (((1,), (1,)), ((), ())),
                                      precision=lax.Precision.HIGHEST,
                                      preferred_element_type=F32))
        v1, rank1 = _top_values(st[0], PEER_TOPK)
        v2, rank2 = _top_values(st[1], PEER_TOPK)
        v2 = jnp.concatenate(v2, axis=0)
        tm = v2.shape[1]
        bidx = lax.broadcasted_iota(jnp.int32, (SUBLANES, tm), 0)
        cand = []
        for a in range(PEER_TOPK // 2):
            nb = PEER_TOPK // (a + 1)
            if nb >= SUBLANES:
                cand.append(v1[a] + v2[0:nb, :])
            else:
                cand.append(jnp.where(bidx < nb, v1[a] + v2[0:SUBLANES, :], NEG_INF))
        tail = jnp.concatenate(v1[PEER_TOPK // 2:], axis=0) + v2[0:1, :]
        cand.append(tail)
        thr = _top_values(jnp.concatenate(cand, axis=0), PEER_TOPK)[0][-1]
        m1 = v1[0]
        m2 = v2[0:1, :]
        z = jnp.zeros_like(m1)
        n1 = jnp.zeros_like(rank1)
        for a, ca in enumerate(cand):
            keep = ca >= thr
            z = z + jnp.sum(jnp.where(keep, jnp.exp(ca - (m1 + m2)), 0.0),
                            axis=0, keepdims=True)
            kept = jnp.where(keep, 1.0, 0.0)
            if a < PEER_TOPK // 2:
                n1 = jnp.where(rank1 == float(a), jnp.sum(kept, axis=0, keepdims=True), n1)
            else:
                for r in range(PEER_TOPK // 2):
                    n1 = jnp.where(rank1 == float(a + r), kept[r:r + 1, :], n1)
        n1_ref[h] = n1
        p1_ref[h] = jnp.exp(st[0] - m1)
        rank2_ref[h] = rank2.astype(BF16)
        p2_ref[h] = (jnp.exp(st[1] - m2) / z).astype(BF16)


def _route(qp, keys, tm):
    t = qp.shape[0]
    big = pl.BlockSpec((PEER_HEADS, N_KEYS, tm), lambda i: (0, 0, i))
    f32s = jax.ShapeDtypeStruct((PEER_HEADS, N_KEYS, t), F32)
    bf16s = jax.ShapeDtypeStruct((PEER_HEADS, N_KEYS, t), BF16)
    return pl.pallas_call(
        _route_kernel,
        grid=(t // tm,),
        in_specs=[pl.BlockSpec((tm, D_MODEL), lambda i: (i, 0)),
                  pl.BlockSpec((2, N_KEYS, DQ_HALF), lambda i: (0, 0, 0))],
        out_specs=[big, big, big, big],
        out_shape=[f32s, f32s, bf16s, bf16s],
        compiler_params=_params(("arbitrary",)),
        name="route",
    )(qp, keys)


def _row_bf16(ref, h, e1, tm):
    row = jnp.broadcast_to(ref[h, pl.ds(e1, 1), :], (BF16_ROWS, tm)).astype(BF16)
    return jnp.concatenate([row] * (N_KEYS // BF16_ROWS), axis=0)


def _peer_kernel(h2t_ref, u_ref, v_ref, n1_ref, p1_ref, rank2_ref, p2_ref,
                 xm_ref, gf_ref, out_ref, *, te):
    j = pl.program_id(1)

    @pl.when(j == 0)
    def _():
        out_ref[...] = jnp.zeros_like(out_ref)

    tm = h2t_ref.shape[1]
    rows_per_part = PEER_PART_ROWS
    acc = None
    for part in range(te // rows_per_part):
        lo = part * rows_per_part
        a = jnp.dot(u_ref[lo:lo + rows_per_part, :], h2t_ref[...],
                    preferred_element_type=F32)
        blocks = []
        for l in range(rows_per_part // N_KEYS):
            e1 = j * (te // N_KEYS) + part * (rows_per_part // N_KEYS) + l
            g = jnp.zeros((N_KEYS, tm), BF16)
            for h in range(PEER_HEADS):
                keep = rank2_ref[h] < _row_bf16(n1_ref, h, e1, tm)
                g = g + jnp.where(keep, p2_ref[h], 0.0) * _row_bf16(p1_ref, h, e1, tm)
            al = a[l * N_KEYS:(l + 1) * N_KEYS, :]
            gelu = 0.5 * al * (1.0 + lax.erf(al * INV_SQRT2))
            blocks.append(gelu.astype(BF16) * g)
        wt = jnp.concatenate(blocks, axis=0) if len(blocks) > 1 else blocks[0]
        d = jnp.dot(jnp.transpose(wt), v_ref[lo:lo + rows_per_part, :],
                    preferred_element_type=F32)
        acc = d if acc is None else acc + d
    out_ref[...] += acc

    @pl.when(j == pl.num_programs(1) - 1)
    def _():
        out_ref[...] = _rms(xm_ref[...] + out_ref[...], gf_ref[...])


def _peer(h2t, u_bf, v_bf, n1, p1, rank2, p2, xm, gf, tm, te):
    t = xm.shape[0]
    big = pl.BlockSpec((PEER_HEADS, N_KEYS, tm), lambda i, j: (0, 0, i))
    return pl.pallas_call(
        functools.partial(_peer_kernel, te=te),
        grid=(t // tm, N_EXPERTS // te),
        in_specs=[pl.BlockSpec((D_MODEL, tm), lambda i, j: (0, i)),
                  pl.BlockSpec((te, D_MODEL), lambda i, j: (j, 0)),
                  pl.BlockSpec((te, D_MODEL), lambda i, j: (j, 0)),
                  big, big, big, big,
                  pl.BlockSpec((tm, D_MODEL), lambda i, j: (i, 0)),
                  pl.BlockSpec((1, D_MODEL), lambda i, j: (0, 0))],
        out_specs=pl.BlockSpec((tm, D_MODEL), lambda i, j: (i, 0)),
        out_shape=jax.ShapeDtypeStruct((t, D_MODEL), F32),
        compiler_params=_params(("arbitrary", "arbitrary")),
        name="peer",
    )(h2t, u_bf, v_bf, n1, p1, rank2, p2, xm, gf)


def _channel_mixer(o, p32, p16, pa, pb, x, wts, tm_mix, tm_route, tm_peer, te, prompt, seq):
    xm, h2t, qp, utail = _mix(o, p32, p16, pa, pb, x, wts, tm_mix, prompt, seq)
    n1, p1, rank2, p2 = _route(qp, wts["keys"], tm_route)
    y = _peer(h2t, wts["u"], wts["v"], n1, p1, rank2, p2, xm, wts["gf"], tm_peer, te)
    return y, utail


def kernel(x_prompt, x_sample, state_gla, state_conv, norm1_g, w_in, w_alpha_up, b_alpha,
           gla_norm_g, conv_w, conv_b, w_out, norm2_g, w_query, sub_keys, expert_u, expert_v,
           norm_f_g):
    batch, seq, _ = x_prompt.shape
    n_dec = x_sample.shape[0]
    assert w_in.shape[0] == 1 and x_sample.shape[1] == 1

    alr0 = 2 * GLA_HEADS * GLA_DK + 2 * GLA_HEADS * GLA_DV
    w_lo = w_in[0][:, :alr0].astype(BF16)
    w_hi = w_in[0][:, alr0 + GLA_RANK:].astype(BF16)
    w_alr = jnp.pad(w_in[0][:, alr0:alr0 + GLA_RANK],
                    ((0, 0), (0, LANES - GLA_RANK))).astype(BF16)
    wup = jnp.pad(w_alpha_up[0], ((0, LANES - GLA_RANK), (0, 0))).astype(BF16)
    ba = b_alpha[0][None, :]
    g1 = norm1_g[0][None, :]
    wts = dict(gg=gla_norm_g[0][None, :], cw=conv_w[0], cbias=conv_b[0][None, :],
               g2=norm2_g[0][None, :], keys=sub_keys[0],
               gf=norm_f_g[None, :])

    xp = x_prompt.reshape(batch * seq, D_MODEL)
    p32, p16, alr, wts["u"], wts["v"], wts["wo"], wts["wq"] = _inproj(
        xp, g1, w_lo, w_hi, w_alr, tm=1024, tn=1024,
        tables=(expert_u[0], expert_v[0], w_out[0], w_query[0]))
    o, s_p = _gla_prompt(p32, alr, wup, ba, batch, seq, rows=256)
    tm_mix = 256
    y_p, utail = _channel_mixer(o, p32, p16, p32, p32, xp, wts, tm_mix=tm_mix, tm_route=256,
                                tm_peer=512, te=1024, prompt=True, seq=seq)
    last = utail.reshape(batch, seq // tm_mix, SUBLANES, D_MODEL)[:, -1, SUBLANES - 2:, :]

    xs = x_sample.reshape(n_dec, D_MODEL)
    ps32, ps16, alrs = _inproj(xs, g1, w_lo, w_hi, w_alr, tm=n_dec, tn=1024)
    os_, s_s = _gla_sample(ps32.reshape(n_dec, 1, N_MAIN // 2),
                           alrs.reshape(n_dec, 1, LANES), wup, ba, state_gla[0], nb=2)
    y_s, us = _channel_mixer(os_.reshape(n_dec, D_MODEL), ps32, ps16, state_conv[0][:, 0, :],
                             state_conv[0][:, 1, :], xs, wts, tm_mix=n_dec, tm_route=n_dec,
                             tm_peer=n_dec, te=512, prompt=False, seq=1)
    conv_s = jnp.stack([state_conv[0][:, 1, :], us], axis=1)

    return (y_p.reshape(batch, seq, D_MODEL),
            y_s.reshape(n_dec, 1, D_MODEL),
            s_p[None],
            last[None],
            s_s[None],
            conv_s[None])
```

```python
import functools
import math

import jax
import jax.numpy as jnp
from jax import lax
from jax.experimental import pallas as pl
from jax.experimental.pallas import tpu as pltpu

F32 = jnp.float32
BF16 = jnp.bfloat16

D_MODEL = 2048
GLA_HEADS = 4
GLA_DK = 256
GLA_DV = 512
GLA_RANK = 16
GLA_TAU = 16.0
GLA_CHUNK = 64
CONV_WIDTH = 3
PEER_HEADS = 8
N_KEYS = 128
N_EXPERTS = N_KEYS * N_KEYS
PEER_TOPK = 16
DQ_HALF = 128
EPS = 1e-6

LANES = 128
SUBLANES = 8
BF16_ROWS = 16
VMEM_LIMIT_BYTES = 60000 * 1024

N_MAIN = 16384
P_SEGMENT = 4096
GLA_SUB = 8
GLA_HEADS_PER_STEP = 4
PEER_PART_ROWS = 256
NEG_INF = float("-inf")
INV_SQRT2 = 1.0 / math.sqrt(2.0)


def _params(semantics):
    return pltpu.CompilerParams(dimension_semantics=semantics,
                                vmem_limit_bytes=VMEM_LIMIT_BYTES)


def _rms(x, g):
    return x * lax.rsqrt(jnp.mean(x * x, axis=-1, keepdims=True) + EPS) * g


def _wprep_kernel(a_ref, b_ref, lo_ref, hi_ref, alr_ref, *, n_lo):
    j = pl.program_id(0)

    @pl.when(j < n_lo)
    def _():
        lo_ref[...] = a_ref[...].astype(BF16)

    @pl.when(j >= n_lo)
    def _():
        both = jnp.concatenate([a_ref[...], b_ref[...]], axis=1)
        width = both.shape[1]
        hi_ref[...] = pltpu.roll(both, width - GLA_RANK, 1)[:, :a_ref.shape[1]].astype(BF16)

    @pl.when(j == n_lo)
    def _():
        head = a_ref[:, :LANES]
        lane = lax.broadcasted_iota(jnp.int32, head.shape, 1)
        alr_ref[...] = jnp.where(lane < GLA_RANK, head, 0.0).astype(BF16)


def _wprep(w, tn):
    rows, cols = w.shape
    alr0 = 2 * GLA_HEADS * GLA_DK + 2 * GLA_HEADS * GLA_DV
    n_lo = alr0 // tn
    n_hi = (cols - alr0 - GLA_RANK) // tn
    return pl.pallas_call(
        functools.partial(_wprep_kernel, n_lo=n_lo),
        grid=(n_lo + n_hi,),
        in_specs=[pl.BlockSpec((rows, tn), lambda j: (0, j)),
                  pl.BlockSpec((rows, LANES), lambda j: (0, (j + 1) * (tn // LANES)))],
        out_specs=[pl.BlockSpec((rows, tn), lambda j: (0, jnp.minimum(j, n_lo - 1))),
                   pl.BlockSpec((rows, tn), lambda j: (0, jnp.maximum(j - n_lo, 0))),
                   pl.BlockSpec((rows, LANES), lambda j: (0, 0))],
        out_shape=[jax.ShapeDtypeStruct((rows, n_lo * tn), BF16),
                   jax.ShapeDtypeStruct((rows, n_hi * tn), BF16),
                   jax.ShapeDtypeStruct((rows, LANES), BF16)],
        compiler_params=_params(("arbitrary",)),
        name="wprep",
    )(w, w)


def _inproj_kernel(*refs, n_lo, tiles_per_seg, n_tables):
    x_ref, g_ref, wlo_ref, whi_ref, walr_ref = refs[:5]
    tables_in = refs[5:5 + n_tables]
    p32_ref, p16_ref, alr_ref = refs[5 + n_tables:8 + n_tables]
    tables_out = refs[8 + n_tables:8 + 2 * n_tables]
    h_ref = refs[-1]
    j = pl.program_id(1)

    @pl.when(j == 0)
    def _():
        hb = _rms(x_ref[...], g_ref[...]).astype(BF16)
        h_ref[...] = hb
        alr_ref[...] = jnp.dot(hb, walr_ref[...], preferred_element_type=F32)

    wide = (j // tiles_per_seg) % 2 == 0
    low = j < n_lo
    for w_ref, from_low in ((wlo_ref, True), (whi_ref, False)):
        for out_ref, is_wide in ((p32_ref, True), (p16_ref, False)):
            @pl.when((low == from_low) & (wide == is_wide))
            def _():
                out_ref[...] = jnp.dot(h_ref[...], w_ref[...],
                                       preferred_element_type=F32).astype(out_ref.dtype)

    for src_ref, dst_ref in zip(tables_in, tables_out):
        dst_ref[...] = src_ref[...].astype(BF16)


def _inproj(x, g, w_lo, w_hi, w_alr, tm, tn, tables=()):
    t = x.shape[0]
    n_lo = w_lo.shape[1] // tn
    per = P_SEGMENT // tn
    nj = N_MAIN // tn
    steps = (t // tm) * nj

    def idx32(j):
        seg, r = j // per, j % per
        return jnp.where(seg % 2 == 0, (seg // 2) * per + r, (seg // 2 + 1) * per - 1)

    def idx16(j):
        seg, r = j // per, j % per
        return jnp.where(seg % 2 == 1, (seg // 2) * per + r,
                         jnp.maximum((seg // 2) * per - 1, 0))

    slabs = [pl.BlockSpec((tb.shape[0] // steps, D_MODEL), lambda i, j: (i * nj + j, 0))
             for tb in tables]
    return pl.pallas_call(
        functools.partial(_inproj_kernel, n_lo=n_lo, tiles_per_seg=per,
                          n_tables=len(tables)),
        grid=(t // tm, nj),
        in_specs=[
            pl.BlockSpec((tm, D_MODEL), lambda i, j: (i, 0), pipeline_mode=pl.Buffered(1)),
            pl.BlockSpec((1, D_MODEL), lambda i, j: (0, 0)),
            pl.BlockSpec((D_MODEL, tn), lambda i, j: (0, jnp.minimum(j, n_lo - 1))),
            pl.BlockSpec((D_MODEL, tn), lambda i, j: (0, jnp.maximum(j - n_lo, 0))),
            pl.BlockSpec((D_MODEL, LANES), lambda i, j: (0, 0)),
        ] + slabs,
        out_specs=[
            pl.BlockSpec((tm, tn), lambda i, j: (i, idx32(j))),
            pl.BlockSpec((tm, tn), lambda i, j: (i, idx16(j))),
            pl.BlockSpec((tm, LANES), lambda i, j: (i, 0)),
        ] + slabs,
        out_shape=[
            jax.ShapeDtypeStruct((t, N_MAIN // 2), F32),
            jax.ShapeDtypeStruct((t, N_MAIN // 2), BF16),
            jax.ShapeDtypeStruct((t, LANES), F32),
        ] + [jax.ShapeDtypeStruct(tb.shape, BF16) for tb in tables],
        scratch_shapes=[pltpu.VMEM((tm, D_MODEL), BF16)],
        compiler_params=_params(("arbitrary", "arbitrary")),
        name="inproj",
    )(x, g, w_lo, w_hi, w_alr, *tables)


def _log_decay(alr, wup, ba):
    xa = jnp.dot(alr.astype(BF16), wup, preferred_element_type=F32) + ba
    return -(jnp.maximum(-xa, 0.0) + jnp.log1p(jnp.exp(-jnp.abs(xa)))) * (1.0 / GLA_TAU)


def _col_scale(row, width):
    col = jnp.transpose(jnp.broadcast_to(row, (LANES, row.shape[1])))
    return jnp.concatenate([col] * (width // LANES), axis=1)


def _gla_prompt_kernel(q_ref, k_ref, v_ref, alr_ref, wup_ref, ba_ref,
                       o_ref, sout_ref, s_ref, b_ref, *, rows):
    c = pl.program_id(2)

    @pl.when(c == 0)
    def _():
        s_ref[...] = jnp.zeros_like(s_ref)

    g = _log_decay(alr_ref[...], wup_ref[...], ba_ref[...])
    rmod = lax.broadcasted_iota(jnp.int32, g.shape, 0) & (GLA_CHUNK - 1)
    step = 1
    while step < GLA_CHUNK:
        g = g + jnp.where(rmod >= step, pltpu.roll(g, step, 0), 0.0)
        step *= 2
    b_ref[...] = g

    row = lax.broadcasted_iota(jnp.int32, (GLA_CHUNK, GLA_DK), 0)
    rsub = row & (GLA_SUB - 1)
    ri = lax.broadcasted_iota(jnp.int32, (GLA_CHUNK, GLA_CHUNK), 0)
    cj = lax.broadcasted_iota(jnp.int32, (GLA_CHUNK, GLA_CHUNK), 1)
    n_sub = GLA_CHUNK // GLA_SUB

    def head_chunk(r0, hh):
        dk = slice(hh * GLA_DK, (hh + 1) * GLA_DK)
        dv = slice(hh * GLA_DV, (hh + 1) * GLA_DV)
        qc = q_ref[pl.ds(r0, GLA_CHUNK), dk] * (GLA_DK ** -0.5)
        kc = k_ref[pl.ds(r0, GLA_CHUNK), dk]
        vc = v_ref[pl.ds(r0, GLA_CHUNK), dv]
        b = b_ref[pl.ds(r0, GLA_CHUNK), dk]
        bl = b[GLA_CHUNK - 1:GLA_CHUNK, :]
        s = s_ref[hh]
        vb = vc.astype(BF16)

        o = jnp.dot((qc * jnp.exp(b)).astype(BF16), s.astype(BF16),
                    preferred_element_type=F32)

        parts = [jnp.zeros((GLA_SUB, GLA_CHUNK), F32)]
        for blk in range(1, n_sub):
            lo = blk * GLA_SUB
            ref = b[lo - 1:lo, :]
            qs = qc[lo:lo + GLA_SUB, :] * jnp.exp(b[lo:lo + GLA_SUB, :] - ref)
            ks = kc * jnp.exp(jnp.where(row < lo, ref - b, NEG_INF))
            parts.append(lax.dot_general(qs.astype(BF16), ks.astype(BF16),
                                         (((1,), (1,)), ((), ())),
                                         preferred_element_type=F32))
        attn = jnp.concatenate(parts, axis=0)

        for d in range(GLA_SUB):
            if d == 0:
                t = qc * kc
            else:
                kd = pltpu.roll(kc, d, 0)
                bd = pltpu.roll(b, d, 0)
                t = qc * kd * jnp.exp(jnp.where(rsub >= d, b - bd, NEG_INF))
            cs = jnp.sum(t, axis=1, keepdims=True)
            attn = attn + jnp.where(ri - cj == d, cs, 0.0)

        o = o + jnp.dot(attn.astype(BF16), vb, preferred_element_type=F32)
        o_ref[pl.ds(r0, GLA_CHUNK), dv] = o

        kdec = (kc * jnp.exp(bl - b)).astype(BF16)
        s_ref[hh] = (_col_scale(jnp.exp(bl), GLA_DV) * s
                     + lax.dot_general(kdec, vb, (((0,), (0,)), ((), ())),
                                       preferred_element_type=F32))

    def chunk(ci, carry):
        r0 = pl.multiple_of(ci * GLA_CHUNK, GLA_CHUNK)
        for hh in range(GLA_HEADS_PER_STEP):
            head_chunk(r0, hh)
        return carry

    lax.fori_loop(0, rows // GLA_CHUNK, chunk, 0)

    @pl.when(c == pl.num_programs(2) - 1)
    def _():
        sout_ref[0] = s_ref[...]


def _gla_prompt(p, alr, wup, ba, batch, seq, rows):
    nsteps = seq // rows
    hps = GLA_HEADS_PER_STEP
    wk = hps * GLA_DK
    wv = hps * GLA_DV
    kq = GLA_HEADS * GLA_DK // wk
    kv = 2 * GLA_HEADS * GLA_DK // wv

    def rowmap(b, h, c):
        return b * nsteps + c

    return pl.pallas_call(
        functools.partial(_gla_prompt_kernel, rows=rows),
        grid=(batch, GLA_HEADS // hps, nsteps),
        in_specs=[
            pl.BlockSpec((rows, wk), lambda b, h, c: (rowmap(b, h, c), h)),
            pl.BlockSpec((rows, wk), lambda b, h, c: (rowmap(b, h, c), kq + h)),
            pl.BlockSpec((rows, wv), lambda b, h, c: (rowmap(b, h, c), kv + h)),
            pl.BlockSpec((rows, LANES), lambda b, h, c: (rowmap(b, h, c), 0)),
            pl.BlockSpec((LANES, wk), lambda b, h, c: (0, h)),
            pl.BlockSpec((1, wk), lambda b, h, c: (0, h)),
        ],
        out_specs=[
            pl.BlockSpec((rows, wv), lambda b, h, c: (rowmap(b, h, c), h)),
            pl.BlockSpec((1, hps, GLA_DK, GLA_DV), lambda b, h, c: (b, h, 0, 0)),
        ],
        out_shape=[
            jax.ShapeDtypeStruct((batch * seq, GLA_HEADS * GLA_DV), F32),
            jax.ShapeDtypeStruct((batch, GLA_HEADS, GLA_DK, GLA_DV), F32),
        ],
        scratch_shapes=[pltpu.VMEM((hps, GLA_DK, GLA_DV), F32),
                        pltpu.VMEM((rows, wk), F32)],
        compiler_params=_params(("arbitrary", "arbitrary", "arbitrary")),
        name="gla_prompt",
    )(p, p, p, alr, wup, ba)


def _gla_sample_kernel(q_ref, k_ref, v_ref, alr_ref, wup_ref, ba_ref, s0_ref,
                       o_ref, s1_ref):
    first = lax.broadcasted_iota(jnp.int32, (SUBLANES, GLA_DK), 0) == 0
    for bb in range(q_ref.shape[0]):
        g_all = _log_decay(jnp.broadcast_to(alr_ref[bb], (SUBLANES, LANES)),
                           wup_ref[...], ba_ref[...])
        for h in range(GLA_HEADS):
            dk = slice(h * GLA_DK, (h + 1) * GLA_DK)
            dv = slice(h * GLA_DV, (h + 1) * GLA_DV)
            q = jnp.broadcast_to(q_ref[bb, :, dk], (SUBLANES, GLA_DK)) * (GLA_DK ** -0.5)
            k = jnp.broadcast_to(k_ref[bb, :, dk], (SUBLANES, GLA_DK))
            v = jnp.broadcast_to(v_ref[bb, :, dv], (SUBLANES, GLA_DV))
            g = g_all[:, dk]
            s = s0_ref[bb, h]
            attn = jnp.sum(q * k, axis=1, keepdims=True)
            o = attn * v + jnp.dot((q * jnp.exp(g)).astype(BF16), s.astype(BF16),
                                   preferred_element_type=F32)
            o_ref[bb, :, dv] = o[0:1, :]
            k1 = jnp.where(first, k, 0.0).astype(BF16)
            s1_ref[bb, h] = (_col_scale(jnp.exp(g[0:1, :]), GLA_DV) * s
                             + lax.dot_general(k1, v.astype(BF16), (((0,), (0,)), ((), ())),
                                               preferred_element_type=F32))


def _gla_sample(p3, alr3, wup, ba, s0, nb):
    n = p3.shape[0]
    qk = GLA_HEADS * GLA_DK
    vw = GLA_HEADS * GLA_DV
    return pl.pallas_call(
        _gla_sample_kernel,
        grid=(n // nb,),
        in_specs=[
            pl.BlockSpec((nb, 1, qk), lambda b: (b, 0, 0)),
            pl.BlockSpec((nb, 1, qk), lambda b: (b, 0, 1)),
            pl.BlockSpec((nb, 1, vw), lambda b: (b, 0, 2 * qk // vw)),
            pl.BlockSpec((nb, 1, LANES), lambda b: (b, 0, 0)),
            pl.BlockSpec((LANES, qk), lambda b: (0, 0)),
            pl.BlockSpec((1, qk), lambda b: (0, 0)),
            pl.BlockSpec((nb, GLA_HEADS, GLA_DK, GLA_DV), lambda b: (b, 0, 0, 0)),
        ],
        out_specs=[
            pl.BlockSpec((nb, 1, vw), lambda b: (b, 0, 0)),
            pl.BlockSpec((nb, GLA_HEADS, GLA_DK, GLA_DV), lambda b: (b, 0, 0, 0)),
        ],
        out_shape=[
            jax.ShapeDtypeStruct((n, 1, vw), F32),
            jax.ShapeDtypeStruct((n, GLA_HEADS, GLA_DK, GLA_DV), F32),
        ],
        compiler_params=_params(("arbitrary",)),
        name="gla_sample",
    )(p3, p3, p3, alr3, wup, ba, s0)


def _mix_kernel(o_ref, r_ref, cb_ref, cc_ref, cx_ref, ga_ref, gb_ref, pa_ref, pb_ref,
                gg_ref, cw_ref, cbias_ref, x_ref, wo_ref, g2_ref, wq_ref,
                xm_ref, h2t_ref, qp_ref, u_ref, m_ref, *, prompt, tiles_per_seq):
    tm = x_ref.shape[0]
    if prompt:
        start = (pl.program_id(0) % tiles_per_seq) == 0
        row = lax.broadcasted_iota(jnp.int32, (tm, GLA_DV), 0)
    for h in range(GLA_HEADS):
        dv = slice(h * GLA_DV, (h + 1) * GLA_DV)
        u = cc_ref[:, dv] * cx_ref[:, dv]
        if prompt:
            prev = jnp.where(start, 0.0, pa_ref[:, dv] * pb_ref[:, dv])
            um1 = jnp.where(row == 0, prev[7:8, :], pltpu.roll(u, 1, 0))
            um2 = jnp.where(row == 0, prev[6:7, :],
                            jnp.where(row == 1, prev[7:8, :], pltpu.roll(u, 2, 0)))
            u_ref[0, :, dv] = u[tm - SUBLANES:, :]
        else:
            um2 = pa_ref[:, dv]
            um1 = pb_ref[:, dv]
            u_ref[:, dv] = u
        z = (cbias_ref[:, dv] + um2 * cw_ref[0:1, dv] + um1 * cw_ref[1:2, dv]
             + u * cw_ref[2:3, dv])
        y_b = cb_ref[:, dv].astype(F32) * z
        r = r_ref[:, dv].astype(F32)
        on = _rms(o_ref[:, dv], gg_ref[:, dv])
        m = (jax.nn.sigmoid(gb_ref[:, dv].astype(F32)) * y_b
             + jax.nn.sigmoid(ga_ref[:, dv].astype(F32)) * (r * jax.nn.sigmoid(r) * on))
        m_ref[:, dv] = m.astype(BF16)
    xm = x_ref[...] + jnp.dot(m_ref[...], wo_ref[...], preferred_element_type=F32)
    xm_ref[...] = xm
    h2 = _rms(xm, g2_ref[...])
    h2t_ref[...] = jnp.transpose(h2).astype(BF16)
    qp_ref[...] = jnp.dot(h2.astype(BF16), wq_ref[...], preferred_element_type=F32)


def _mix(o, p32, p16, pa, pb, x, wts, tm, prompt, seq):
    t = o.shape[0]
    n_tiles = t // tm
    tile = pl.BlockSpec((tm, D_MODEL), lambda i: (i, 0))
    col = lambda c: pl.BlockSpec((tm, D_MODEL), lambda i: (i, c))
    cc_col = P_SEGMENT // D_MODEL
    if prompt:
        prev_rows = tm // SUBLANES
        prev = lambda c: pl.BlockSpec(
            (SUBLANES, D_MODEL), lambda i: (jnp.maximum(i * prev_rows - 1, 0), c))
        pspecs = [prev(cc_col), prev(cc_col + 1)]
        uspec = pl.BlockSpec((1, SUBLANES, D_MODEL), lambda i: (i, 0, 0))
        ushape = jax.ShapeDtypeStruct((n_tiles, SUBLANES, D_MODEL), F32)
    else:
        pspecs = [tile, tile]
        uspec = tile
        ushape = jax.ShapeDtypeStruct((t, D_MODEL), F32)
    vec = lambda r: pl.BlockSpec((r, D_MODEL), lambda i: (0, 0))
    whole = pl.BlockSpec((D_MODEL, D_MODEL), lambda i: (0, 0),
                         pipeline_mode=pl.Buffered(1))
    return pl.pallas_call(
        functools.partial(_mix_kernel, prompt=prompt, tiles_per_seq=max(seq // tm, 1)),
        grid=(n_tiles,),
        in_specs=[tile, col(0), col(1), col(cc_col), col(cc_col + 1), col(2), col(3)]
                 + pspecs + [vec(1), vec(CONV_WIDTH), vec(1), tile, whole, vec(1), whole],
        out_specs=[tile, pl.BlockSpec((D_MODEL, tm), lambda i: (0, i)), tile, uspec],
        out_shape=[jax.ShapeDtypeStruct((t, D_MODEL), F32),
                   jax.ShapeDtypeStruct((D_MODEL, t), BF16),
                   jax.ShapeDtypeStruct((t, D_MODEL), F32),
                   ushape],
        scratch_shapes=[pltpu.VMEM((tm, D_MODEL), BF16)],
        compiler_params=_params(("arbitrary",)),
        name="mix_prompt" if prompt else "mix_sample",
    )(o, p16, p16, p32, p32, p16, p16, pa, pb, wts["gg"], wts["cw"], wts["cbias"],
      x, wts["wo"], wts["g2"], wts["wq"])


def _top_values(work, n):
    vals = []
    rank = jnp.full(work.shape, float(n), F32)
    for r in range(n):
        mx = jnp.max(work, axis=0, keepdims=True)
        hit = work == mx
        vals.append(mx)
        rank = jnp.where(hit, float(r), rank)
        work = jnp.where(hit, NEG_INF, work)
    return vals, rank


def _route_kernel(qp_ref, keys_ref, n1_ref, p1_ref, rank2_ref, p2_ref):
    for h in range(PEER_HEADS):
        st = []
        for side in range(2):
            c0 = (2 * h + side) * DQ_HALF
            st.append(lax.dot_general(keys_ref[side], qp_ref[:, c0:c0 + DQ_HALF],
                                      (((1,), (1,)), ((), ())),
                                      precision=lax.Precision.HIGHEST,
                                      preferred_element_type=F32))
        v1, rank1 = _top_values(st[0], PEER_TOPK)
        v2, rank2 = _top_values(st[1], PEER_TOPK)
        v2 = jnp.concatenate(v2, axis=0)
        tm = v2.shape[1]
        bidx = lax.broadcasted_iota(jnp.int32, (SUBLANES, tm), 0)
        cand = []
        for a in range(PEER_TOPK // 2):
            nb = PEER_TOPK // (a + 1)
            if nb >= SUBLANES:
                cand.append(v1[a] + v2[0:nb, :])
            else:
                cand.append(jnp.where(bidx < nb, v1[a] + v2[0:SUBLANES, :], NEG_INF))
        tail = jnp.concatenate(v1[PEER_TOPK // 2:], axis=0) + v2[0:1, :]
        cand.append(tail)
        thr = _top_values(jnp.concatenate(cand, axis=0), PEER_TOPK)[0][-1]
        m1 = v1[0]
        m2 = v2[0:1, :]
        z = jnp.zeros_like(m1)
        n1 = jnp.zeros_like(rank1)
        for a, ca in enumerate(cand):
            keep = ca >= thr
            z = z + jnp.sum(jnp.where(keep, jnp.exp(ca - (m1 + m2)), 0.0),
                            axis=0, keepdims=True)
            kept = jnp.where(keep, 1.0, 0.0)
            if a < PEER_TOPK // 2:
                n1 = jnp.where(rank1 == float(a), jnp.sum(kept, axis=0, keepdims=True), n1)
            else:
                for r in range(PEER_TOPK // 2):
                    n1 = jnp.where(rank1 == float(a + r), kept[r:r + 1, :], n1)
        n1_ref[h] = n1
        p1_ref[h] = jnp.exp(st[0] - m1)
        rank2_ref[h] = rank2.astype(BF16)
        p2_ref[h] = (jnp.exp(st[1] - m2) / z).astype(BF16)


def _route(qp, keys, tm):
    t = qp.shape[0]
    big = pl.BlockSpec((PEER_HEADS, N_KEYS, tm), lambda i: (0, 0, i))
    f32s = jax.ShapeDtypeStruct((PEER_HEADS, N_KEYS, t), F32)
    bf16s = jax.ShapeDtypeStruct((PEER_HEADS, N_KEYS, t), BF16)
    return pl.pallas_call(
        _route_kernel,
        grid=(t // tm,),
        in_specs=[pl.BlockSpec((tm, D_MODEL), lambda i: (i, 0)),
                  pl.BlockSpec((2, N_KEYS, DQ_HALF), lambda i: (0, 0, 0))],
        out_specs=[big, big, big, big],
        out_shape=[f32s, f32s, bf16s, bf16s],
        compiler_params=_params(("arbitrary",)),
        name="route",
    )(qp, keys)


def _row_bf16(ref, h, e1, tm):
    row = jnp.broadcast_to(ref[h, pl.ds(e1, 1), :], (BF16_ROWS, tm)).astype(BF16)
    return jnp.concatenate([row] * (N_KEYS // BF16_ROWS), axis=0)


def _peer_kernel(h2t_ref, u_ref, v_ref, n1_ref, p1_ref, rank2_ref, p2_ref,
                 xm_ref, gf_ref, out_ref, *, te):
    j = pl.program_id(1)

    @pl.when(j == 0)
    def _():
        out_ref[...] = jnp.zeros_like(out_ref)

    tm = h2t_ref.shape[1]
    rows_per_part = PEER_PART_ROWS
    acc = None
    for part in range(te // rows_per_part):
        lo = part * rows_per_part
        a = jnp.dot(u_ref[lo:lo + rows_per_part, :], h2t_ref[...],
                    preferred_element_type=F32)
        blocks = []
        for l in range(rows_per_part // N_KEYS):
            e1 = j * (te // N_KEYS) + part * (rows_per_part // N_KEYS) + l
            g = jnp.zeros((N_KEYS, tm), BF16)
            for h in range(PEER_HEADS):
                keep = rank2_ref[h] < _row_bf16(n1_ref, h, e1, tm)
                g = g + jnp.where(keep, p2_ref[h], 0.0) * _row_bf16(p1_ref, h, e1, tm)
            al = a[l * N_KEYS:(l + 1) * N_KEYS, :]
            gelu = 0.5 * al * (1.0 + lax.erf(al * INV_SQRT2))
            blocks.append(gelu.astype(BF16) * g)
        wt = jnp.concatenate(blocks, axis=0) if len(blocks) > 1 else blocks[0]
        d = jnp.dot(jnp.transpose(wt), v_ref[lo:lo + rows_per_part, :],
                    preferred_element_type=F32)
        acc = d if acc is None else acc + d
    out_ref[...] += acc

    @pl.when(j == pl.num_programs(1) - 1)
    def _():
        out_ref[...] = _rms(xm_ref[...] + out_ref[...], gf_ref[...])


def _peer(h2t, u_bf, v_bf, n1, p1, rank2, p2, xm, gf, tm, te):
    t = xm.shape[0]
    big = pl.BlockSpec((PEER_HEADS, N_KEYS, tm), lambda i, j: (0, 0, i))
    return pl.pallas_call(
        functools.partial(_peer_kernel, te=te),
        grid=(t // tm, N_EXPERTS // te),
        in_specs=[pl.BlockSpec((D_MODEL, tm), lambda i, j: (0, i)),
                  pl.BlockSpec((te, D_MODEL), lambda i, j: (j, 0)),
                  pl.BlockSpec((te, D_MODEL), lambda i, j: (j, 0)),
                  big, big, big, big,
                  pl.BlockSpec((tm, D_MODEL), lambda i, j: (i, 0)),
                  pl.BlockSpec((1, D_MODEL), lambda i, j: (0, 0))],
        out_specs=pl.BlockSpec((tm, D_MODEL), lambda i, j: (i, 0)),
        out_shape=jax.ShapeDtypeStruct((t, D_MODEL), F32),
        compiler_params=_params(("arbitrary", "arbitrary")),
        name="peer",
    )(h2t, u_bf, v_bf, n1, p1, rank2, p2, xm, gf)


def _channel_mixer(o, p32, p16, pa, pb, x, wts, tm_mix, tm_route, tm_peer, te, prompt, seq):
    xm, h2t, qp, utail = _mix(o, p32, p16, pa, pb, x, wts, tm_mix, prompt, seq)
    n1, p1, rank2, p2 = _route(qp, wts["keys"], tm_route)
    y = _peer(h2t, wts["u"], wts["v"], n1, p1, rank2, p2, xm, wts["gf"], tm_peer, te)
    return y, utail


def kernel(x_prompt, x_sample, state_gla, state_conv, norm1_g, w_in, w_alpha_up, b_alpha,
           gla_norm_g, conv_w, conv_b, w_out, norm2_g, w_query, sub_keys, expert_u, expert_v,
           norm_f_g):
    batch, seq, _ = x_prompt.shape
    n_dec = x_sample.shape[0]
    assert w_in.shape[0] == 1 and x_sample.shape[1] == 1

    w_lo, w_hi, w_alr = _wprep(w_in[0], tn=1024)
    wup = jnp.pad(w_alpha_up[0], ((0, LANES - GLA_RANK), (0, 0))).astype(BF16)
    ba = b_alpha[0][None, :]
    g1 = norm1_g[0][None, :]
    wts = dict(gg=gla_norm_g[0][None, :], cw=conv_w[0], cbias=conv_b[0][None, :],
               g2=norm2_g[0][None, :], keys=sub_keys[0],
               gf=norm_f_g[None, :])

    xp = x_prompt.reshape(batch * seq, D_MODEL)
    p32, p16, alr, wts["u"], wts["v"], wts["wo"], wts["wq"] = _inproj(
        xp, g1, w_lo, w_hi, w_alr, tm=1024, tn=1024,
        tables=(expert_u[0], expert_v[0], w_out[0], w_query[0]))
    o, s_p = _gla_prompt(p32, alr, wup, ba, batch, seq, rows=256)
    tm_mix = 256
    y_p, utail = _channel_mixer(o, p32, p16, p32, p32, xp, wts, tm_mix=tm_mix, tm_route=256,
                                tm_peer=512, te=1024, prompt=True, seq=seq)
    last = utail.reshape(batch, seq // tm_mix, SUBLANES, D_MODEL)[:, -1, SUBLANES - 2:, :]

    xs = x_sample.reshape(n_dec, D_MODEL)
    ps32, ps16, alrs = _inproj(xs, g1, w_lo, w_hi, w_alr, tm=n_dec, tn=1024)
    os_, s_s = _gla_sample(ps32.reshape(n_dec, 1, N_MAIN // 2),
                           alrs.reshape(n_dec, 1, LANES), wup, ba, state_gla[0], nb=2)
    y_s, us = _channel_mixer(os_.reshape(n_dec, D_MODEL), ps32, ps16, state_conv[0][:, 0, :],
                             state_conv[0][:, 1, :], xs, wts, tm_mix=n_dec, tm_route=n_dec,
                             tm_peer=n_dec, te=512, prompt=False, seq=1)
    conv_s = jnp.stack([state_conv[0][:, 1, :], us], axis=1)

    return (y_p.reshape(batch, seq, D_MODEL),
            y_s.reshape(n_dec, 1, D_MODEL),
            s_p[None],
            last[None],
            s_s[None],
            conv_s[None])
```

```python
import functools
import math

import jax
import jax.numpy as jnp
from jax import lax
from jax.experimental import pallas as pl
from jax.experimental.pallas import tpu as pltpu

F32 = jnp.float32
BF16 = jnp.bfloat16

D_MODEL = 2048
GLA_HEADS = 4
GLA_DK = 256
GLA_DV = 512
GLA_RANK = 16
GLA_TAU = 16.0
GLA_CHUNK = 64
CONV_WIDTH = 3
PEER_HEADS = 8
N_KEYS = 128
N_EXPERTS = N_KEYS * N_KEYS
PEER_TOPK = 16
DQ_HALF = 128
EPS = 1e-6

LANES = 128
SUBLANES = 8
BF16_ROWS = 16
VMEM_LIMIT_BYTES = 60000 * 1024

N_MAIN = 16384
P_SEGMENT = 4096
GLA_SUB = 8
GLA_HEADS_PER_STEP = 4
PEER_PART_ROWS = 512

TM_INPROJ = 1024
TN_INPROJ = 1024
GLA_ROWS = 256
GLA_SAMPLE_SEQS = 4
TM_MIX = 256
TM_ROUTE = 256
TM_PEER = 512
TE_PEER = 1024
NEG_INF = float("-inf")
INV_SQRT2 = 1.0 / math.sqrt(2.0)


def _params(semantics):
    return pltpu.CompilerParams(dimension_semantics=semantics,
                                vmem_limit_bytes=VMEM_LIMIT_BYTES)


def _rms(x, g):
    return x * lax.rsqrt(jnp.mean(x * x, axis=-1, keepdims=True) + EPS) * g


def _inproj_kernel(*refs, n_lo, tiles_per_seg, n_tables):
    x_ref, g_ref, wlo_ref, whi_ref, walr_ref = refs[:5]
    tables_in = refs[5:5 + n_tables]
    p32_ref, p16_ref, alr_ref = refs[5 + n_tables:8 + n_tables]
    tables_out = refs[8 + n_tables:8 + 2 * n_tables]
    h_ref = refs[-1]
    j = pl.program_id(1)

    @pl.when(j == 0)
    def _():
        hb = _rms(x_ref[...], g_ref[...]).astype(BF16)
        h_ref[...] = hb
        alr_ref[...] = jnp.dot(hb, walr_ref[...], preferred_element_type=F32)

    wide = (j // tiles_per_seg) % 2 == 0
    low = j < n_lo
    for w_ref, from_low in ((wlo_ref, True), (whi_ref, False)):
        for out_ref, is_wide in ((p32_ref, True), (p16_ref, False)):
            @pl.when((low == from_low) & (wide == is_wide))
            def _():
                out_ref[...] = jnp.dot(h_ref[...], w_ref[...],
                                       preferred_element_type=F32).astype(out_ref.dtype)

    for src_ref, dst_ref in zip(tables_in, tables_out):
        dst_ref[...] = src_ref[...].astype(BF16)


def _inproj(x, g, w_lo, w_hi, w_alr, tm, tn, tables=()):
    t = x.shape[0]
    n_lo = w_lo.shape[1] // tn
    per = P_SEGMENT // tn
    nj = N_MAIN // tn
    steps = (t // tm) * nj

    def idx32(j):
        seg, r = j // per, j % per
        return jnp.where(seg % 2 == 0, (seg // 2) * per + r, (seg // 2 + 1) * per - 1)

    def idx16(j):
        seg, r = j // per, j % per
        return jnp.where(seg % 2 == 1, (seg // 2) * per + r,
                         jnp.maximum((seg // 2) * per - 1, 0))

    slabs = [pl.BlockSpec((tb.shape[0] // steps, D_MODEL), lambda i, j: (i * nj + j, 0))
             for tb in tables]
    return pl.pallas_call(
        functools.partial(_inproj_kernel, n_lo=n_lo, tiles_per_seg=per,
                          n_tables=len(tables)),
        grid=(t // tm, nj),
        in_specs=[
            pl.BlockSpec((tm, D_MODEL), lambda i, j: (i, 0), pipeline_mode=pl.Buffered(1)),
            pl.BlockSpec((1, D_MODEL), lambda i, j: (0, 0)),
            pl.BlockSpec((D_MODEL, tn), lambda i, j: (0, jnp.minimum(j, n_lo - 1))),
            pl.BlockSpec((D_MODEL, tn), lambda i, j: (0, jnp.maximum(j - n_lo, 0))),
            pl.BlockSpec((D_MODEL, LANES), lambda i, j: (0, 0)),
        ] + slabs,
        out_specs=[
            pl.BlockSpec((tm, tn), lambda i, j: (i, idx32(j))),
            pl.BlockSpec((tm, tn), lambda i, j: (i, idx16(j))),
            pl.BlockSpec((tm, LANES), lambda i, j: (i, 0)),
        ] + slabs,
        out_shape=[
            jax.ShapeDtypeStruct((t, N_MAIN // 2), F32),
            jax.ShapeDtypeStruct((t, N_MAIN // 2), BF16),
            jax.ShapeDtypeStruct((t, LANES), F32),
        ] + [jax.ShapeDtypeStruct(tb.shape, BF16) for tb in tables],
        scratch_shapes=[pltpu.VMEM((tm, D_MODEL), BF16)],
        compiler_params=_params(("arbitrary", "arbitrary")),
        name="inproj",
    )(x, g, w_lo, w_hi, w_alr, *tables)


def _log_decay(alr, wup, ba):
    xa = jnp.dot(alr.astype(BF16), wup, preferred_element_type=F32) + ba
    return -(jnp.maximum(-xa, 0.0) + jnp.log1p(jnp.exp(-jnp.abs(xa)))) * (1.0 / GLA_TAU)


def _col_scale(row, width):
    col = jnp.transpose(jnp.broadcast_to(row, (LANES, row.shape[1])))
    return jnp.concatenate([col] * (width // LANES), axis=1)


def _gla_prompt_kernel(q_ref, k_ref, v_ref, alr_ref, wup_ref, ba_ref,
                       o_ref, sout_ref, s_ref, b_ref, *, rows):
    c = pl.program_id(2)

    @pl.when(c == 0)
    def _():
        s_ref[...] = jnp.zeros_like(s_ref)

    g = _log_decay(alr_ref[...], wup_ref[...], ba_ref[...])
    rmod = lax.broadcasted_iota(jnp.int32, g.shape, 0) & (GLA_CHUNK - 1)
    step = 1
    while step < GLA_CHUNK:
        g = g + jnp.where(rmod >= step, pltpu.roll(g, step, 0), 0.0)
        step *= 2
    b_ref[...] = g

    row = lax.broadcasted_iota(jnp.int32, (GLA_CHUNK, GLA_DK), 0)
    rsub = row & (GLA_SUB - 1)
    ri = lax.broadcasted_iota(jnp.int32, (GLA_CHUNK, GLA_CHUNK), 0)
    cj = lax.broadcasted_iota(jnp.int32, (GLA_CHUNK, GLA_CHUNK), 1)
    n_sub = GLA_CHUNK // GLA_SUB

    def head_chunk(r0, hh):
        dk = slice(hh * GLA_DK, (hh + 1) * GLA_DK)
        dv = slice(hh * GLA_DV, (hh + 1) * GLA_DV)
        qc = q_ref[pl.ds(r0, GLA_CHUNK), dk] * (GLA_DK ** -0.5)
        kc = k_ref[pl.ds(r0, GLA_CHUNK), dk]
        vc = v_ref[pl.ds(r0, GLA_CHUNK), dv]
        b = b_ref[pl.ds(r0, GLA_CHUNK), dk]
        bl = b[GLA_CHUNK - 1:GLA_CHUNK, :]
        s = s_ref[hh]
        vb = vc.astype(BF16)

        o = jnp.dot((qc * jnp.exp(b)).astype(BF16), s.astype(BF16),
                    preferred_element_type=F32)

        parts = [jnp.zeros((GLA_SUB, GLA_CHUNK), F32)]
        for blk in range(1, n_sub):
            lo = blk * GLA_SUB
            ref = b[lo - 1:lo, :]
            qs = qc[lo:lo + GLA_SUB, :] * jnp.exp(b[lo:lo + GLA_SUB, :] - ref)
            ks = kc * jnp.exp(jnp.where(row < lo, ref - b, NEG_INF))
            parts.append(lax.dot_general(qs.astype(BF16), ks.astype(BF16),
                                         (((1,), (1,)), ((), ())),
                                         preferred_element_type=F32))
        attn = jnp.concatenate(parts, axis=0)

        for d in range(GLA_SUB):
            if d == 0:
                t = qc * kc
            else:
                kd = pltpu.roll(kc, d, 0)
                bd = pltpu.roll(b, d, 0)
                t = qc * kd * jnp.exp(jnp.where(rsub >= d, b - bd, NEG_INF))
            cs = jnp.sum(t, axis=1, keepdims=True)
            attn = attn + jnp.where(ri - cj == d, cs, 0.0)

        o = o + jnp.dot(attn.astype(BF16), vb, preferred_element_type=F32)
        o_ref[pl.ds(r0, GLA_CHUNK), dv] = o

        kdec = (kc * jnp.exp(bl - b)).astype(BF16)
        s_ref[hh] = (_col_scale(jnp.exp(bl), GLA_DV) * s
                     + lax.dot_general(kdec, vb, (((0,), (0,)), ((), ())),
                                       preferred_element_type=F32))

    def chunk(ci, carry):
        r0 = pl.multiple_of(ci * GLA_CHUNK, GLA_CHUNK)
        for hh in range(GLA_HEADS_PER_STEP):
            head_chunk(r0, hh)
        return carry

    lax.fori_loop(0, rows // GLA_CHUNK, chunk, 0)

    @pl.when(c == pl.num_programs(2) - 1)
    def _():
        sout_ref[0] = s_ref[...]


def _gla_prompt(p, alr, wup, ba, batch, seq, rows):
    nsteps = seq // rows
    hps = GLA_HEADS_PER_STEP
    wk = hps * GLA_DK
    wv = hps * GLA_DV
    kq = GLA_HEADS * GLA_DK // wk
    kv = 2 * GLA_HEADS * GLA_DK // wv

    def rowmap(b, h, c):
        return b * nsteps + c

    return pl.pallas_call(
        functools.partial(_gla_prompt_kernel, rows=rows),
        grid=(batch, GLA_HEADS // hps, nsteps),
        in_specs=[
            pl.BlockSpec((rows, wk), lambda b, h, c: (rowmap(b, h, c), h)),
            pl.BlockSpec((rows, wk), lambda b, h, c: (rowmap(b, h, c), kq + h)),
            pl.BlockSpec((rows, wv), lambda b, h, c: (rowmap(b, h, c), kv + h)),
            pl.BlockSpec((rows, LANES), lambda b, h, c: (rowmap(b, h, c), 0)),
            pl.BlockSpec((LANES, wk), lambda b, h, c: (0, h)),
            pl.BlockSpec((1, wk), lambda b, h, c: (0, h)),
        ],
        out_specs=[
            pl.BlockSpec((rows, wv), lambda b, h, c: (rowmap(b, h, c), h)),
            pl.BlockSpec((1, hps, GLA_DK, GLA_DV), lambda b, h, c: (b, h, 0, 0)),
        ],
        out_shape=[
            jax.ShapeDtypeStruct((batch * seq, GLA_HEADS * GLA_DV), F32),
            jax.ShapeDtypeStruct((batch, GLA_HEADS, GLA_DK, GLA_DV), F32),
        ],
        scratch_shapes=[pltpu.VMEM((hps, GLA_DK, GLA_DV), F32),
                        pltpu.VMEM((rows, wk), F32)],
        compiler_params=_params(("arbitrary", "arbitrary", "arbitrary")),
        name="gla_prompt",
    )(p, p, p, alr, wup, ba)


def _gla_sample_kernel(q_ref, k_ref, v_ref, alr_ref, wup_ref, ba_ref, s0_ref,
                       o_ref, s1_ref):
    first = lax.broadcasted_iota(jnp.int32, (SUBLANES, GLA_DK), 0) == 0
    for bb in range(q_ref.shape[0]):
        g_all = _log_decay(jnp.broadcast_to(alr_ref[bb], (SUBLANES, LANES)),
                           wup_ref[...], ba_ref[...])
        for h in range(GLA_HEADS):
            dk = slice(h * GLA_DK, (h + 1) * GLA_DK)
            dv = slice(h * GLA_DV, (h + 1) * GLA_DV)
            q = jnp.broadcast_to(q_ref[bb, :, dk], (SUBLANES, GLA_DK)) * (GLA_DK ** -0.5)
            k = jnp.broadcast_to(k_ref[bb, :, dk], (SUBLANES, GLA_DK))
            v = jnp.broadcast_to(v_ref[bb, :, dv], (SUBLANES, GLA_DV))
            g = g_all[:, dk]
            s = s0_ref[bb, h]
            attn = jnp.sum(q * k, axis=1, keepdims=True)
            o = attn * v + jnp.dot((q * jnp.exp(g)).astype(BF16), s.astype(BF16),
                                   preferred_element_type=F32)
            o_ref[bb, :, dv] = o[0:1, :]
            k1 = jnp.where(first, k, 0.0).astype(BF16)
            s1_ref[bb, h] = (_col_scale(jnp.exp(g[0:1, :]), GLA_DV) * s
                             + lax.dot_general(k1, v.astype(BF16), (((0,), (0,)), ((), ())),
                                               preferred_element_type=F32))


def _gla_sample(p3, alr3, wup, ba, s0, nb):
    n = p3.shape[0]
    qk = GLA_HEADS * GLA_DK
    vw = GLA_HEADS * GLA_DV
    return pl.pallas_call(
        _gla_sample_kernel,
        grid=(n // nb,),
        in_specs=[
            pl.BlockSpec((nb, 1, qk), lambda b: (b, 0, 0)),
            pl.BlockSpec((nb, 1, qk), lambda b: (b, 0, 1)),
            pl.BlockSpec((nb, 1, vw), lambda b: (b, 0, 2 * qk // vw)),
            pl.BlockSpec((nb, 1, LANES), lambda b: (b, 0, 0)),
            pl.BlockSpec((LANES, qk), lambda b: (0, 0)),
            pl.BlockSpec((1, qk), lambda b: (0, 0)),
            pl.BlockSpec((nb, GLA_HEADS, GLA_DK, GLA_DV), lambda b: (b, 0, 0, 0)),
        ],
        out_specs=[
            pl.BlockSpec((nb, 1, vw), lambda b: (b, 0, 0)),
            pl.BlockSpec((nb, GLA_HEADS, GLA_DK, GLA_DV), lambda b: (b, 0, 0, 0)),
        ],
        out_shape=[
            jax.ShapeDtypeStruct((n, 1, vw), F32),
            jax.ShapeDtypeStruct((n, GLA_HEADS, GLA_DK, GLA_DV), F32),
        ],
        compiler_params=_params(("arbitrary",)),
        name="gla_sample",
    )(p3, p3, p3, alr3, wup, ba, s0)


def _mix_kernel(o_ref, r_ref, cb_ref, cc_ref, cx_ref, ga_ref, gb_ref, pa_ref, pb_ref,
                gg_ref, cw_ref, cbias_ref, x_ref, wo_ref, g2_ref, wq_ref,
                xm_ref, h2t_ref, qp_ref, u_ref, m_ref, *, prompt, tiles_per_seq):
    tm = x_ref.shape[0]
    if prompt:
        start = (pl.program_id(0) % tiles_per_seq) == 0
        row = lax.broadcasted_iota(jnp.int32, (tm, GLA_DV), 0)
    for h in range(GLA_HEADS):
        dv = slice(h * GLA_DV, (h + 1) * GLA_DV)
        u = cc_ref[:, dv] * cx_ref[:, dv]
        if prompt:
            prev = jnp.where(start, 0.0, pa_ref[:, dv] * pb_ref[:, dv])
            um1 = jnp.where(row == 0, prev[7:8, :], pltpu.roll(u, 1, 0))
            um2 = jnp.where(row == 0, prev[6:7, :],
                            jnp.where(row == 1, prev[7:8, :], pltpu.roll(u, 2, 0)))
            u_ref[0, :, dv] = u[tm - SUBLANES:, :]
        else:
            um2 = pa_ref[:, dv]
            um1 = pb_ref[:, dv]
            u_ref[:, dv] = u
        z = (cbias_ref[:, dv] + um2 * cw_ref[0:1, dv] + um1 * cw_ref[1:2, dv]
             + u * cw_ref[2:3, dv])
        y_b = cb_ref[:, dv].astype(F32) * z
        r = r_ref[:, dv].astype(F32)
        on = _rms(o_ref[:, dv], gg_ref[:, dv])
        m = (jax.nn.sigmoid(gb_ref[:, dv].astype(F32)) * y_b
             + jax.nn.sigmoid(ga_ref[:, dv].astype(F32)) * (r * jax.nn.sigmoid(r) * on))
        m_ref[:, dv] = m.astype(BF16)
    xm = x_ref[...] + jnp.dot(m_ref[...], wo_ref[...], preferred_element_type=F32)
    xm_ref[...] = xm
    h2 = _rms(xm, g2_ref[...])
    h2t_ref[...] = jnp.transpose(h2).astype(BF16)
    qp_ref[...] = jnp.dot(h2.astype(BF16), wq_ref[...], preferred_element_type=F32)


def _mix(o, p32, p16, pa, pb, x, wts, tm, prompt, seq):
    t = o.shape[0]
    n_tiles = t // tm
    tile = pl.BlockSpec((tm, D_MODEL), lambda i: (i, 0))
    col = lambda c: pl.BlockSpec((tm, D_MODEL), lambda i: (i, c))
    cc_col = P_SEGMENT // D_MODEL
    if prompt:
        prev_rows = tm // SUBLANES
        prev = lambda c: pl.BlockSpec(
            (SUBLANES, D_MODEL), lambda i: (jnp.maximum(i * prev_rows - 1, 0), c))
        pspecs = [prev(cc_col), prev(cc_col + 1)]
        uspec = pl.BlockSpec((1, SUBLANES, D_MODEL), lambda i: (i, 0, 0))
        ushape = jax.ShapeDtypeStruct((n_tiles, SUBLANES, D_MODEL), F32)
    else:
        pspecs = [tile, tile]
        uspec = tile
        ushape = jax.ShapeDtypeStruct((t, D_MODEL), F32)
    vec = lambda r: pl.BlockSpec((r, D_MODEL), lambda i: (0, 0))
    whole = pl.BlockSpec((D_MODEL, D_MODEL), lambda i: (0, 0),
                         pipeline_mode=pl.Buffered(1))
    return pl.pallas_call(
        functools.partial(_mix_kernel, prompt=prompt, tiles_per_seq=max(seq // tm, 1)),
        grid=(n_tiles,),
        in_specs=[tile, col(0), col(1), col(cc_col), col(cc_col + 1), col(2), col(3)]
                 + pspecs + [vec(1), vec(CONV_WIDTH), vec(1), tile, whole, vec(1), whole],
        out_specs=[tile, pl.BlockSpec((D_MODEL, tm), lambda i: (0, i)), tile, uspec],
        out_shape=[jax.ShapeDtypeStruct((t, D_MODEL), F32),
                   jax.ShapeDtypeStruct((D_MODEL, t), BF16),
                   jax.ShapeDtypeStruct((t, D_MODEL), F32),
                   ushape],
        scratch_shapes=[pltpu.VMEM((tm, D_MODEL), BF16)],
        compiler_params=_params(("arbitrary",)),
        name="mix_prompt" if prompt else "mix_sample",
    )(o, p16, p16, p32, p32, p16, p16, pa, pb, wts["gg"], wts["cw"], wts["cbias"],
      x, wts["wo"], wts["g2"], wts["wq"])


def _top_values(work, n):
    vals = []
    rank = jnp.full(work.shape, float(n), F32)
    for r in range(n):
        mx = jnp.max(work, axis=0, keepdims=True)
        hit = work == mx
        vals.append(mx)
        rank = jnp.where(hit, float(r), rank)
        work = jnp.where(hit, NEG_INF, work)
    return vals, rank


def _route_kernel(qp_ref, keys_ref, n1_ref, p1_ref, rank2_ref, p2_ref):
    for h in range(PEER_HEADS):
        st = []
        for side in range(2):
            c0 = (2 * h + side) * DQ_HALF
            st.append(lax.dot_general(keys_ref[side], qp_ref[:, c0:c0 + DQ_HALF],
                                      (((1,), (1,)), ((), ())),
                                      precision=lax.Precision.HIGHEST,
                                      preferred_element_type=F32))
        v1, rank1 = _top_values(st[0], PEER_TOPK)
        v2, rank2 = _top_values(st[1], PEER_TOPK)
        v2 = jnp.concatenate(v2, axis=0)
        tm = v2.shape[1]
        bidx = lax.broadcasted_iota(jnp.int32, (SUBLANES, tm), 0)
        cand = []
        for a in range(PEER_TOPK // 2):
            nb = PEER_TOPK // (a + 1)
            if nb >= SUBLANES:
                cand.append(v1[a] + v2[0:nb, :])
            else:
                cand.append(jnp.where(bidx < nb, v1[a] + v2[0:SUBLANES, :], NEG_INF))
        tail = jnp.concatenate(v1[PEER_TOPK // 2:], axis=0) + v2[0:1, :]
        cand.append(tail)
        thr = _top_values(jnp.concatenate(cand, axis=0), PEER_TOPK)[0][-1]
        m1 = v1[0]
        m2 = v2[0:1, :]
        z = jnp.zeros_like(m1)
        n1 = jnp.zeros_like(rank1)
        for a, ca in enumerate(cand):
            keep = ca >= thr
            z = z + jnp.sum(jnp.where(keep, jnp.exp(ca - (m1 + m2)), 0.0),
                            axis=0, keepdims=True)
            kept = jnp.where(keep, 1.0, 0.0)
            if a < PEER_TOPK // 2:
                n1 = jnp.where(rank1 == float(a), jnp.sum(kept, axis=0, keepdims=True), n1)
            else:
                for r in range(PEER_TOPK // 2):
                    n1 = jnp.where(rank1 == float(a + r), kept[r:r + 1, :], n1)
        n1_ref[h] = n1
        p1_ref[h] = jnp.exp(st[0] - m1)
        rank2_ref[h] = rank2.astype(BF16)
        p2_ref[h] = (jnp.exp(st[1] - m2) / z).astype(BF16)


def _route(qp, keys, tm):
    t = qp.shape[0]
    big = pl.BlockSpec((PEER_HEADS, N_KEYS, tm), lambda i: (0, 0, i))
    f32s = jax.ShapeDtypeStruct((PEER_HEADS, N_KEYS, t), F32)
    bf16s = jax.ShapeDtypeStruct((PEER_HEADS, N_KEYS, t), BF16)
    return pl.pallas_call(
        _route_kernel,
        grid=(t // tm,),
        in_specs=[pl.BlockSpec((tm, D_MODEL), lambda i: (i, 0)),
                  pl.BlockSpec((2, N_KEYS, DQ_HALF), lambda i: (0, 0, 0))],
        out_specs=[big, big, big, big],
        out_shape=[f32s, f32s, bf16s, bf16s],
        compiler_params=_params(("arbitrary",)),
        name="route",
    )(qp, keys)


def _row_bf16(ref, h, e1, tm):
    row = jnp.broadcast_to(ref[h, pl.ds(e1, 1), :], (BF16_ROWS, tm)).astype(BF16)
    return jnp.concatenate([row] * (N_KEYS // BF16_ROWS), axis=0)


def _peer_kernel(h2t_ref, u_ref, v_ref, n1_ref, p1_ref, rank2_ref, p2_ref,
                 xm_ref, gf_ref, out_ref, *, te):
    j = pl.program_id(1)

    @pl.when(j == 0)
    def _():
        out_ref[...] = jnp.zeros_like(out_ref)

    tm = h2t_ref.shape[1]
    rows_per_part = PEER_PART_ROWS
    acc = None
    for part in range(te // rows_per_part):
        lo = part * rows_per_part
        a = jnp.dot(u_ref[lo:lo + rows_per_part, :], h2t_ref[...],
                    preferred_element_type=F32)
        blocks = []
        for l in range(rows_per_part // N_KEYS):
            e1 = j * (te // N_KEYS) + part * (rows_per_part // N_KEYS) + l
            g = jnp.zeros((N_KEYS, tm), BF16)
            for h in range(PEER_HEADS):
                keep = rank2_ref[h] < _row_bf16(n1_ref, h, e1, tm)
                g = g + jnp.where(keep, p2_ref[h], 0.0) * _row_bf16(p1_ref, h, e1, tm)
            al = a[l * N_KEYS:(l + 1) * N_KEYS, :]
            gelu = 0.5 * al * (1.0 + lax.erf(al * INV_SQRT2))
            blocks.append(gelu.astype(BF16) * g)
        wt = jnp.concatenate(blocks, axis=0) if len(blocks) > 1 else blocks[0]
        d = jnp.dot(jnp.transpose(wt), v_ref[lo:lo + rows_per_part, :],
                    preferred_element_type=F32)
        acc = d if acc is None else acc + d
    out_ref[...] += acc

    @pl.when(j == pl.num_programs(1) - 1)
    def _():
        out_ref[...] = _rms(xm_ref[...] + out_ref[...], gf_ref[...])


def _peer(h2t, u_bf, v_bf, n1, p1, rank2, p2, xm, gf, tm, te):
    t = xm.shape[0]
    big = pl.BlockSpec((PEER_HEADS, N_KEYS, tm), lambda i, j: (0, 0, i))
    return pl.pallas_call(
        functools.partial(_peer_kernel, te=te),
        grid=(t // tm, N_EXPERTS // te),
        in_specs=[pl.BlockSpec((D_MODEL, tm), lambda i, j: (0, i)),
                  pl.BlockSpec((te, D_MODEL), lambda i, j: (j, 0)),
                  pl.BlockSpec((te, D_MODEL), lambda i, j: (j, 0)),
                  big, big, big, big,
                  pl.BlockSpec((tm, D_MODEL), lambda i, j: (i, 0)),
                  pl.BlockSpec((1, D_MODEL), lambda i, j: (0, 0))],
        out_specs=pl.BlockSpec((tm, D_MODEL), lambda i, j: (i, 0)),
        out_shape=jax.ShapeDtypeStruct((t, D_MODEL), F32),
        compiler_params=_params(("arbitrary", "arbitrary")),
        name="peer",
    )(h2t, u_bf, v_bf, n1, p1, rank2, p2, xm, gf)


def _channel_mixer(o, p32, p16, pa, pb, x, wts, tm_mix, tm_route, tm_peer, te, prompt, seq):
    xm, h2t, qp, utail = _mix(o, p32, p16, pa, pb, x, wts, tm_mix, prompt, seq)
    n1, p1, rank2, p2 = _route(qp, wts["keys"], tm_route)
    y = _peer(h2t, wts["u"], wts["v"], n1, p1, rank2, p2, xm, wts["gf"], tm_peer, te)
    return y, utail


def kernel(x_prompt, x_sample, state_gla, state_conv, norm1_g, w_in, w_alpha_up, b_alpha,
           gla_norm_g, conv_w, conv_b, w_out, norm2_g, w_query, sub_keys, expert_u, expert_v,
           norm_f_g):
    batch, seq, _ = x_prompt.shape
    n_dec = x_sample.shape[0]
    assert w_in.shape[0] == 1 and x_sample.shape[1] == 1

    alr0 = 2 * GLA_HEADS * GLA_DK + 2 * GLA_HEADS * GLA_DV
    w_lo = w_in[0][:, :alr0].astype(BF16)
    w_hi = w_in[0][:, alr0 + GLA_RANK:].astype(BF16)
    w_alr = jnp.pad(w_in[0][:, alr0:alr0 + GLA_RANK],
                    ((0, 0), (0, LANES - GLA_RANK))).astype(BF16)
    wup = jnp.pad(w_alpha_up[0], ((0, LANES - GLA_RANK), (0, 0))).astype(BF16)
    ba = b_alpha[0][None, :]
    g1 = norm1_g[0][None, :]
    wts = dict(gg=gla_norm_g[0][None, :], cw=conv_w[0], cbias=conv_b[0][None, :],
               g2=norm2_g[0][None, :], keys=sub_keys[0],
               gf=norm_f_g[None, :])

    xp = x_prompt.reshape(batch * seq, D_MODEL)
    p32, p16, alr, wts["u"], wts["v"], wts["wo"], wts["wq"] = _inproj(
        xp, g1, w_lo, w_hi, w_alr, tm=TM_INPROJ, tn=TN_INPROJ,
        tables=(expert_u[0], expert_v[0], w_out[0], w_query[0]))
    o, s_p = _gla_prompt(p32, alr, wup, ba, batch, seq, rows=GLA_ROWS)
    y_p, utail = _channel_mixer(o, p32, p16, p32, p32, xp, wts, tm_mix=TM_MIX,
                                tm_route=TM_ROUTE, tm_peer=TM_PEER, te=TE_PEER,
                                prompt=True, seq=seq)
    last = utail.reshape(batch, seq // TM_MIX, SUBLANES, D_MODEL)[:, -1, SUBLANES - 2:, :]

    xs = x_sample.reshape(n_dec, D_MODEL)
    ps32, ps16, alrs = _inproj(xs, g1, w_lo, w_hi, w_alr, tm=n_dec, tn=TN_INPROJ)
    os_, s_s = _gla_sample(ps32.reshape(n_dec, 1, N_MAIN // 2),
                           alrs.reshape(n_dec, 1, LANES), wup, ba, state_gla[0],
                           nb=GLA_SAMPLE_SEQS)
    y_s, us = _channel_mixer(os_.reshape(n_dec, D_MODEL), ps32, ps16, state_conv[0][:, 0, :],
                             state_conv[0][:, 1, :], xs, wts, tm_mix=n_dec, tm_route=n_dec,
                             tm_peer=n_dec, te=TE_PEER, prompt=False, seq=1)
    conv_s = jnp.stack([state_conv[0][:, 1, :], us], axis=1)

    return (y_p.reshape(batch, seq, D_MODEL),
            y_s.reshape(n_dec, 1, D_MODEL),
            s_p[None],
            last[None],
            s_s[None],
            conv_s[None])
```

```python
import functools
import math

import jax
import jax.numpy as jnp
from jax import lax
from jax.experimental import pallas as pl
from jax.experimental.pallas import tpu as pltpu

F32 = jnp.float32
BF16 = jnp.bfloat16

D_MODEL = 2048
GLA_HEADS = 4
GLA_DK = 256
GLA_DV = 512
GLA_RANK = 16
GLA_TAU = 16.0
GLA_CHUNK = 64
CONV_WIDTH = 3
PEER_HEADS = 8
N_KEYS = 128
N_EXPERTS = N_KEYS * N_KEYS
PEER_TOPK = 16
DQ_HALF = 128
EPS = 1e-6

LANES = 128
SUBLANES = 8
BF16_ROWS = 16
VMEM_LIMIT_BYTES = 60000 * 1024

N_MAIN = 16384
P_SEGMENT = 4096
GLA_SUB = 8
GLA_HEADS_PER_STEP = 4
PEER_PART_ROWS = 512

TM_INPROJ = 1024
TN_INPROJ = 1024
GLA_ROWS = 256
GLA_SAMPLE_SEQS = 4
TM_MIX = 256
TM_ROUTE = 256
TM_PEER = 512
TE_PEER = 1024
NEG_INF = float("-inf")
INV_SQRT2 = 1.0 / math.sqrt(2.0)


def _params(semantics):
    return pltpu.CompilerParams(dimension_semantics=semantics,
                                vmem_limit_bytes=VMEM_LIMIT_BYTES)


def _rms(x, g):
    return x * lax.rsqrt(jnp.mean(x * x, axis=-1, keepdims=True) + EPS) * g


def _inproj_kernel(*refs, n_lo, tiles_per_seg, n_tables):
    x_ref, g_ref, wlo_ref, whi_ref, walr_ref = refs[:5]
    tables_in = refs[5:5 + n_tables]
    p32_ref, p16_ref, alr_ref = refs[5 + n_tables:8 + n_tables]
    tables_out = refs[8 + n_tables:8 + 2 * n_tables]
    h_ref = refs[-1]
    j = pl.program_id(1)

    @pl.when(j == 0)
    def _():
        hb = _rms(x_ref[...], g_ref[...]).astype(BF16)
        h_ref[...] = hb
        alr_ref[...] = jnp.dot(hb, walr_ref[...], preferred_element_type=F32)

    wide = (j // tiles_per_seg) % 2 == 0
    low = j < n_lo
    for w_ref, from_low in ((wlo_ref, True), (whi_ref, False)):
        for out_ref, is_wide in ((p32_ref, True), (p16_ref, False)):
            @pl.when((low == from_low) & (wide == is_wide))
            def _():
                out_ref[...] = jnp.dot(h_ref[...], w_ref[...],
                                       preferred_element_type=F32).astype(out_ref.dtype)

    for src_ref, dst_ref in zip(tables_in, tables_out):
        dst_ref[...] = src_ref[...].astype(BF16)


def _inproj(x, g, w_lo, w_hi, w_alr, tm, tn, tables=()):
    t = x.shape[0]
    n_lo = w_lo.shape[1] // tn
    per = P_SEGMENT // tn
    nj = N_MAIN // tn
    steps = (t // tm) * nj

    def idx32(j):
        seg, r = j // per, j % per
        return jnp.where(seg % 2 == 0, (seg // 2) * per + r, (seg // 2 + 1) * per - 1)

    def idx16(j):
        seg, r = j // per, j % per
        return jnp.where(seg % 2 == 1, (seg // 2) * per + r,
                         jnp.maximum((seg // 2) * per - 1, 0))

    slabs = [pl.BlockSpec((tb.shape[0] // steps, D_MODEL), lambda i, j: (i * nj + j, 0))
             for tb in tables]
    return pl.pallas_call(
        functools.partial(_inproj_kernel, n_lo=n_lo, tiles_per_seg=per,
                          n_tables=len(tables)),
        grid=(t // tm, nj),
        in_specs=[
            pl.BlockSpec((tm, D_MODEL), lambda i, j: (i, 0), pipeline_mode=pl.Buffered(1)),
            pl.BlockSpec((1, D_MODEL), lambda i, j: (0, 0)),
            pl.BlockSpec((D_MODEL, tn), lambda i, j: (0, jnp.minimum(j, n_lo - 1))),
            pl.BlockSpec((D_MODEL, tn), lambda i, j: (0, jnp.maximum(j - n_lo, 0))),
            pl.BlockSpec((D_MODEL, LANES), lambda i, j: (0, 0)),
        ] + slabs,
        out_specs=[
            pl.BlockSpec((tm, tn), lambda i, j: (i, idx32(j))),
            pl.BlockSpec((tm, tn), lambda i, j: (i, idx16(j))),
            pl.BlockSpec((tm, LANES), lambda i, j: (i, 0)),
        ] + slabs,
        out_shape=[
            jax.ShapeDtypeStruct((t, N_MAIN // 2), F32),
            jax.ShapeDtypeStruct((t, N_MAIN // 2), BF16),
            jax.ShapeDtypeStruct((t, LANES), F32),
        ] + [jax.ShapeDtypeStruct(tb.shape, BF16) for tb in tables],
        scratch_shapes=[pltpu.VMEM((tm, D_MODEL), BF16)],
        compiler_params=_params(("arbitrary", "arbitrary")),
        name="inproj",
    )(x, g, w_lo, w_hi, w_alr, *tables)


def _log_decay(alr, wup, ba):
    xa = jnp.dot(alr.astype(BF16), wup, preferred_element_type=F32) + ba
    return -(jnp.maximum(-xa, 0.0) + jnp.log1p(jnp.exp(-jnp.abs(xa)))) * (1.0 / GLA_TAU)


def _col_scale(row, width):
    col = jnp.transpose(jnp.broadcast_to(row, (LANES, row.shape[1])))
    return jnp.concatenate([col] * (width // LANES), axis=1)


def _gla_prompt_kernel(q_ref, k_ref, v_ref, alr_ref, wup_ref, ba_ref,
                       o_ref, sout_ref, s_ref, b_ref, *, rows):
    c = pl.program_id(2)

    @pl.when(c == 0)
    def _():
        s_ref[...] = jnp.zeros_like(s_ref)

    g = _log_decay(alr_ref[...], wup_ref[...], ba_ref[...])
    rmod = lax.broadcasted_iota(jnp.int32, g.shape, 0) & (GLA_CHUNK - 1)
    step = 1
    while step < GLA_CHUNK:
        g = g + jnp.where(rmod >= step, pltpu.roll(g, step, 0), 0.0)
        step *= 2
    b_ref[...] = g

    row = lax.broadcasted_iota(jnp.int32, (GLA_CHUNK, GLA_DK), 0)
    rsub = row & (GLA_SUB - 1)
    ri = lax.broadcasted_iota(jnp.int32, (GLA_CHUNK, GLA_CHUNK), 0)
    cj = lax.broadcasted_iota(jnp.int32, (GLA_CHUNK, GLA_CHUNK), 1)
    n_sub = GLA_CHUNK // GLA_SUB

    def head_chunk(r0, hh):
        dk = slice(hh * GLA_DK, (hh + 1) * GLA_DK)
        dv = slice(hh * GLA_DV, (hh + 1) * GLA_DV)
        qc = q_ref[pl.ds(r0, GLA_CHUNK), dk] * (GLA_DK ** -0.5)
        kc = k_ref[pl.ds(r0, GLA_CHUNK), dk]
        vc = v_ref[pl.ds(r0, GLA_CHUNK), dv]
        b = b_ref[pl.ds(r0, GLA_CHUNK), dk]
        bl = b[GLA_CHUNK - 1:GLA_CHUNK, :]
        s = s_ref[hh]
        vb = vc.astype(BF16)

        o = jnp.dot((qc * jnp.exp(b)).astype(BF16), s.astype(BF16),
                    preferred_element_type=F32)

        parts = [jnp.zeros((GLA_SUB, GLA_CHUNK), F32)]
        for blk in range(1, n_sub):
            lo = blk * GLA_SUB
            ref = b[lo - 1:lo, :]
            qs = qc[lo:lo + GLA_SUB, :] * jnp.exp(b[lo:lo + GLA_SUB, :] - ref)
            ks = kc * jnp.exp(jnp.where(row < lo, ref - b, NEG_INF))
            parts.append(lax.dot_general(qs.astype(BF16), ks.astype(BF16),
                                         (((1,), (1,)), ((), ())),
                                         preferred_element_type=F32))
        attn = jnp.concatenate(parts, axis=0)

        for d in range(GLA_SUB):
            if d == 0:
                t = qc * kc
            else:
                kd = pltpu.roll(kc, d, 0)
                bd = pltpu.roll(b, d, 0)
                t = qc * kd * jnp.exp(jnp.where(rsub >= d, b - bd, NEG_INF))
            cs = jnp.sum(t, axis=1, keepdims=True)
            attn = attn + jnp.where(ri - cj == d, cs, 0.0)

        o = o + jnp.dot(attn.astype(BF16), vb, preferred_element_type=F32)
        o_ref[pl.ds(r0, GLA_CHUNK), dv] = o

        kdec = (kc * jnp.exp(bl - b)).astype(BF16)
        s_ref[hh] = (_col_scale(jnp.exp(bl), GLA_DV) * s
                     + lax.dot_general(kdec, vb, (((0,), (0,)), ((), ())),
                                       preferred_element_type=F32))

    def chunk(ci, carry):
        r0 = pl.multiple_of(ci * GLA_CHUNK, GLA_CHUNK)
        for hh in range(GLA_HEADS_PER_STEP):
            head_chunk(r0, hh)
        return carry

    lax.fori_loop(0, rows // GLA_CHUNK, chunk, 0)

    @pl.when(c == pl.num_programs(2) - 1)
    def _():
        sout_ref[0] = s_ref[...]


def _gla_prompt(p, alr, wup, ba, batch, seq, rows):
    nsteps = seq // rows
    hps = GLA_HEADS_PER_STEP
    wk = hps * GLA_DK
    wv = hps * GLA_DV
    kq = GLA_HEADS * GLA_DK // wk
    kv = 2 * GLA_HEADS * GLA_DK // wv

    def rowmap(b, h, c):
        return b * nsteps + c

    return pl.pallas_call(
        functools.partial(_gla_prompt_kernel, rows=rows),
        grid=(batch, GLA_HEADS // hps, nsteps),
        in_specs=[
            pl.BlockSpec((rows, wk), lambda b, h, c: (rowmap(b, h, c), h)),
            pl.BlockSpec((rows, wk), lambda b, h, c: (rowmap(b, h, c), kq + h)),
            pl.BlockSpec((rows, wv), lambda b, h, c: (rowmap(b, h, c), kv + h)),
            pl.BlockSpec((rows, LANES), lambda b, h, c: (rowmap(b, h, c), 0)),
            pl.BlockSpec((LANES, wk), lambda b, h, c: (0, h)),
            pl.BlockSpec((1, wk), lambda b, h, c: (0, h)),
        ],
        out_specs=[
            pl.BlockSpec((rows, wv), lambda b, h, c: (rowmap(b, h, c), h)),
            pl.BlockSpec((1, hps, GLA_DK, GLA_DV), lambda b, h, c: (b, h, 0, 0)),
        ],
        out_shape=[
            jax.ShapeDtypeStruct((batch * seq, GLA_HEADS * GLA_DV), F32),
            jax.ShapeDtypeStruct((batch, GLA_HEADS, GLA_DK, GLA_DV), F32),
        ],
        scratch_shapes=[pltpu.VMEM((hps, GLA_DK, GLA_DV), F32),
                        pltpu.VMEM((rows, wk), F32)],
        compiler_params=_params(("arbitrary", "arbitrary", "arbitrary")),
        name="gla_prompt",
    )(p, p, p, alr, wup, ba)


def _gla_sample_kernel(q_ref, k_ref, v_ref, alr_ref, wup_ref, ba_ref, s0_ref,
                       o_ref, s1_ref):
    first = lax.broadcasted_iota(jnp.int32, (SUBLANES, GLA_DK), 0) == 0
    for bb in range(q_ref.shape[0]):
        g_all = _log_decay(jnp.broadcast_to(alr_ref[bb], (SUBLANES, LANES)),
                           wup_ref[...], ba_ref[...])
        for h in range(GLA_HEADS):
            dk = slice(h * GLA_DK, (h + 1) * GLA_DK)
            dv = slice(h * GLA_DV, (h + 1) * GLA_DV)
            q = jnp.broadcast_to(q_ref[bb, :, dk], (SUBLANES, GLA_DK)) * (GLA_DK ** -0.5)
            k = jnp.broadcast_to(k_ref[bb, :, dk], (SUBLANES, GLA_DK))
            v = jnp.broadcast_to(v_ref[bb, :, dv], (SUBLANES, GLA_DV))
            g = g_all[:, dk]
            s = s0_ref[bb, h]
            attn = jnp.sum(q * k, axis=1, keepdims=True)
            o = attn * v + jnp.dot((q * jnp.exp(g)).astype(BF16), s.astype(BF16),
                                   preferred_element_type=F32)
            o_ref[bb, :, dv] = o[0:1, :]
            k1 = jnp.where(first, k, 0.0).astype(BF16)
            s1_ref[bb, h] = (_col_scale(jnp.exp(g[0:1, :]), GLA_DV) * s
                             + lax.dot_general(k1, v.astype(BF16), (((0,), (0,)), ((), ())),
                                               preferred_element_type=F32))


def _gla_sample(p3, alr3, wup, ba, s0, nb):
    n = p3.shape[0]
    qk = GLA_HEADS * GLA_DK
    vw = GLA_HEADS * GLA_DV
    return pl.pallas_call(
        _gla_sample_kernel,
        grid=(n // nb,),
        in_specs=[
            pl.BlockSpec((nb, 1, qk), lambda b: (b, 0, 0)),
            pl.BlockSpec((nb, 1, qk), lambda b: (b, 0, 1)),
            pl.BlockSpec((nb, 1, vw), lambda b: (b, 0, 2 * qk // vw)),
            pl.BlockSpec((nb, 1, LANES), lambda b: (b, 0, 0)),
            pl.BlockSpec((LANES, qk), lambda b: (0, 0)),
            pl.BlockSpec((1, qk), lambda b: (0, 0)),
            pl.BlockSpec((nb, GLA_HEADS, GLA_DK, GLA_DV), lambda b: (b, 0, 0, 0)),
        ],
        out_specs=[
            pl.BlockSpec((nb, 1, vw), lambda b: (b, 0, 0)),
            pl.BlockSpec((nb, GLA_HEADS, GLA_DK, GLA_DV), lambda b: (b, 0, 0, 0)),
        ],
        out_shape=[
            jax.ShapeDtypeStruct((n, 1, vw), F32),
            jax.ShapeDtypeStruct((n, GLA_HEADS, GLA_DK, GLA_DV), F32),
        ],
        compiler_params=_params(("arbitrary",)),
        name="gla_sample",
    )(p3, p3, p3, alr3, wup, ba, s0)


def _mix_kernel(o_ref, r_ref, cb_ref, cc_ref, cx_ref, ga_ref, gb_ref, pa_ref, pb_ref,
                gg_ref, cw_ref, cbias_ref, x_ref, wo_ref, g2_ref, wq_ref,
                xm_ref, h2t_ref, qp_ref, u_ref, m_ref, *, prompt, tiles_per_seq):
    tm = x_ref.shape[0]
    if prompt:
        start = (pl.program_id(0) % tiles_per_seq) == 0
        row = lax.broadcasted_iota(jnp.int32, (tm, GLA_DV), 0)
    for h in range(GLA_HEADS):
        dv = slice(h * GLA_DV, (h + 1) * GLA_DV)
        u = cc_ref[:, dv] * cx_ref[:, dv]
        if prompt:
            prev = jnp.where(start, 0.0, pa_ref[:, dv] * pb_ref[:, dv])
            um1 = jnp.where(row == 0, prev[7:8, :], pltpu.roll(u, 1, 0))
            um2 = jnp.where(row == 0, prev[6:7, :],
                            jnp.where(row == 1, prev[7:8, :], pltpu.roll(u, 2, 0)))
            u_ref[0, :, dv] = u[tm - SUBLANES:, :]
        else:
            um2 = pa_ref[:, dv]
            um1 = pb_ref[:, dv]
            u_ref[:, dv] = u
        z = (cbias_ref[:, dv] + um2 * cw_ref[0:1, dv] + um1 * cw_ref[1:2, dv]
             + u * cw_ref[2:3, dv])
        y_b = cb_ref[:, dv].astype(F32) * z
        r = r_ref[:, dv].astype(F32)
        on = _rms(o_ref[:, dv], gg_ref[:, dv])
        m = (jax.nn.sigmoid(gb_ref[:, dv].astype(F32)) * y_b
             + jax.nn.sigmoid(ga_ref[:, dv].astype(F32)) * (r * jax.nn.sigmoid(r) * on))
        m_ref[:, dv] = m.astype(BF16)
    xm = x_ref[...] + jnp.dot(m_ref[...], wo_ref[...], preferred_element_type=F32)
    xm_ref[...] = xm
    h2 = _rms(xm, g2_ref[...])
    h2t_ref[...] = jnp.transpose(h2).astype(BF16)
    qp_ref[...] = jnp.dot(h2.astype(BF16), wq_ref[...], preferred_element_type=F32)


def _mix(o, p32, p16, pa, pb, x, wts, tm, prompt, seq):
    t = o.shape[0]
    n_tiles = t // tm
    tile = pl.BlockSpec((tm, D_MODEL), lambda i: (i, 0))
    col = lambda c: pl.BlockSpec((tm, D_MODEL), lambda i: (i, c))
    cc_col = P_SEGMENT // D_MODEL
    if prompt:
        prev_rows = tm // SUBLANES
        prev = lambda c: pl.BlockSpec(
            (SUBLANES, D_MODEL), lambda i: (jnp.maximum(i * prev_rows - 1, 0), c))
        pspecs = [prev(cc_col), prev(cc_col + 1)]
        uspec = pl.BlockSpec((1, SUBLANES, D_MODEL), lambda i: (i, 0, 0))
        ushape = jax.ShapeDtypeStruct((n_tiles, SUBLANES, D_MODEL), F32)
    else:
        pspecs = [tile, tile]
        uspec = tile
        ushape = jax.ShapeDtypeStruct((t, D_MODEL), F32)
    vec = lambda r: pl.BlockSpec((r, D_MODEL), lambda i: (0, 0))
    whole = pl.BlockSpec((D_MODEL, D_MODEL), lambda i: (0, 0),
                         pipeline_mode=pl.Buffered(1))
    return pl.pallas_call(
        functools.partial(_mix_kernel, prompt=prompt, tiles_per_seq=max(seq // tm, 1)),
        grid=(n_tiles,),
        in_specs=[tile, col(0), col(1), col(cc_col), col(cc_col + 1), col(2), col(3)]
                 + pspecs + [vec(1), vec(CONV_WIDTH), vec(1), tile, whole, vec(1), whole],
        out_specs=[tile, pl.BlockSpec((D_MODEL, tm), lambda i: (0, i)), tile, uspec],
        out_shape=[jax.ShapeDtypeStruct((t, D_MODEL), F32),
                   jax.ShapeDtypeStruct((D_MODEL, t), BF16),
                   jax.ShapeDtypeStruct((t, D_MODEL), F32),
                   ushape],
        scratch_shapes=[pltpu.VMEM((tm, D_MODEL), BF16)],
        compiler_params=_params(("arbitrary",)),
        name="mix_prompt" if prompt else "mix_sample",
    )(o, p16, p16, p32, p32, p16, p16, pa, pb, wts["gg"], wts["cw"], wts["cbias"],
      x, wts["wo"], wts["g2"], wts["wq"])


def _top_values(work, n):
    vals = []
    rank = jnp.full(work.shape, float(n), F32)
    for r in range(n):
        mx = jnp.max(work, axis=0, keepdims=True)
        hit = work == mx
        vals.append(mx)
        rank = jnp.where(hit, float(r), rank)
        work = jnp.where(hit, NEG_INF, work)
    return vals, rank


def _route_kernel(qp_ref, keys_ref, n1_ref, p1_ref, rank2_ref, p2_ref):
    for h in range(PEER_HEADS):
        st = []
        for side in range(2):
            c0 = (2 * h + side) * DQ_HALF
            st.append(lax.dot_general(keys_ref[side], qp_ref[:, c0:c0 + DQ_HALF],
                                      (((1,), (1,)), ((), ())),
                                      precision=lax.Precision.HIGHEST,
                                      preferred_element_type=F32))
        v1, rank1 = _top_values(st[0], PEER_TOPK)
        v2, rank2 = _top_values(st[1], PEER_TOPK)
        v2 = jnp.concatenate(v2, axis=0)
        tm = v2.shape[1]
        bidx = lax.broadcasted_iota(jnp.int32, (SUBLANES, tm), 0)
        cand = []
        for a in range(PEER_TOPK // 2):
            nb = PEER_TOPK // (a + 1)
            if nb >= SUBLANES:
                cand.append(v1[a] + v2[0:nb, :])
            else:
                cand.append(jnp.where(bidx < nb, v1[a] + v2[0:SUBLANES, :], NEG_INF))
        tail = jnp.concatenate(v1[PEER_TOPK // 2:], axis=0) + v2[0:1, :]
        cand.append(tail)
        thr = _top_values(jnp.concatenate(cand, axis=0), PEER_TOPK)[0][-1]
        m1 = v1[0]
        m2 = v2[0:1, :]
        z = jnp.zeros_like(m1)
        n1 = jnp.zeros_like(rank1)
        for a, ca in enumerate(cand):
            keep = ca >= thr
            z = z + jnp.sum(jnp.where(keep, jnp.exp(ca - (m1 + m2)), 0.0),
                            axis=0, keepdims=True)
            kept = jnp.where(keep, 1.0, 0.0)
            if a < PEER_TOPK // 2:
                n1 = jnp.where(rank1 == float(a), jnp.sum(kept, axis=0, keepdims=True), n1)
            else:
                for r in range(PEER_TOPK // 2):
                    n1 = jnp.where(rank1 == float(a + r), kept[r:r + 1, :], n1)
        n1_ref[h] = n1
        p1_ref[h] = jnp.exp(st[0] - m1)
        rank2_ref[h] = rank2.astype(BF16)
        p2_ref[h] = (0.5 * jnp.exp(st[1] - m2) / z).astype(BF16)


def _route(qp, keys, tm):
    t = qp.shape[0]
    big = pl.BlockSpec((PEER_HEADS, N_KEYS, tm), lambda i: (0, 0, i))
    f32s = jax.ShapeDtypeStruct((PEER_HEADS, N_KEYS, t), F32)
    bf16s = jax.ShapeDtypeStruct((PEER_HEADS, N_KEYS, t), BF16)
    return pl.pallas_call(
        _route_kernel,
        grid=(t // tm,),
        in_specs=[pl.BlockSpec((tm, D_MODEL), lambda i: (i, 0)),
                  pl.BlockSpec((2, N_KEYS, DQ_HALF), lambda i: (0, 0, 0))],
        out_specs=[big, big, big, big],
        out_shape=[f32s, f32s, bf16s, bf16s],
        compiler_params=_params(("arbitrary",)),
        name="route",
    )(qp, keys)


def _row_bf16(ref, h, e1, tm):
    row = jnp.broadcast_to(ref[h, pl.ds(e1, 1), :], (BF16_ROWS, tm)).astype(BF16)
    return jnp.concatenate([row] * (N_KEYS // BF16_ROWS), axis=0)


def _peer_kernel(h2t_ref, u_ref, v_ref, n1_ref, p1_ref, rank2_ref, p2_ref,
                 xm_ref, gf_ref, out_ref, *, te):
    j = pl.program_id(1)

    @pl.when(j == 0)
    def _():
        out_ref[...] = jnp.zeros_like(out_ref)

    tm = h2t_ref.shape[1]
    rows_per_part = PEER_PART_ROWS
    acc = None
    for part in range(te // rows_per_part):
        lo = part * rows_per_part
        a = jnp.dot(u_ref[lo:lo + rows_per_part, :], h2t_ref[...],
                    preferred_element_type=F32)
        blocks = []
        for l in range(rows_per_part // N_KEYS):
            e1 = j * (te // N_KEYS) + part * (rows_per_part // N_KEYS) + l
            g = jnp.zeros((N_KEYS, tm), BF16)
            for h in range(PEER_HEADS):
                keep = rank2_ref[h] < _row_bf16(n1_ref, h, e1, tm)
                g = g + jnp.where(keep, p2_ref[h], 0.0) * _row_bf16(p1_ref, h, e1, tm)
            al = a[l * N_KEYS:(l + 1) * N_KEYS, :]
            gelu2 = al * (1.0 + lax.erf(al * INV_SQRT2))
            blocks.append(gelu2.astype(BF16) * g)
        wt = jnp.concatenate(blocks, axis=0) if len(blocks) > 1 else blocks[0]
        d = jnp.dot(jnp.transpose(wt), v_ref[lo:lo + rows_per_part, :],
                    preferred_element_type=F32)
        acc = d if acc is None else acc + d
    out_ref[...] += acc

    @pl.when(j == pl.num_programs(1) - 1)
    def _():
        out_ref[...] = _rms(xm_ref[...] + out_ref[...], gf_ref[...])


def _peer(h2t, u_bf, v_bf, n1, p1, rank2, p2, xm, gf, tm, te):
    t = xm.shape[0]
    big = pl.BlockSpec((PEER_HEADS, N_KEYS, tm), lambda i, j: (0, 0, i))
    return pl.pallas_call(
        functools.partial(_peer_kernel, te=te),
        grid=(t // tm, N_EXPERTS // te),
        in_specs=[pl.BlockSpec((D_MODEL, tm), lambda i, j: (0, i)),
                  pl.BlockSpec((te, D_MODEL), lambda i, j: (j, 0)),
                  pl.BlockSpec((te, D_MODEL), lambda i, j: (j, 0)),
                  big, big, big, big,
                  pl.BlockSpec((tm, D_MODEL), lambda i, j: (i, 0)),
                  pl.BlockSpec((1, D_MODEL), lambda i, j: (0, 0))],
        out_specs=pl.BlockSpec((tm, D_MODEL), lambda i, j: (i, 0)),
        out_shape=jax.ShapeDtypeStruct((t, D_MODEL), F32),
        compiler_params=_params(("arbitrary", "arbitrary")),
        name="peer",
    )(h2t, u_bf, v_bf, n1, p1, rank2, p2, xm, gf)


def _channel_mixer(o, p32, p16, pa, pb, x, wts, tm_mix, tm_route, tm_peer, te, prompt, seq):
    xm, h2t, qp, utail = _mix(o, p32, p16, pa, pb, x, wts, tm_mix, prompt, seq)
    n1, p1, rank2, p2 = _route(qp, wts["keys"], tm_route)
    y = _peer(h2t, wts["u"], wts["v"], n1, p1, rank2, p2, xm, wts["gf"], tm_peer, te)
    return y, utail


def kernel(x_prompt, x_sample, state_gla, state_conv, norm1_g, w_in, w_alpha_up, b_alpha,
           gla_norm_g, conv_w, conv_b, w_out, norm2_g, w_query, sub_keys, expert_u, expert_v,
           norm_f_g):
    batch, seq, _ = x_prompt.shape
    n_dec = x_sample.shape[0]
    assert w_in.shape[0] == 1 and x_sample.shape[1] == 1

    alr0 = 2 * GLA_HEADS * GLA_DK + 2 * GLA_HEADS * GLA_DV
    w_lo = w_in[0][:, :alr0].astype(BF16)
    w_hi = w_in[0][:, alr0 + GLA_RANK:].astype(BF16)
    w_alr = jnp.pad(w_in[0][:, alr0:alr0 + GLA_RANK],
                    ((0, 0), (0, LANES - GLA_RANK))).astype(BF16)
    wup = jnp.pad(w_alpha_up[0], ((0, LANES - GLA_RANK), (0, 0))).astype(BF16)
    ba = b_alpha[0][None, :]
    g1 = norm1_g[0][None, :]
    wts = dict(gg=gla_norm_g[0][None, :], cw=conv_w[0], cbias=conv_b[0][None, :],
               g2=norm2_g[0][None, :], keys=sub_keys[0],
               gf=norm_f_g[None, :])

    xp = x_prompt.reshape(batch * seq, D_MODEL)
    p32, p16, alr, wts["u"], wts["v"], wts["wo"], wts["wq"] = _inproj(
        xp, g1, w_lo, w_hi, w_alr, tm=TM_INPROJ, tn=TN_INPROJ,
        tables=(expert_u[0], expert_v[0], w_out[0], w_query[0]))
    o, s_p = _gla_prompt(p32, alr, wup, ba, batch, seq, rows=GLA_ROWS)
    y_p, utail = _channel_mixer(o, p32, p16, p32, p32, xp, wts, tm_mix=TM_MIX,
                                tm_route=TM_ROUTE, tm_peer=TM_PEER, te=TE_PEER,
                                prompt=True, seq=seq)
    last = utail.reshape(batch, seq // TM_MIX, SUBLANES, D_MODEL)[:, -1, SUBLANES - 2:, :]

    xs = x_sample.reshape(n_dec, D_MODEL)
    ps32, ps16, alrs = _inproj(xs, g1, w_lo, w_hi, w_alr, tm=n_dec, tn=TN_INPROJ)
    os_, s_s = _gla_sample(ps32.reshape(n_dec, 1, N_MAIN // 2),
                           alrs.reshape(n_dec, 1, LANES), wup, ba, state_gla[0],
                           nb=GLA_SAMPLE_SEQS)
    y_s, us = _channel_mixer(os_.reshape(n_dec, D_MODEL), ps32, ps16, state_conv[0][:, 0, :],
                             state_conv[0][:, 1, :], xs, wts, tm_mix=n_dec, tm_route=n_dec,
                             tm_peer=n_dec, te=TE_PEER, prompt=False, seq=1)
    conv_s = jnp.stack([state_conv[0][:, 1, :], us], axis=1)

    return (y_p.reshape(batch, seq, D_MODEL),
            y_s.reshape(n_dec, 1, D_MODEL),
            s_p[None],
            last[None],
            s_s[None],
            conv_s[None])
```

```python
import functools
import math

import jax
import jax.numpy as jnp
from jax import lax
from jax.experimental import pallas as pl
from jax.experimental.pallas import tpu as pltpu

F32 = jnp.float32
BF16 = jnp.bfloat16

D_MODEL = 2048
GLA_HEADS = 4
GLA_DK = 256
GLA_DV = 512
GLA_RANK = 16
GLA_TAU = 16.0
GLA_CHUNK = 64
CONV_WIDTH = 3
PEER_HEADS = 8
N_KEYS = 128
N_EXPERTS = N_KEYS * N_KEYS
PEER_TOPK = 16
DQ_HALF = 128
EPS = 1e-6

LANES = 128
SUBLANES = 8
BF16_ROWS = 16
VMEM_LIMIT_BYTES = 60000 * 1024

N_MAIN = 16384
P_SEGMENT = 4096
GLA_SUB = 8
GLA_HEADS_PER_STEP = 4
PEER_PART_ROWS = 512

TM_INPROJ = 1024
TN_INPROJ = 1024
GLA_ROWS = 256
GLA_SAMPLE_SEQS = 4
TM_MIX = 256
TM_ROUTE = 256
TM_PEER = 512
TE_PEER = 1024
NEG_INF = float("-inf")
INV_SQRT2 = 1.0 / math.sqrt(2.0)


def _params(semantics):
    return pltpu.CompilerParams(dimension_semantics=semantics,
                                vmem_limit_bytes=VMEM_LIMIT_BYTES)


def _rms(x, g):
    return x * lax.rsqrt(jnp.mean(x * x, axis=-1, keepdims=True) + EPS) * g


def _inproj_kernel(*refs, n_lo, tiles_per_seg, n_tables):
    x_ref, g_ref, wlo_ref, whi_ref, walr_ref = refs[:5]
    tables_in = refs[5:5 + n_tables]
    p32_ref, p16_ref, alr_ref = refs[5 + n_tables:8 + n_tables]
    tables_out = refs[8 + n_tables:8 + 2 * n_tables]
    h_ref = refs[-1]
    j = pl.program_id(1)

    @pl.when(j == 0)
    def _():
        hb = _rms(x_ref[...], g_ref[...]).astype(BF16)
        h_ref[...] = hb
        alr_ref[...] = jnp.dot(hb, walr_ref[...], preferred_element_type=F32)

    wide = (j // tiles_per_seg) % 2 == 0
    low = j < n_lo
    for w_ref, from_low in ((wlo_ref, True), (whi_ref, False)):
        for out_ref, is_wide in ((p32_ref, True), (p16_ref, False)):
            @pl.when((low == from_low) & (wide == is_wide))
            def _():
                out_ref[...] = jnp.dot(h_ref[...], w_ref[...],
                                       preferred_element_type=F32).astype(out_ref.dtype)

    for src_ref, dst_ref in zip(tables_in, tables_out):
        dst_ref[...] = src_ref[...].astype(BF16)


def _inproj(x, g, w_lo, w_hi, w_alr, tm, tn, tables=()):
    t = x.shape[0]
    n_lo = w_lo.shape[1] // tn
    per = P_SEGMENT // tn
    nj = N_MAIN // tn
    steps = (t // tm) * nj

    def idx32(j):
        seg, r = j // per, j % per
        return jnp.where(seg % 2 == 0, (seg // 2) * per + r, (seg // 2 + 1) * per - 1)

    def idx16(j):
        seg, r = j // per, j % per
        return jnp.where(seg % 2 == 1, (seg // 2) * per + r,
                         jnp.maximum((seg // 2) * per - 1, 0))

    slabs = [pl.BlockSpec((tb.shape[0] // steps, D_MODEL), lambda i, j: (i * nj + j, 0))
             for tb in tables]
    return pl.pallas_call(
        functools.partial(_inproj_kernel, n_lo=n_lo, tiles_per_seg=per,
                          n_tables=len(tables)),
        grid=(t // tm, nj),
        in_specs=[
            pl.BlockSpec((tm, D_MODEL), lambda i, j: (i, 0), pipeline_mode=pl.Buffered(1)),
            pl.BlockSpec((1, D_MODEL), lambda i, j: (0, 0)),
            pl.BlockSpec((D_MODEL, tn), lambda i, j: (0, jnp.minimum(j, n_lo - 1))),
            pl.BlockSpec((D_MODEL, tn), lambda i, j: (0, jnp.maximum(j - n_lo, 0))),
            pl.BlockSpec((D_MODEL, LANES), lambda i, j: (0, 0)),
        ] + slabs,
        out_specs=[
            pl.BlockSpec((tm, tn), lambda i, j: (i, idx32(j))),
            pl.BlockSpec((tm, tn), lambda i, j: (i, idx16(j))),
            pl.BlockSpec((tm, LANES), lambda i, j: (i, 0)),
        ] + slabs,
        out_shape=[
            jax.ShapeDtypeStruct((t, N_MAIN // 2), F32),
            jax.ShapeDtypeStruct((t, N_MAIN // 2), BF16),
            jax.ShapeDtypeStruct((t, LANES), F32),
        ] + [jax.ShapeDtypeStruct(tb.shape, BF16) for tb in tables],
        scratch_shapes=[pltpu.VMEM((tm, D_MODEL), BF16)],
        compiler_params=_params(("arbitrary", "arbitrary")),
        name="inproj",
    )(x, g, w_lo, w_hi, w_alr, *tables)


def _log_decay(alr, wup, ba):
    xa = jnp.dot(alr.astype(BF16), wup, preferred_element_type=F32) + ba
    return -(jnp.maximum(-xa, 0.0) + jnp.log1p(jnp.exp(-jnp.abs(xa)))) * (1.0 / GLA_TAU)


def _col_scale(row, width):
    col = jnp.transpose(jnp.broadcast_to(row, (LANES, row.shape[1])))
    return jnp.concatenate([col] * (width // LANES), axis=1)


def _gla_prompt_kernel(q_ref, k_ref, v_ref, alr_ref, wup_ref, ba_ref,
                       o_ref, sout_ref, s_ref, b_ref, *, rows):
    c = pl.program_id(2)

    @pl.when(c == 0)
    def _():
        s_ref[...] = jnp.zeros_like(s_ref)

    g = _log_decay(alr_ref[...], wup_ref[...], ba_ref[...])
    rmod = lax.broadcasted_iota(jnp.int32, g.shape, 0) & (GLA_CHUNK - 1)
    step = 1
    while step < GLA_CHUNK:
        g = g + jnp.where(rmod >= step, pltpu.roll(g, step, 0), 0.0)
        step *= 2
    b_ref[...] = g

    row = lax.broadcasted_iota(jnp.int32, (GLA_CHUNK, GLA_DK), 0)
    rsub = row & (GLA_SUB - 1)
    ri = lax.broadcasted_iota(jnp.int32, (GLA_CHUNK, GLA_CHUNK), 0)
    cj = lax.broadcasted_iota(jnp.int32, (GLA_CHUNK, GLA_CHUNK), 1)
    n_sub = GLA_CHUNK // GLA_SUB

    def head_chunk(r0, hh):
        dk = slice(hh * GLA_DK, (hh + 1) * GLA_DK)
        dv = slice(hh * GLA_DV, (hh + 1) * GLA_DV)
        qc = q_ref[pl.ds(r0, GLA_CHUNK), dk] * (GLA_DK ** -0.5)
        kc = k_ref[pl.ds(r0, GLA_CHUNK), dk]
        vc = v_ref[pl.ds(r0, GLA_CHUNK), dv]
        b = b_ref[pl.ds(r0, GLA_CHUNK), dk]
        bl = b[GLA_CHUNK - 1:GLA_CHUNK, :]
        s = s_ref[hh]
        vb = vc.astype(BF16)

        o = jnp.dot((qc * jnp.exp(b)).astype(BF16), s.astype(BF16),
                    preferred_element_type=F32)

        parts = [jnp.zeros((GLA_SUB, GLA_CHUNK), F32)]
        for blk in range(1, n_sub):
            lo = blk * GLA_SUB
            ref = b[lo - 1:lo, :]
            qs = qc[lo:lo + GLA_SUB, :] * jnp.exp(b[lo:lo + GLA_SUB, :] - ref)
            ks = kc * jnp.exp(jnp.where(row < lo, ref - b, NEG_INF))
            parts.append(lax.dot_general(qs.astype(BF16), ks.astype(BF16),
                                         (((1,), (1,)), ((), ())),
                                         preferred_element_type=F32))
        attn = jnp.concatenate(parts, axis=0)

        for d in range(GLA_SUB):
            if d == 0:
                t = qc * kc
            else:
                kd = pltpu.roll(kc, d, 0)
                bd = pltpu.roll(b, d, 0)
                t = qc * kd * jnp.exp(jnp.where(rsub >= d, b - bd, NEG_INF))
            cs = jnp.sum(t, axis=1, keepdims=True)
            attn = attn + jnp.where(ri - cj == d, cs, 0.0)

        o = o + jnp.dot(attn.astype(BF16), vb, preferred_element_type=F32)
        o_ref[pl.ds(r0, GLA_CHUNK), dv] = o

        kdec = (kc * jnp.exp(bl - b)).astype(BF16)
        s_ref[hh] = (_col_scale(jnp.exp(bl), GLA_DV) * s
                     + lax.dot_general(kdec, vb, (((0,), (0,)), ((), ())),
                                       preferred_element_type=F32))

    def chunk(ci, carry):
        r0 = pl.multiple_of(ci * GLA_CHUNK, GLA_CHUNK)
        for hh in range(GLA_HEADS_PER_STEP):
            head_chunk(r0, hh)
        return carry

    lax.fori_loop(0, rows // GLA_CHUNK, chunk, 0)

    @pl.when(c == pl.num_programs(2) - 1)
    def _():
        sout_ref[0] = s_ref[...]


def _gla_prompt(p, alr, wup, ba, batch, seq, rows):
    nsteps = seq // rows
    hps = GLA_HEADS_PER_STEP
    wk = hps * GLA_DK
    wv = hps * GLA_DV
    kq = GLA_HEADS * GLA_DK // wk
    kv = 2 * GLA_HEADS * GLA_DK // wv

    def rowmap(b, h, c):
        return b * nsteps + c

    return pl.pallas_call(
        functools.partial(_gla_prompt_kernel, rows=rows),
        grid=(batch, GLA_HEADS // hps, nsteps),
        in_specs=[
            pl.BlockSpec((rows, wk), lambda b, h, c: (rowmap(b, h, c), h)),
            pl.BlockSpec((rows, wk), lambda b, h, c: (rowmap(b, h, c), kq + h)),
            pl.BlockSpec((rows, wv), lambda b, h, c: (rowmap(b, h, c), kv + h)),
            pl.BlockSpec((rows, LANES), lambda b, h, c: (rowmap(b, h, c), 0)),
            pl.BlockSpec((LANES, wk), lambda b, h, c: (0, h)),
            pl.BlockSpec((1, wk), lambda b, h, c: (0, h)),
        ],
        out_specs=[
            pl.BlockSpec((rows, wv), lambda b, h, c: (rowmap(b, h, c), h)),
            pl.BlockSpec((1, hps, GLA_DK, GLA_DV), lambda b, h, c: (b, h, 0, 0)),
        ],
        out_shape=[
            jax.ShapeDtypeStruct((batch * seq, GLA_HEADS * GLA_DV), F32),
            jax.ShapeDtypeStruct((batch, GLA_HEADS, GLA_DK, GLA_DV), F32),
        ],
        scratch_shapes=[pltpu.VMEM((hps, GLA_DK, GLA_DV), F32),
                        pltpu.VMEM((rows, wk), F32)],
        compiler_params=_params(("arbitrary", "arbitrary", "arbitrary")),
        name="gla_prompt",
    )(p, p, p, alr, wup, ba)


def _gla_sample_kernel(q_ref, k_ref, v_ref, alr_ref, wup_ref, ba_ref, s0_ref,
                       o_ref, s1_ref):
    first = lax.broadcasted_iota(jnp.int32, (SUBLANES, GLA_DK), 0) == 0
    for bb in range(q_ref.shape[0]):
        g_all = _log_decay(jnp.broadcast_to(alr_ref[bb], (SUBLANES, LANES)),
                           wup_ref[...], ba_ref[...])
        for h in range(GLA_HEADS):
            dk = slice(h * GLA_DK, (h + 1) * GLA_DK)
            dv = slice(h * GLA_DV, (h + 1) * GLA_DV)
            q = jnp.broadcast_to(q_ref[bb, :, dk], (SUBLANES, GLA_DK)) * (GLA_DK ** -0.5)
            k = jnp.broadcast_to(k_ref[bb, :, dk], (SUBLANES, GLA_DK))
            v = jnp.broadcast_to(v_ref[bb, :, dv], (SUBLANES, GLA_DV))
            g = g_all[:, dk]
            s = s0_ref[bb, h]
            attn = jnp.sum(q * k, axis=1, keepdims=True)
            o = attn * v + jnp.dot((q * jnp.exp(g)).astype(BF16), s.astype(BF16),
                                   preferred_element_type=F32)
            o_ref[bb, :, dv] = o[0:1, :]
            k1 = jnp.where(first, k, 0.0).astype(BF16)
            s1_ref[bb, h] = (_col_scale(jnp.exp(g[0:1, :]), GLA_DV) * s
                             + lax.dot_general(k1, v.astype(BF16), (((0,), (0,)), ((), ())),
                                               preferred_element_type=F32))


def _gla_sample(p3, alr3, wup, ba, s0, nb):
    n = p3.shape[0]
    qk = GLA_HEADS * GLA_DK
    vw = GLA_HEADS * GLA_DV
    return pl.pallas_call(
        _gla_sample_kernel,
        grid=(n // nb,),
        in_specs=[
            pl.BlockSpec((nb, 1, qk), lambda b: (b, 0, 0)),
            pl.BlockSpec((nb, 1, qk), lambda b: (b, 0, 1)),
            pl.BlockSpec((nb, 1, vw), lambda b: (b, 0, 2 * qk // vw)),
            pl.BlockSpec((nb, 1, LANES), lambda b: (b, 0, 0)),
            pl.BlockSpec((LANES, qk), lambda b: (0, 0)),
            pl.BlockSpec((1, qk), lambda b: (0, 0)),
            pl.BlockSpec((nb, GLA_HEADS, GLA_DK, GLA_DV), lambda b: (b, 0, 0, 0)),
        ],
        out_specs=[
            pl.BlockSpec((nb, 1, vw), lambda b: (b, 0, 0)),
            pl.BlockSpec((nb, GLA_HEADS, GLA_DK, GLA_DV), lambda b: (b, 0, 0, 0)),
        ],
        out_shape=[
            jax.ShapeDtypeStruct((n, 1, vw), F32),
            jax.ShapeDtypeStruct((n, GLA_HEADS, GLA_DK, GLA_DV), F32),
        ],
        compiler_params=_params(("arbitrary",)),
        name="gla_sample",
    )(p3, p3, p3, alr3, wup, ba, s0)


def _mix_kernel(o_ref, r_ref, cb_ref, cc_ref, cx_ref, ga_ref, gb_ref, pa_ref, pb_ref,
                gg_ref, cw_ref, cbias_ref, x_ref, wo_ref, g2_ref, wq_ref,
                xm_ref, h2t_ref, qp_ref, u_ref, m_ref, *, prompt, tiles_per_seq):
    tm = x_ref.shape[0]
    if prompt:
        start = (pl.program_id(0) % tiles_per_seq) == 0
        row = lax.broadcasted_iota(jnp.int32, (tm, GLA_DV), 0)
    for h in range(GLA_HEADS):
        dv = slice(h * GLA_DV, (h + 1) * GLA_DV)
        u = cc_ref[:, dv] * cx_ref[:, dv]
        if prompt:
            prev = jnp.where(start, 0.0, pa_ref[:, dv] * pb_ref[:, dv])
            um1 = jnp.where(row == 0, prev[7:8, :], pltpu.roll(u, 1, 0))
            um2 = jnp.where(row == 0, prev[6:7, :],
                            jnp.where(row == 1, prev[7:8, :], pltpu.roll(u, 2, 0)))
            u_ref[0, :, dv] = u[tm - SUBLANES:, :]
        else:
            um2 = pa_ref[:, dv]
            um1 = pb_ref[:, dv]
            u_ref[:, dv] = u
        z = (cbias_ref[:, dv] + um2 * cw_ref[0:1, dv] + um1 * cw_ref[1:2, dv]
             + u * cw_ref[2:3, dv])
        y_b = cb_ref[:, dv].astype(F32) * z
        r = r_ref[:, dv].astype(F32)
        on = _rms(o_ref[:, dv], gg_ref[:, dv])
        m = (jax.nn.sigmoid(gb_ref[:, dv].astype(F32)) * y_b
             + jax.nn.sigmoid(ga_ref[:, dv].astype(F32)) * (r * jax.nn.sigmoid(r) * on))
        m_ref[:, dv] = m.astype(BF16)
    xm = x_ref[...] + jnp.dot(m_ref[...], wo_ref[...], preferred_element_type=F32)
    xm_ref[...] = xm
    h2 = _rms(xm, g2_ref[...])
    h2t_ref[...] = jnp.transpose(h2).astype(BF16)
    qp_ref[...] = jnp.dot(h2.astype(BF16), wq_ref[...], preferred_element_type=F32)


def _mix(o, p32, p16, pa, pb, x, wts, tm, prompt, seq):
    t = o.shape[0]
    n_tiles = t // tm
    tile = pl.BlockSpec((tm, D_MODEL), lambda i: (i, 0))
    col = lambda c: pl.BlockSpec((tm, D_MODEL), lambda i: (i, c))
    cc_col = P_SEGMENT // D_MODEL
    if prompt:
        prev_rows = tm // SUBLANES
        prev = lambda c: pl.BlockSpec(
            (SUBLANES, D_MODEL), lambda i: (jnp.maximum(i * prev_rows - 1, 0), c))
        pspecs = [prev(cc_col), prev(cc_col + 1)]
        uspec = pl.BlockSpec((1, SUBLANES, D_MODEL), lambda i: (i, 0, 0))
        ushape = jax.ShapeDtypeStruct((n_tiles, SUBLANES, D_MODEL), F32)
    else:
        pspecs = [tile, tile]
        uspec = tile
        ushape = jax.ShapeDtypeStruct((t, D_MODEL), F32)
    vec = lambda r: pl.BlockSpec((r, D_MODEL), lambda i: (0, 0))
    whole = pl.BlockSpec((D_MODEL, D_MODEL), lambda i: (0, 0),
                         pipeline_mode=pl.Buffered(1))
    return pl.pallas_call(
        functools.partial(_mix_kernel, prompt=prompt, tiles_per_seq=max(seq // tm, 1)),
        grid=(n_tiles,),
        in_specs=[tile, col(0), col(1), col(cc_col), col(cc_col + 1), col(2), col(3)]
                 + pspecs + [vec(1), vec(CONV_WIDTH), vec(1), tile, whole, vec(1), whole],
        out_specs=[tile, pl.BlockSpec((D_MODEL, tm), lambda i: (0, i)), tile, uspec],
        out_shape=[jax.ShapeDtypeStruct((t, D_MODEL), F32),
                   jax.ShapeDtypeStruct((D_MODEL, t), BF16),
                   jax.ShapeDtypeStruct((t, D_MODEL), F32),
                   ushape],
        scratch_shapes=[pltpu.VMEM((tm, D_MODEL), BF16)],
        compiler_params=_params(("arbitrary",)),
        name="mix_prompt" if prompt else "mix_sample",
    )(o, p16, p16, p32, p32, p16, p16, pa, pb, wts["gg"], wts["cw"], wts["cbias"],
      x, wts["wo"], wts["g2"], wts["wq"])


def _top_values(work, n, with_rank=False):
    vals = []
    rank = jnp.full(work.shape, float(n), F32) if with_rank else None
    for r in range(n):
        mx = jnp.max(work, axis=0, keepdims=True)
        hit = work == mx
        vals.append(mx)
        if with_rank:
            rank = jnp.where(hit, float(r), rank)
        work = jnp.where(hit, NEG_INF, work)
    return vals, rank


def _route_kernel(qp_ref, keys_ref, n1_ref, p1_ref, rank2_ref, p2_ref):
    for h in range(PEER_HEADS):
        st = []
        for side in range(2):
            c0 = (2 * h + side) * DQ_HALF
            st.append(lax.dot_general(keys_ref[side], qp_ref[:, c0:c0 + DQ_HALF],
                                      (((1,), (1,)), ((), ())),
                                      precision=lax.Precision.HIGHEST,
                                      preferred_element_type=F32))
        v1, _ = _top_values(st[0], PEER_TOPK)
        v2, rank2 = _top_values(st[1], PEER_TOPK, with_rank=True)
        v2 = jnp.concatenate(v2, axis=0)
        tm = v2.shape[1]
        bidx = lax.broadcasted_iota(jnp.int32, (SUBLANES, tm), 0)
        cand = []
        for a in range(PEER_TOPK // 2):
            nb = PEER_TOPK // (a + 1)
            if nb >= SUBLANES:
                cand.append(v1[a] + v2[0:nb, :])
            else:
                cand.append(jnp.where(bidx < nb, v1[a] + v2[0:SUBLANES, :], NEG_INF))
        tail = jnp.concatenate(v1[PEER_TOPK // 2:], axis=0) + v2[0:1, :]
        cand.append(tail)
        thr = _top_values(jnp.concatenate(cand, axis=0), PEER_TOPK)[0][-1]
        m1 = v1[0]
        m2 = v2[0:1, :]
        z = jnp.zeros_like(m1)
        n1 = jnp.zeros_like(st[0])
        for a, ca in enumerate(cand):
            keep = ca >= thr
            z = z + jnp.sum(jnp.where(keep, jnp.exp(ca - (m1 + m2)), 0.0),
                            axis=0, keepdims=True)
            kept = jnp.where(keep, 1.0, 0.0)
            if a < PEER_TOPK // 2:
                n1 = jnp.where(st[0] == v1[a], jnp.sum(kept, axis=0, keepdims=True), n1)
            else:
                for r in range(PEER_TOPK // 2):
                    n1 = jnp.where(st[0] == v1[a + r], kept[r:r + 1, :], n1)
        n1_ref[h] = n1
        p1_ref[h] = jnp.exp(st[0] - m1)
        rank2_ref[h] = rank2.astype(BF16)
        p2_ref[h] = (0.5 * jnp.exp(st[1] - m2) / z).astype(BF16)


def _route(qp, keys, tm):
    t = qp.shape[0]
    big = pl.BlockSpec((PEER_HEADS, N_KEYS, tm), lambda i: (0, 0, i))
    f32s = jax.ShapeDtypeStruct((PEER_HEADS, N_KEYS, t), F32)
    bf16s = jax.ShapeDtypeStruct((PEER_HEADS, N_KEYS, t), BF16)
    return pl.pallas_call(
        _route_kernel,
        grid=(t // tm,),
        in_specs=[pl.BlockSpec((tm, D_MODEL), lambda i: (i, 0)),
                  pl.BlockSpec((2, N_KEYS, DQ_HALF), lambda i: (0, 0, 0))],
        out_specs=[big, big, big, big],
        out_shape=[f32s, f32s, bf16s, bf16s],
        compiler_params=_params(("arbitrary",)),
        name="route",
    )(qp, keys)


def _row_bf16(ref, h, e1, tm):
    row = jnp.broadcast_to(ref[h, pl.ds(e1, 1), :], (BF16_ROWS, tm)).astype(BF16)
    return jnp.concatenate([row] * (N_KEYS // BF16_ROWS), axis=0)


def _peer_kernel(h2t_ref, u_ref, v_ref, n1_ref, p1_ref, rank2_ref, p2_ref,
                 xm_ref, gf_ref, out_ref, *, te):
    j = pl.program_id(1)

    @pl.when(j == 0)
    def _():
        out_ref[...] = jnp.zeros_like(out_ref)

    tm = h2t_ref.shape[1]
    rows_per_part = PEER_PART_ROWS
    acc = None
    for part in range(te // rows_per_part):
        lo = part * rows_per_part
        a = jnp.dot(u_ref[lo:lo + rows_per_part, :], h2t_ref[...],
                    preferred_element_type=F32)
        blocks = []
        for l in range(rows_per_part // N_KEYS):
            e1 = j * (te // N_KEYS) + part * (rows_per_part // N_KEYS) + l
            g = jnp.zeros((N_KEYS, tm), BF16)
            for h in range(PEER_HEADS):
                keep = rank2_ref[h] < _row_bf16(n1_ref, h, e1, tm)
                g = g + jnp.where(keep, p2_ref[h], 0.0) * _row_bf16(p1_ref, h, e1, tm)
            al = a[l * N_KEYS:(l + 1) * N_KEYS, :]
            gelu2 = al * (1.0 + lax.erf(al * INV_SQRT2))
            blocks.append(gelu2.astype(BF16) * g)
        wt = jnp.concatenate(blocks, axis=0) if len(blocks) > 1 else blocks[0]
        d = jnp.dot(jnp.transpose(wt), v_ref[lo:lo + rows_per_part, :],
                    preferred_element_type=F32)
        acc = d if acc is None else acc + d
    out_ref[...] += acc

    @pl.when(j == pl.num_programs(1) - 1)
    def _():
        out_ref[...] = _rms(xm_ref[...] + out_ref[...], gf_ref[...])


def _peer(h2t, u_bf, v_bf, n1, p1, rank2, p2, xm, gf, tm, te):
    t = xm.shape[0]
    big = pl.BlockSpec((PEER_HEADS, N_KEYS, tm), lambda i, j: (0, 0, i))
    return pl.pallas_call(
        functools.partial(_peer_kernel, te=te),
        grid=(t // tm, N_EXPERTS // te),
        in_specs=[pl.BlockSpec((D_MODEL, tm), lambda i, j: (0, i)),
                  pl.BlockSpec((te, D_MODEL), lambda i, j: (j, 0)),
                  pl.BlockSpec((te, D_MODEL), lambda i, j: (j, 0)),
                  big, big, big, big,
                  pl.BlockSpec((tm, D_MODEL), lambda i, j: (i, 0)),
                  pl.BlockSpec((1, D_MODEL), lambda i, j: (0, 0))],
        out_specs=pl.BlockSpec((tm, D_MODEL), lambda i, j: (i, 0)),
        out_shape=jax.ShapeDtypeStruct((t, D_MODEL), F32),
        compiler_params=_params(("arbitrary", "arbitrary")),
        name="peer",
    )(h2t, u_bf, v_bf, n1, p1, rank2, p2, xm, gf)


def _channel_mixer(o, p32, p16, pa, pb, x, wts, tm_mix, tm_route, tm_peer, te, prompt, seq):
    xm, h2t, qp, utail = _mix(o, p32, p16, pa, pb, x, wts, tm_mix, prompt, seq)
    n1, p1, rank2, p2 = _route(qp, wts["keys"], tm_route)
    y = _peer(h2t, wts["u"], wts["v"], n1, p1, rank2, p2, xm, wts["gf"], tm_peer, te)
    return y, utail


def kernel(x_prompt, x_sample, state_gla, state_conv, norm1_g, w_in, w_alpha_up, b_alpha,
           gla_norm_g, conv_w, conv_b, w_out, norm2_g, w_query, sub_keys, expert_u, expert_v,
           norm_f_g):
    batch, seq, _ = x_prompt.shape
    n_dec = x_sample.shape[0]
    assert w_in.shape[0] == 1 and x_sample.shape[1] == 1

    alr0 = 2 * GLA_HEADS * GLA_DK + 2 * GLA_HEADS * GLA_DV
    w_lo = w_in[0][:, :alr0].astype(BF16)
    w_hi = w_in[0][:, alr0 + GLA_RANK:].astype(BF16)
    w_alr = jnp.pad(w_in[0][:, alr0:alr0 + GLA_RANK],
                    ((0, 0), (0, LANES - GLA_RANK))).astype(BF16)
    wup = jnp.pad(w_alpha_up[0], ((0, LANES - GLA_RANK), (0, 0))).astype(BF16)
    ba = b_alpha[0][None, :]
    g1 = norm1_g[0][None, :]
    wts = dict(gg=gla_norm_g[0][None, :], cw=conv_w[0], cbias=conv_b[0][None, :],
               g2=norm2_g[0][None, :], keys=sub_keys[0],
               gf=norm_f_g[None, :])

    xp = x_prompt.reshape(batch * seq, D_MODEL)
    p32, p16, alr, wts["u"], wts["v"], wts["wo"], wts["wq"] = _inproj(
        xp, g1, w_lo, w_hi, w_alr, tm=TM_INPROJ, tn=TN_INPROJ,
        tables=(expert_u[0], expert_v[0], w_out[0], w_query[0]))
    o, s_p = _gla_prompt(p32, alr, wup, ba, batch, seq, rows=GLA_ROWS)
    y_p, utail = _channel_mixer(o, p32, p16, p32, p32, xp, wts, tm_mix=TM_MIX,
                                tm_route=TM_ROUTE, tm_peer=TM_PEER, te=TE_PEER,
                                prompt=True, seq=seq)
    last = utail.reshape(batch, seq // TM_MIX, SUBLANES, D_MODEL)[:, -1, SUBLANES - 2:, :]

    xs = x_sample.reshape(n_dec, D_MODEL)
    ps32, ps16, alrs = _inproj(xs, g1, w_lo, w_hi, w_alr, tm=n_dec, tn=TN_INPROJ)
    os_, s_s = _gla_sample(ps32.reshape(n_dec, 1, N_MAIN // 2),
                           alrs.reshape(n_dec, 1, LANES), wup, ba, state_gla[0],
                           nb=GLA_SAMPLE_SEQS)
    y_s, us = _channel_mixer(os_.reshape(n_dec, D_MODEL), ps32, ps16, state_conv[0][:, 0, :],
                             state_conv[0][:, 1, :], xs, wts, tm_mix=n_dec, tm_route=n_dec,
                             tm_peer=n_dec, te=TE_PEER, prompt=False, seq=1)
    conv_s = jnp.stack([state_conv[0][:, 1, :], us], axis=1)

    return (y_p.reshape(batch, seq, D_MODEL),
            y_s.reshape(n_dec, 1, D_MODEL),
            s_p[None],
            last[None],
            s_s[None],
            conv_s[None])
```

```python
import functools
import math

import jax
import jax.numpy as jnp
from jax import lax
from jax.experimental import pallas as pl
from jax.experimental.pallas import tpu as pltpu

F32 = jnp.float32
BF16 = jnp.bfloat16

D_MODEL = 2048
GLA_HEADS = 4
GLA_DK = 256
GLA_DV = 512
GLA_RANK = 16
GLA_TAU = 16.0
GLA_CHUNK = 64
CONV_WIDTH = 3
PEER_HEADS = 8
N_KEYS = 128
N_EXPERTS = N_KEYS * N_KEYS
PEER_TOPK = 16
DQ_HALF = 128
EPS = 1e-6

LANES = 128
SUBLANES = 8
BF16_ROWS = 16
VMEM_LIMIT_BYTES = 60000 * 1024

N_MAIN = 16384
P_SEGMENT = 4096
GLA_SUB = 8
GLA_HEADS_PER_STEP = 4
PEER_PART_ROWS = 512

TM_INPROJ = 1024
TN_INPROJ = 1024
GLA_ROWS = 256
GLA_SAMPLE_SEQS = 4
TM_MIX = 256
TM_ROUTE = 256
TM_PEER = 512
TE_PEER = 1024
NEG_INF = float("-inf")
INV_SQRT2 = 1.0 / math.sqrt(2.0)


def _params(semantics):
    return pltpu.CompilerParams(dimension_semantics=semantics,
                                vmem_limit_bytes=VMEM_LIMIT_BYTES)


def _rms(x, g):
    return x * lax.rsqrt(jnp.mean(x * x, axis=-1, keepdims=True) + EPS) * g


def _inproj_kernel(*refs, n_lo, tiles_per_seg, n_tables):
    x_ref, g_ref, wlo_ref, whi_ref, walr_ref = refs[:5]
    tables_in = refs[5:5 + n_tables]
    p32_ref, p16_ref, alr_ref = refs[5 + n_tables:8 + n_tables]
    tables_out = refs[8 + n_tables:8 + 2 * n_tables]
    h_ref = refs[-1]
    j = pl.program_id(1)

    @pl.when(j == 0)
    def _():
        hb = _rms(x_ref[...], g_ref[...]).astype(BF16)
        h_ref[...] = hb
        alr_ref[...] = jnp.dot(hb, walr_ref[...], preferred_element_type=F32)

    wide = (j // tiles_per_seg) % 2 == 0
    low = j < n_lo
    for w_ref, from_low in ((wlo_ref, True), (whi_ref, False)):
        for out_ref, is_wide in ((p32_ref, True), (p16_ref, False)):
            @pl.when((low == from_low) & (wide == is_wide))
            def _():
                out_ref[...] = jnp.dot(h_ref[...], w_ref[...],
                                       preferred_element_type=F32).astype(out_ref.dtype)

    for src_ref, dst_ref in zip(tables_in, tables_out):
        dst_ref[...] = src_ref[...].astype(BF16)


def _inproj(x, g, w_lo, w_hi, w_alr, tm, tn, tables=()):
    t = x.shape[0]
    n_lo = (N_MAIN - w_hi.shape[1]) // tn
    per = P_SEGMENT // tn
    nj = N_MAIN // tn
    steps = (t // tm) * nj

    def idx32(j):
        seg, r = j // per, j % per
        return jnp.where(seg % 2 == 0, (seg // 2) * per + r, (seg // 2 + 1) * per - 1)

    def idx16(j):
        seg, r = j // per, j % per
        return jnp.where(seg % 2 == 1, (seg // 2) * per + r,
                         jnp.maximum((seg // 2) * per - 1, 0))

    slabs = [pl.BlockSpec((tb.shape[0] // steps, D_MODEL), lambda i, j: (i * nj + j, 0))
             for tb in tables]
    return pl.pallas_call(
        functools.partial(_inproj_kernel, n_lo=n_lo, tiles_per_seg=per,
                          n_tables=len(tables)),
        grid=(t // tm, nj),
        in_specs=[
            pl.BlockSpec((tm, D_MODEL), lambda i, j: (i, 0), pipeline_mode=pl.Buffered(1)),
            pl.BlockSpec((1, D_MODEL), lambda i, j: (0, 0)),
            pl.BlockSpec((D_MODEL, tn), lambda i, j: (0, jnp.minimum(j, n_lo - 1))),
            pl.BlockSpec((D_MODEL, tn), lambda i, j: (0, jnp.maximum(j - n_lo, 0))),
            pl.BlockSpec((D_MODEL, LANES), lambda i, j: (0, 0)),
        ] + slabs,
        out_specs=[
            pl.BlockSpec((tm, tn), lambda i, j: (i, idx32(j))),
            pl.BlockSpec((tm, tn), lambda i, j: (i, idx16(j))),
            pl.BlockSpec((tm, LANES), lambda i, j: (i, 0)),
        ] + slabs,
        out_shape=[
            jax.ShapeDtypeStruct((t, N_MAIN // 2), F32),
            jax.ShapeDtypeStruct((t, N_MAIN // 2), BF16),
            jax.ShapeDtypeStruct((t, LANES), F32),
        ] + [jax.ShapeDtypeStruct(tb.shape, BF16) for tb in tables],
        scratch_shapes=[pltpu.VMEM((tm, D_MODEL), BF16)],
        compiler_params=_params(("arbitrary", "arbitrary")),
        name="inproj",
    )(x, g, w_lo, w_hi, w_alr, *tables)


def _log_decay(alr, wup, ba):
    xa = jnp.dot(alr.astype(BF16), wup, preferred_element_type=F32) + ba
    return -(jnp.maximum(-xa, 0.0) + jnp.log1p(jnp.exp(-jnp.abs(xa)))) * (1.0 / GLA_TAU)


def _col_scale(row, width):
    col = jnp.transpose(jnp.broadcast_to(row, (LANES, row.shape[1])))
    return jnp.concatenate([col] * (width // LANES), axis=1)


def _gla_prompt_kernel(q_ref, k_ref, v_ref, alr_ref, wup_ref, ba_ref,
                       o_ref, sout_ref, s_ref, b_ref, *, rows):
    c = pl.program_id(2)

    @pl.when(c == 0)
    def _():
        s_ref[...] = jnp.zeros_like(s_ref)

    g = _log_decay(alr_ref[...], wup_ref[...], ba_ref[...])
    rmod = lax.broadcasted_iota(jnp.int32, g.shape, 0) & (GLA_CHUNK - 1)
    step = 1
    while step < GLA_CHUNK:
        g = g + jnp.where(rmod >= step, pltpu.roll(g, step, 0), 0.0)
        step *= 2
    b_ref[...] = g

    row = lax.broadcasted_iota(jnp.int32, (GLA_CHUNK, GLA_DK), 0)
    rsub = row & (GLA_SUB - 1)
    ri = lax.broadcasted_iota(jnp.int32, (GLA_CHUNK, GLA_CHUNK), 0)
    cj = lax.broadcasted_iota(jnp.int32, (GLA_CHUNK, GLA_CHUNK), 1)
    n_sub = GLA_CHUNK // GLA_SUB

    def head_chunk(r0, hh):
        dk = slice(hh * GLA_DK, (hh + 1) * GLA_DK)
        dv = slice(hh * GLA_DV, (hh + 1) * GLA_DV)
        qc = q_ref[pl.ds(r0, GLA_CHUNK), dk] * (GLA_DK ** -0.5)
        kc = k_ref[pl.ds(r0, GLA_CHUNK), dk]
        vc = v_ref[pl.ds(r0, GLA_CHUNK), dv]
        b = b_ref[pl.ds(r0, GLA_CHUNK), dk]
        bl = b[GLA_CHUNK - 1:GLA_CHUNK, :]
        s = s_ref[hh]
        vb = vc.astype(BF16)

        o = jnp.dot((qc * jnp.exp(b)).astype(BF16), s.astype(BF16),
                    preferred_element_type=F32)

        parts = [jnp.zeros((GLA_SUB, GLA_CHUNK), F32)]
        for blk in range(1, n_sub):
            lo = blk * GLA_SUB
            ref = b[lo - 1:lo, :]
            qs = qc[lo:lo + GLA_SUB, :] * jnp.exp(b[lo:lo + GLA_SUB, :] - ref)
            ks = kc * jnp.exp(jnp.where(row < lo, ref - b, NEG_INF))
            parts.append(lax.dot_general(qs.astype(BF16), ks.astype(BF16),
                                         (((1,), (1,)), ((), ())),
                                         preferred_element_type=F32))
        attn = jnp.concatenate(parts, axis=0)

        for d in range(GLA_SUB):
            if d == 0:
                t = qc * kc
            else:
                kd = pltpu.roll(kc, d, 0)
                bd = pltpu.roll(b, d, 0)
                t = qc * kd * jnp.exp(jnp.where(rsub >= d, b - bd, NEG_INF))
            cs = jnp.sum(t, axis=1, keepdims=True)
            attn = attn + jnp.where(ri - cj == d, cs, 0.0)

        o = o + jnp.dot(attn.astype(BF16), vb, preferred_element_type=F32)
        o_ref[pl.ds(r0, GLA_CHUNK), dv] = o

        kdec = (kc * jnp.exp(bl - b)).astype(BF16)
        s_ref[hh] = (_col_scale(jnp.exp(bl), GLA_DV) * s
                     + lax.dot_general(kdec, vb, (((0,), (0,)), ((), ())),
                                       preferred_element_type=F32))

    def chunk(ci, carry):
        r0 = pl.multiple_of(ci * GLA_CHUNK, GLA_CHUNK)
        for hh in range(GLA_HEADS_PER_STEP):
            head_chunk(r0, hh)
        return carry

    lax.fori_loop(0, rows // GLA_CHUNK, chunk, 0)

    @pl.when(c == pl.num_programs(2) - 1)
    def _():
        sout_ref[0] = s_ref[...]


def _gla_prompt(p, alr, wup, ba, batch, seq, rows):
    nsteps = seq // rows
    hps = GLA_HEADS_PER_STEP
    wk = hps * GLA_DK
    wv = hps * GLA_DV
    kq = GLA_HEADS * GLA_DK // wk
    kv = 2 * GLA_HEADS * GLA_DK // wv

    def rowmap(b, h, c):
        return b * nsteps + c

    return pl.pallas_call(
        functools.partial(_gla_prompt_kernel, rows=rows),
        grid=(batch, GLA_HEADS // hps, nsteps),
        in_specs=[
            pl.BlockSpec((rows, wk), lambda b, h, c: (rowmap(b, h, c), h)),
            pl.BlockSpec((rows, wk), lambda b, h, c: (rowmap(b, h, c), kq + h)),
            pl.BlockSpec((rows, wv), lambda b, h, c: (rowmap(b, h, c), kv + h)),
            pl.BlockSpec((rows, LANES), lambda b, h, c: (rowmap(b, h, c), 0)),
            pl.BlockSpec((LANES, wk), lambda b, h, c: (0, h)),
            pl.BlockSpec((1, wk), lambda b, h, c: (0, h)),
        ],
        out_specs=[
            pl.BlockSpec((rows, wv), lambda b, h, c: (rowmap(b, h, c), h)),
            pl.BlockSpec((1, hps, GLA_DK, GLA_DV), lambda b, h, c: (b, h, 0, 0)),
        ],
        out_shape=[
            jax.ShapeDtypeStruct((batch * seq, GLA_HEADS * GLA_DV), F32),
            jax.ShapeDtypeStruct((batch, GLA_HEADS, GLA_DK, GLA_DV), F32),
        ],
        scratch_shapes=[pltpu.VMEM((hps, GLA_DK, GLA_DV), F32),
                        pltpu.VMEM((rows, wk), F32)],
        compiler_params=_params(("arbitrary", "arbitrary", "arbitrary")),
        name="gla_prompt",
    )(p, p, p, alr, wup, ba)


def _gla_sample_kernel(q_ref, k_ref, v_ref, alr_ref, wup_ref, ba_ref, s0_ref,
                       o_ref, s1_ref):
    first = lax.broadcasted_iota(jnp.int32, (SUBLANES, GLA_DK), 0) == 0
    for bb in range(q_ref.shape[0]):
        g_all = _log_decay(jnp.broadcast_to(alr_ref[bb], (SUBLANES, LANES)),
                           wup_ref[...], ba_ref[...])
        for h in range(GLA_HEADS):
            dk = slice(h * GLA_DK, (h + 1) * GLA_DK)
            dv = slice(h * GLA_DV, (h + 1) * GLA_DV)
            q = jnp.broadcast_to(q_ref[bb, :, dk], (SUBLANES, GLA_DK)) * (GLA_DK ** -0.5)
            k = jnp.broadcast_to(k_ref[bb, :, dk], (SUBLANES, GLA_DK))
            v = jnp.broadcast_to(v_ref[bb, :, dv], (SUBLANES, GLA_DV))
            g = g_all[:, dk]
            s = s0_ref[bb, h]
            attn = jnp.sum(q * k, axis=1, keepdims=True)
            o = attn * v + jnp.dot((q * jnp.exp(g)).astype(BF16), s.astype(BF16),
                                   preferred_element_type=F32)
            o_ref[bb, :, dv] = o[0:1, :]
            k1 = jnp.where(first, k, 0.0).astype(BF16)
            s1_ref[bb, h] = (_col_scale(jnp.exp(g[0:1, :]), GLA_DV) * s
                             + lax.dot_general(k1, v.astype(BF16), (((0,), (0,)), ((), ())),
                                               preferred_element_type=F32))


def _gla_sample(p3, alr3, wup, ba, s0, nb):
    n = p3.shape[0]
    qk = GLA_HEADS * GLA_DK
    vw = GLA_HEADS * GLA_DV
    return pl.pallas_call(
        _gla_sample_kernel,
        grid=(n // nb,),
        in_specs=[
            pl.BlockSpec((nb, 1, qk), lambda b: (b, 0, 0)),
            pl.BlockSpec((nb, 1, qk), lambda b: (b, 0, 1)),
            pl.BlockSpec((nb, 1, vw), lambda b: (b, 0, 2 * qk // vw)),
            pl.BlockSpec((nb, 1, LANES), lambda b: (b, 0, 0)),
            pl.BlockSpec((LANES, qk), lambda b: (0, 0)),
            pl.BlockSpec((1, qk), lambda b: (0, 0)),
            pl.BlockSpec((nb, GLA_HEADS, GLA_DK, GLA_DV), lambda b: (b, 0, 0, 0)),
        ],
        out_specs=[
            pl.BlockSpec((nb, 1, vw), lambda b: (b, 0, 0)),
            pl.BlockSpec((nb, GLA_HEADS, GLA_DK, GLA_DV), lambda b: (b, 0, 0, 0)),
        ],
        out_shape=[
            jax.ShapeDtypeStruct((n, 1, vw), F32),
            jax.ShapeDtypeStruct((n, GLA_HEADS, GLA_DK, GLA_DV), F32),
        ],
        compiler_params=_params(("arbitrary",)),
        name="gla_sample",
    )(p3, p3, p3, alr3, wup, ba, s0)


def _mix_kernel(o_ref, r_ref, cb_ref, cc_ref, cx_ref, ga_ref, gb_ref, pa_ref, pb_ref,
                gg_ref, cw_ref, cbias_ref, x_ref, wo_ref, g2_ref, wq_ref,
                xm_ref, h2t_ref, qp_ref, u_ref, m_ref, *, prompt, tiles_per_seq):
    tm = x_ref.shape[0]
    if prompt:
        start = (pl.program_id(0) % tiles_per_seq) == 0
        row = lax.broadcasted_iota(jnp.int32, (tm, GLA_DV), 0)
    for h in range(GLA_HEADS):
        dv = slice(h * GLA_DV, (h + 1) * GLA_DV)
        u = cc_ref[:, dv] * cx_ref[:, dv]
        if prompt:
            prev = jnp.where(start, 0.0, pa_ref[:, dv] * pb_ref[:, dv])
            um1 = jnp.where(row == 0, prev[7:8, :], pltpu.roll(u, 1, 0))
            um2 = jnp.where(row == 0, prev[6:7, :],
                            jnp.where(row == 1, prev[7:8, :], pltpu.roll(u, 2, 0)))
            u_ref[0, :, dv] = u[tm - SUBLANES:, :]
        else:
            um2 = pa_ref[:, dv]
            um1 = pb_ref[:, dv]
            u_ref[:, dv] = u
        z = (cbias_ref[:, dv] + um2 * cw_ref[0:1, dv] + um1 * cw_ref[1:2, dv]
             + u * cw_ref[2:3, dv])
        y_b = cb_ref[:, dv].astype(F32) * z
        r = r_ref[:, dv].astype(F32)
        on = _rms(o_ref[:, dv], gg_ref[:, dv])
        m = (jax.nn.sigmoid(gb_ref[:, dv].astype(F32)) * y_b
             + jax.nn.sigmoid(ga_ref[:, dv].astype(F32)) * (r * jax.nn.sigmoid(r) * on))
        m_ref[:, dv] = m.astype(BF16)
    xm = x_ref[...] + jnp.dot(m_ref[...], wo_ref[...], preferred_element_type=F32)
    xm_ref[...] = xm
    h2 = _rms(xm, g2_ref[...])
    h2t_ref[...] = jnp.transpose(h2).astype(BF16)
    qp_ref[...] = jnp.dot(h2.astype(BF16), wq_ref[...], preferred_element_type=F32)


def _mix(o, p32, p16, pa, pb, x, wts, tm, prompt, seq):
    t = o.shape[0]
    n_tiles = t // tm
    tile = pl.BlockSpec((tm, D_MODEL), lambda i: (i, 0))
    col = lambda c: pl.BlockSpec((tm, D_MODEL), lambda i: (i, c))
    cc_col = P_SEGMENT // D_MODEL
    if prompt:
        prev_rows = tm // SUBLANES
        prev = lambda c: pl.BlockSpec(
            (SUBLANES, D_MODEL), lambda i: (jnp.maximum(i * prev_rows - 1, 0), c))
        pspecs = [prev(cc_col), prev(cc_col + 1)]
        uspec = pl.BlockSpec((1, SUBLANES, D_MODEL), lambda i: (i, 0, 0))
        ushape = jax.ShapeDtypeStruct((n_tiles, SUBLANES, D_MODEL), F32)
    else:
        pspecs = [tile, tile]
        uspec = tile
        ushape = jax.ShapeDtypeStruct((t, D_MODEL), F32)
    vec = lambda r: pl.BlockSpec((r, D_MODEL), lambda i: (0, 0))
    whole = pl.BlockSpec((D_MODEL, D_MODEL), lambda i: (0, 0),
                         pipeline_mode=pl.Buffered(1))
    return pl.pallas_call(
        functools.partial(_mix_kernel, prompt=prompt, tiles_per_seq=max(seq // tm, 1)),
        grid=(n_tiles,),
        in_specs=[tile, col(0), col(1), col(cc_col), col(cc_col + 1), col(2), col(3)]
                 + pspecs + [vec(1), vec(CONV_WIDTH), vec(1), tile, whole, vec(1), whole],
        out_specs=[tile, pl.BlockSpec((D_MODEL, tm), lambda i: (0, i)), tile, uspec],
        out_shape=[jax.ShapeDtypeStruct((t, D_MODEL), F32),
                   jax.ShapeDtypeStruct((D_MODEL, t), BF16),
                   jax.ShapeDtypeStruct((t, D_MODEL), F32),
                   ushape],
        scratch_shapes=[pltpu.VMEM((tm, D_MODEL), BF16)],
        compiler_params=_params(("arbitrary",)),
        name="mix_prompt" if prompt else "mix_sample",
    )(o, p16, p16, p32, p32, p16, p16, pa, pb, wts["gg"], wts["cw"], wts["cbias"],
      x, wts["wo"], wts["g2"], wts["wq"])


def _top_values(work, n, with_rank=False):
    vals = []
    rank = jnp.full(work.shape, float(n), F32) if with_rank else None
    for r in range(n):
        mx = jnp.max(work, axis=0, keepdims=True)
        hit = work == mx
        vals.append(mx)
        if with_rank:
            rank = jnp.where(hit, float(r), rank)
        work = jnp.where(hit, NEG_INF, work)
    return vals, rank


def _route_kernel(qp_ref, keys_ref, n1_ref, p1_ref, rank2_ref, p2_ref):
    for h in range(PEER_HEADS):
        st = []
        for side in range(2):
            c0 = (2 * h + side) * DQ_HALF
            st.append(lax.dot_general(keys_ref[side], qp_ref[:, c0:c0 + DQ_HALF],
                                      (((1,), (1,)), ((), ())),
                                      precision=lax.Precision.HIGHEST,
                                      preferred_element_type=F32))
        v1, _ = _top_values(st[0], PEER_TOPK)
        v2, rank2 = _top_values(st[1], PEER_TOPK, with_rank=True)
        v2 = jnp.concatenate(v2, axis=0)
        tm = v2.shape[1]
        bidx = lax.broadcasted_iota(jnp.int32, (SUBLANES, tm), 0)
        cand = []
        for a in range(PEER_TOPK // 2):
            nb = PEER_TOPK // (a + 1)
            if nb >= SUBLANES:
                cand.append(v1[a] + v2[0:nb, :])
            else:
                cand.append(jnp.where(bidx < nb, v1[a] + v2[0:SUBLANES, :], NEG_INF))
        tail = jnp.concatenate(v1[PEER_TOPK // 2:], axis=0) + v2[0:1, :]
        cand.append(tail)
        thr = _top_values(jnp.concatenate(cand, axis=0), PEER_TOPK)[0][-1]
        m1 = v1[0]
        m2 = v2[0:1, :]
        z = jnp.zeros_like(m1)
        n1 = jnp.zeros_like(st[0])
        for a, ca in enumerate(cand):
            keep = ca >= thr
            z = z + jnp.sum(jnp.where(keep, jnp.exp(ca - (m1 + m2)), 0.0),
                            axis=0, keepdims=True)
            kept = jnp.where(keep, 1.0, 0.0)
            if a < PEER_TOPK // 2:
                n1 = jnp.where(st[0] == v1[a], jnp.sum(kept, axis=0, keepdims=True), n1)
            else:
                for r in range(PEER_TOPK // 2):
                    n1 = jnp.where(st[0] == v1[a + r], kept[r:r + 1, :], n1)
        n1_ref[h] = n1
        p1_ref[h] = jnp.exp(st[0] - m1)
        rank2_ref[h] = rank2.astype(BF16)
        p2_ref[h] = (0.5 * jnp.exp(st[1] - m2) / z).astype(BF16)


def _route(qp, keys, tm):
    t = qp.shape[0]
    big = pl.BlockSpec((PEER_HEADS, N_KEYS, tm), lambda i: (0, 0, i))
    f32s = jax.ShapeDtypeStruct((PEER_HEADS, N_KEYS, t), F32)
    bf16s = jax.ShapeDtypeStruct((PEER_HEADS, N_KEYS, t), BF16)
    return pl.pallas_call(
        _route_kernel,
        grid=(t // tm,),
        in_specs=[pl.BlockSpec((tm, D_MODEL), lambda i: (i, 0)),
                  pl.BlockSpec((2, N_KEYS, DQ_HALF), lambda i: (0, 0, 0))],
        out_specs=[big, big, big, big],
        out_shape=[f32s, f32s, bf16s, bf16s],
        compiler_params=_params(("arbitrary",)),
        name="route",
    )(qp, keys)


def _row_bf16(ref, h, e1, tm):
    row = jnp.broadcast_to(ref[h, pl.ds(e1, 1), :], (BF16_ROWS, tm)).astype(BF16)
    return jnp.concatenate([row] * (N_KEYS // BF16_ROWS), axis=0)


def _peer_kernel(h2t_ref, u_ref, v_ref, n1_ref, p1_ref, rank2_ref, p2_ref,
                 xm_ref, gf_ref, out_ref, *, te):
    j = pl.program_id(1)

    @pl.when(j == 0)
    def _():
        out_ref[...] = jnp.zeros_like(out_ref)

    tm = h2t_ref.shape[1]
    rows_per_part = PEER_PART_ROWS
    acc = None
    for part in range(te // rows_per_part):
        lo = part * rows_per_part
        a = jnp.dot(u_ref[lo:lo + rows_per_part, :], h2t_ref[...],
                    preferred_element_type=F32)
        blocks = []
        for l in range(rows_per_part // N_KEYS):
            e1 = j * (te // N_KEYS) + part * (rows_per_part // N_KEYS) + l
            g = jnp.zeros((N_KEYS, tm), BF16)
            for h in range(PEER_HEADS):
                keep = rank2_ref[h] < _row_bf16(n1_ref, h, e1, tm)
                g = g + jnp.where(keep, p2_ref[h], 0.0) * _row_bf16(p1_ref, h, e1, tm)
            al = a[l * N_KEYS:(l + 1) * N_KEYS, :]
            gelu2 = al * (1.0 + lax.erf(al * INV_SQRT2))
            blocks.append(gelu2.astype(BF16) * g)
        wt = jnp.concatenate(blocks, axis=0) if len(blocks) > 1 else blocks[0]
        d = jnp.dot(jnp.transpose(wt), v_ref[lo:lo + rows_per_part, :],
                    preferred_element_type=F32)
        acc = d if acc is None else acc + d
    out_ref[...] += acc

    @pl.when(j == pl.num_programs(1) - 1)
    def _():
        out_ref[...] = _rms(xm_ref[...] + out_ref[...], gf_ref[...])


def _peer(h2t, u_bf, v_bf, n1, p1, rank2, p2, xm, gf, tm, te):
    t = xm.shape[0]
    big = pl.BlockSpec((PEER_HEADS, N_KEYS, tm), lambda i, j: (0, 0, i))
    return pl.pallas_call(
        functools.partial(_peer_kernel, te=te),
        grid=(t // tm, N_EXPERTS // te),
        in_specs=[pl.BlockSpec((D_MODEL, tm), lambda i, j: (0, i)),
                  pl.BlockSpec((te, D_MODEL), lambda i, j: (j, 0)),
                  pl.BlockSpec((te, D_MODEL), lambda i, j: (j, 0)),
                  big, big, big, big,
                  pl.BlockSpec((tm, D_MODEL), lambda i, j: (i, 0)),
                  pl.BlockSpec((1, D_MODEL), lambda i, j: (0, 0))],
        out_specs=pl.BlockSpec((tm, D_MODEL), lambda i, j: (i, 0)),
        out_shape=jax.ShapeDtypeStruct((t, D_MODEL), F32),
        compiler_params=_params(("arbitrary", "arbitrary")),
        name="peer",
    )(h2t, u_bf, v_bf, n1, p1, rank2, p2, xm, gf)


def _channel_mixer(o, p32, p16, pa, pb, x, wts, tm_mix, tm_route, tm_peer, te, prompt, seq):
    xm, h2t, qp, utail = _mix(o, p32, p16, pa, pb, x, wts, tm_mix, prompt, seq)
    n1, p1, rank2, p2 = _route(qp, wts["keys"], tm_route)
    y = _peer(h2t, wts["u"], wts["v"], n1, p1, rank2, p2, xm, wts["gf"], tm_peer, te)
    return y, utail


def kernel(x_prompt, x_sample, state_gla, state_conv, norm1_g, w_in, w_alpha_up, b_alpha,
           gla_norm_g, conv_w, conv_b, w_out, norm2_g, w_query, sub_keys, expert_u, expert_v,
           norm_f_g):
    batch, seq, _ = x_prompt.shape
    n_dec = x_sample.shape[0]
    assert w_in.shape[0] == 1 and x_sample.shape[1] == 1

    alr0 = 2 * GLA_HEADS * GLA_DK + 2 * GLA_HEADS * GLA_DV
    w_bf = w_in[0].astype(BF16)
    w_lo = w_bf
    w_hi = w_bf[:, alr0 + GLA_RANK:]
    w_alr = jnp.pad(w_bf[:, alr0:alr0 + GLA_RANK], ((0, 0), (0, LANES - GLA_RANK)))
    wup = jnp.pad(w_alpha_up[0], ((0, LANES - GLA_RANK), (0, 0))).astype(BF16)
    ba = b_alpha[0][None, :]
    g1 = norm1_g[0][None, :]
    wts = dict(gg=gla_norm_g[0][None, :], cw=conv_w[0], cbias=conv_b[0][None, :],
               g2=norm2_g[0][None, :], keys=sub_keys[0],
               gf=norm_f_g[None, :])

    xp = x_prompt.reshape(batch * seq, D_MODEL)
    p32, p16, alr, wts["u"], wts["v"], wts["wo"], wts["wq"] = _inproj(
        xp, g1, w_lo, w_hi, w_alr, tm=TM_INPROJ, tn=TN_INPROJ,
        tables=(expert_u[0], expert_v[0], w_out[0], w_query[0]))
    o, s_p = _gla_prompt(p32, alr, wup, ba, batch, seq, rows=GLA_ROWS)
    y_p, utail = _channel_mixer(o, p32, p16, p32, p32, xp, wts, tm_mix=TM_MIX,
                                tm_route=TM_ROUTE, tm_peer=TM_PEER, te=TE_PEER,
                                prompt=True, seq=seq)
    last = utail.reshape(batch, seq // TM_MIX, SUBLANES, D_MODEL)[:, -1, SUBLANES - 2:, :]

    xs = x_sample.reshape(n_dec, D_MODEL)
    ps32, ps16, alrs = _inproj(xs, g1, w_lo, w_hi, w_alr, tm=n_dec, tn=TN_INPROJ)
    os_, s_s = _gla_sample(ps32.reshape(n_dec, 1, N_MAIN // 2),
                           alrs.reshape(n_dec, 1, LANES), wup, ba, state_gla[0],
                           nb=GLA_SAMPLE_SEQS)
    y_s, us = _channel_mixer(os_.reshape(n_dec, D_MODEL), ps32, ps16, state_conv[0][:, 0, :],
                             state_conv[0][:, 1, :], xs, wts, tm_mix=n_dec, tm_route=n_dec,
                             tm_peer=n_dec, te=TE_PEER, prompt=False, seq=1)
    conv_s = jnp.stack([state_conv[0][:, 1, :], us], axis=1)

    return (y_p.reshape(batch, seq, D_MODEL),
            y_s.reshape(n_dec, 1, D_MODEL),
            s_p[None],
            last[None],
            s_s[None],
            conv_s[None])
```

```python
import functools
import math

import jax
import jax.numpy as jnp
from jax import lax
from jax.experimental import pallas as pl
from jax.experimental.pallas import tpu as pltpu

F32 = jnp.float32
BF16 = jnp.bfloat16

D_MODEL = 2048
GLA_HEADS = 4
GLA_DK = 256
GLA_DV = 512
GLA_RANK = 16
GLA_TAU = 16.0
GLA_CHUNK = 64
CONV_WIDTH = 3
PEER_HEADS = 8
N_KEYS = 128
N_EXPERTS = N_KEYS * N_KEYS
PEER_TOPK = 16
DQ_HALF = 128
EPS = 1e-6

LANES = 128
SUBLANES = 8
BF16_ROWS = 16
VMEM_LIMIT_BYTES = 60000 * 1024

N_MAIN = 16384
P_SEGMENT = 4096
GLA_SUB = 8
GLA_HEADS_PER_STEP = 4
PEER_PART_ROWS = 512

TM_INPROJ = 1024
TN_INPROJ = 1024
GLA_ROWS = 512
GLA_SAMPLE_SEQS = 4
TM_MIX = 256
TM_ROUTE = 256
TM_PEER = 512
TE_PEER = 1024
NEG_INF = float("-inf")
INV_SQRT2 = 1.0 / math.sqrt(2.0)


def _params(semantics):
    return pltpu.CompilerParams(dimension_semantics=semantics,
                                vmem_limit_bytes=VMEM_LIMIT_BYTES)


def _rms(x, g):
    return x * lax.rsqrt(jnp.mean(x * x, axis=-1, keepdims=True) + EPS) * g


def _inproj_kernel(*refs, n_lo, tiles_per_seg, n_tables):
    x_ref, g_ref, wlo_ref, whi_ref, walr_ref = refs[:5]
    tables_in = refs[5:5 + n_tables]
    p32_ref, p16_ref, alr_ref = refs[5 + n_tables:8 + n_tables]
    tables_out = refs[8 + n_tables:8 + 2 * n_tables]
    h_ref = refs[-1]
    j = pl.program_id(1)

    @pl.when(j == 0)
    def _():
        hb = _rms(x_ref[...], g_ref[...]).astype(BF16)
        h_ref[...] = hb
        alr_ref[...] = jnp.dot(hb, walr_ref[...], preferred_element_type=F32)

    wide = (j // tiles_per_seg) % 2 == 0
    low = j < n_lo
    for w_ref, from_low in ((wlo_ref, True), (whi_ref, False)):
        for out_ref, is_wide in ((p32_ref, True), (p16_ref, False)):
            @pl.when((low == from_low) & (wide == is_wide))
            def _():
                out_ref[...] = jnp.dot(h_ref[...], w_ref[...],
                                       preferred_element_type=F32).astype(out_ref.dtype)

    for src_ref, dst_ref in zip(tables_in, tables_out):
        dst_ref[...] = src_ref[...].astype(BF16)


def _inproj(x, g, w_lo, w_hi, w_alr, tm, tn, tables=()):
    t = x.shape[0]
    n_lo = (N_MAIN - w_hi.shape[1]) // tn
    per = P_SEGMENT // tn
    nj = N_MAIN // tn
    steps = (t // tm) * nj

    def idx32(j):
        seg, r = j // per, j % per
        return jnp.where(seg % 2 == 0, (seg // 2) * per + r, (seg // 2 + 1) * per - 1)

    def idx16(j):
        seg, r = j // per, j % per
        return jnp.where(seg % 2 == 1, (seg // 2) * per + r,
                         jnp.maximum((seg // 2) * per - 1, 0))

    slabs = [pl.BlockSpec((tb.shape[0] // steps, D_MODEL), lambda i, j: (i * nj + j, 0))
             for tb in tables]
    return pl.pallas_call(
        functools.partial(_inproj_kernel, n_lo=n_lo, tiles_per_seg=per,
                          n_tables=len(tables)),
        grid=(t // tm, nj),
        in_specs=[
            pl.BlockSpec((tm, D_MODEL), lambda i, j: (i, 0), pipeline_mode=pl.Buffered(1)),
            pl.BlockSpec((1, D_MODEL), lambda i, j: (0, 0)),
            pl.BlockSpec((D_MODEL, tn), lambda i, j: (0, jnp.minimum(j, n_lo - 1))),
            pl.BlockSpec((D_MODEL, tn), lambda i, j: (0, jnp.maximum(j - n_lo, 0))),
            pl.BlockSpec((D_MODEL, LANES), lambda i, j: (0, 0)),
        ] + slabs,
        out_specs=[
            pl.BlockSpec((tm, tn), lambda i, j: (i, idx32(j))),
            pl.BlockSpec((tm, tn), lambda i, j: (i, idx16(j))),
            pl.BlockSpec((tm, LANES), lambda i, j: (i, 0)),
        ] + slabs,
        out_shape=[
            jax.ShapeDtypeStruct((t, N_MAIN // 2), F32),
            jax.ShapeDtypeStruct((t, N_MAIN // 2), BF16),
            jax.ShapeDtypeStruct((t, LANES), F32),
        ] + [jax.ShapeDtypeStruct(tb.shape, BF16) for tb in tables],
        scratch_shapes=[pltpu.VMEM((tm, D_MODEL), BF16)],
        compiler_params=_params(("arbitrary", "arbitrary")),
        name="inproj",
    )(x, g, w_lo, w_hi, w_alr, *tables)


def _log_decay(alr, wup, ba):
    xa = jnp.dot(alr.astype(BF16), wup, preferred_element_type=F32) + ba
    return -(jnp.maximum(-xa, 0.0) + jnp.log1p(jnp.exp(-jnp.abs(xa)))) * (1.0 / GLA_TAU)


def _col_scale(row, width):
    col = jnp.transpose(jnp.broadcast_to(row, (LANES, row.shape[1])))
    return jnp.concatenate([col] * (width // LANES), axis=1)


def _gla_prompt_kernel(q_ref, k_ref, v_ref, alr_ref, wup_ref, ba_ref,
                       o_ref, sout_ref, s_ref, b_ref, *, rows):
    c = pl.program_id(2)

    @pl.when(c == 0)
    def _():
        s_ref[...] = jnp.zeros_like(s_ref)

    g = _log_decay(alr_ref[...], wup_ref[...], ba_ref[...])
    rmod = lax.broadcasted_iota(jnp.int32, g.shape, 0) & (GLA_CHUNK - 1)
    step = 1
    while step < GLA_CHUNK:
        g = g + jnp.where(rmod >= step, pltpu.roll(g, step, 0), 0.0)
        step *= 2
    b_ref[...] = g

    row = lax.broadcasted_iota(jnp.int32, (GLA_CHUNK, GLA_DK), 0)
    rsub = row & (GLA_SUB - 1)
    ri = lax.broadcasted_iota(jnp.int32, (GLA_CHUNK, GLA_CHUNK), 0)
    cj = lax.broadcasted_iota(jnp.int32, (GLA_CHUNK, GLA_CHUNK), 1)
    n_sub = GLA_CHUNK // GLA_SUB

    def head_chunk(r0, hh):
        dk = slice(hh * GLA_DK, (hh + 1) * GLA_DK)
        dv = slice(hh * GLA_DV, (hh + 1) * GLA_DV)
        qc = q_ref[pl.ds(r0, GLA_CHUNK), dk] * (GLA_DK ** -0.5)
        kc = k_ref[pl.ds(r0, GLA_CHUNK), dk]
        vc = v_ref[pl.ds(r0, GLA_CHUNK), dv]
        b = b_ref[pl.ds(r0, GLA_CHUNK), dk]
        bl = b[GLA_CHUNK - 1:GLA_CHUNK, :]
        s = s_ref[hh]
        vb = vc.astype(BF16)

        o = jnp.dot((qc * jnp.exp(b)).astype(BF16), s.astype(BF16),
                    preferred_element_type=F32)

        parts = [jnp.zeros((GLA_SUB, GLA_CHUNK), F32)]
        for blk in range(1, n_sub):
            lo = blk * GLA_SUB
            ref = b[lo - 1:lo, :]
            qs = qc[lo:lo + GLA_SUB, :] * jnp.exp(b[lo:lo + GLA_SUB, :] - ref)
            ks = kc * jnp.exp(jnp.where(row < lo, ref - b, NEG_INF))
            parts.append(lax.dot_general(qs.astype(BF16), ks.astype(BF16),
                                         (((1,), (1,)), ((), ())),
                                         preferred_element_type=F32))
        attn = jnp.concatenate(parts, axis=0)

        for d in range(GLA_SUB):
            if d == 0:
                t = qc * kc
            else:
                kd = pltpu.roll(kc, d, 0)
                bd = pltpu.roll(b, d, 0)
                t = qc * kd * jnp.exp(jnp.where(rsub >= d, b - bd, NEG_INF))
            cs = jnp.sum(t, axis=1, keepdims=True)
            attn = attn + jnp.where(ri - cj == d, cs, 0.0)

        o = o + jnp.dot(attn.astype(BF16), vb, preferred_element_type=F32)
        o_ref[pl.ds(r0, GLA_CHUNK), dv] = o

        kdec = (kc * jnp.exp(bl - b)).astype(BF16)
        s_ref[hh] = (_col_scale(jnp.exp(bl), GLA_DV) * s
                     + lax.dot_general(kdec, vb, (((0,), (0,)), ((), ())),
                                       preferred_element_type=F32))

    def chunk(ci, carry):
        r0 = pl.multiple_of(ci * GLA_CHUNK, GLA_CHUNK)
        for hh in range(GLA_HEADS_PER_STEP):
            head_chunk(r0, hh)
        return carry

    lax.fori_loop(0, rows // GLA_CHUNK, chunk, 0)

    @pl.when(c == pl.num_programs(2) - 1)
    def _():
        sout_ref[0] = s_ref[...]


def _gla_prompt(p, alr, wup, ba, batch, seq, rows):
    nsteps = seq // rows
    hps = GLA_HEADS_PER_STEP
    wk = hps * GLA_DK
    wv = hps * GLA_DV
    kq = GLA_HEADS * GLA_DK // wk
    kv = 2 * GLA_HEADS * GLA_DK // wv

    def rowmap(b, h, c):
        return b * nsteps + c

    return pl.pallas_call(
        functools.partial(_gla_prompt_kernel, rows=rows),
        grid=(batch, GLA_HEADS // hps, nsteps),
        in_specs=[
            pl.BlockSpec((rows, wk), lambda b, h, c: (rowmap(b, h, c), h)),
            pl.BlockSpec((rows, wk), lambda b, h, c: (rowmap(b, h, c), kq + h)),
            pl.BlockSpec((rows, wv), lambda b, h, c: (rowmap(b, h, c), kv + h)),
            pl.BlockSpec((rows, LANES), lambda b, h, c: (rowmap(b, h, c), 0)),
            pl.BlockSpec((LANES, wk), lambda b, h, c: (0, h)),
            pl.BlockSpec((1, wk), lambda b, h, c: (0, h)),
        ],
        out_specs=[
            pl.BlockSpec((rows, wv), lambda b, h, c: (rowmap(b, h, c), h)),
            pl.BlockSpec((1, hps, GLA_DK, GLA_DV), lambda b, h, c: (b, h, 0, 0)),
        ],
        out_shape=[
            jax.ShapeDtypeStruct((batch * seq, GLA_HEADS * GLA_DV), F32),
            jax.ShapeDtypeStruct((batch, GLA_HEADS, GLA_DK, GLA_DV), F32),
        ],
        scratch_shapes=[pltpu.VMEM((hps, GLA_DK, GLA_DV), F32),
                        pltpu.VMEM((rows, wk), F32)],
        compiler_params=_params(("arbitrary", "arbitrary", "arbitrary")),
        name="gla_prompt",
    )(p, p, p, alr, wup, ba)


def _gla_sample_kernel(q_ref, k_ref, v_ref, alr_ref, wup_ref, ba_ref, s0_ref,
                       o_ref, s1_ref):
    first = lax.broadcasted_iota(jnp.int32, (SUBLANES, GLA_DK), 0) == 0
    for bb in range(q_ref.shape[0]):
        g_all = _log_decay(jnp.broadcast_to(alr_ref[bb], (SUBLANES, LANES)),
                           wup_ref[...], ba_ref[...])
        for h in range(GLA_HEADS):
            dk = slice(h * GLA_DK, (h + 1) * GLA_DK)
            dv = slice(h * GLA_DV, (h + 1) * GLA_DV)
            q = jnp.broadcast_to(q_ref[bb, :, dk], (SUBLANES, GLA_DK)) * (GLA_DK ** -0.5)
            k = jnp.broadcast_to(k_ref[bb, :, dk], (SUBLANES, GLA_DK))
            v = jnp.broadcast_to(v_ref[bb, :, dv], (SUBLANES, GLA_DV))
            g = g_all[:, dk]
            s = s0_ref[bb, h]
            attn = jnp.sum(q * k, axis=1, keepdims=True)
            o = attn * v + jnp.dot((q * jnp.exp(g)).astype(BF16), s.astype(BF16),
                                   preferred_element_type=F32)
            o_ref[bb, :, dv] = o[0:1, :]
            k1 = jnp.where(first, k, 0.0).astype(BF16)
            s1_ref[bb, h] = (_col_scale(jnp.exp(g[0:1, :]), GLA_DV) * s
                             + lax.dot_general(k1, v.astype(BF16), (((0,), (0,)), ((), ())),
                                               preferred_element_type=F32))


def _gla_sample(p3, alr3, wup, ba, s0, nb):
    n = p3.shape[0]
    qk = GLA_HEADS * GLA_DK
    vw = GLA_HEADS * GLA_DV
    return pl.pallas_call(
        _gla_sample_kernel,
        grid=(n // nb,),
        in_specs=[
            pl.BlockSpec((nb, 1, qk), lambda b: (b, 0, 0)),
            pl.BlockSpec((nb, 1, qk), lambda b: (b, 0, 1)),
            pl.BlockSpec((nb, 1, vw), lambda b: (b, 0, 2 * qk // vw)),
            pl.BlockSpec((nb, 1, LANES), lambda b: (b, 0, 0)),
            pl.BlockSpec((LANES, qk), lambda b: (0, 0)),
            pl.BlockSpec((1, qk), lambda b: (0, 0)),
            pl.BlockSpec((nb, GLA_HEADS, GLA_DK, GLA_DV), lambda b: (b, 0, 0, 0)),
        ],
        out_specs=[
            pl.BlockSpec((nb, 1, vw), lambda b: (b, 0, 0)),
            pl.BlockSpec((nb, GLA_HEADS, GLA_DK, GLA_DV), lambda b: (b, 0, 0, 0)),
        ],
        out_shape=[
            jax.ShapeDtypeStruct((n, 1, vw), F32),
            jax.ShapeDtypeStruct((n, GLA_HEADS, GLA_DK, GLA_DV), F32),
        ],
        compiler_params=_params(("arbitrary",)),
        name="gla_sample",
    )(p3, p3, p3, alr3, wup, ba, s0)


def _mix_kernel(o_ref, r_ref, cb_ref, cc_ref, cx_ref, ga_ref, gb_ref, pa_ref, pb_ref,
                gg_ref, cw_ref, cbias_ref, x_ref, wo_ref, g2_ref, wq_ref,
                xm_ref, h2t_ref, qp_ref, u_ref, m_ref, *, prompt, tiles_per_seq):
    tm = x_ref.shape[0]
    if prompt:
        start = (pl.program_id(0) % tiles_per_seq) == 0
        row = lax.broadcasted_iota(jnp.int32, (tm, GLA_DV), 0)
    for h in range(GLA_HEADS):
        dv = slice(h * GLA_DV, (h + 1) * GLA_DV)
        u = cc_ref[:, dv] * cx_ref[:, dv]
        if prompt:
            prev = jnp.where(start, 0.0, pa_ref[:, dv] * pb_ref[:, dv])
            um1 = jnp.where(row == 0, prev[7:8, :], pltpu.roll(u, 1, 0))
            um2 = jnp.where(row == 0, prev[6:7, :],
                            jnp.where(row == 1, prev[7:8, :], pltpu.roll(u, 2, 0)))
            u_ref[0, :, dv] = u[tm - SUBLANES:, :]
        else:
            um2 = pa_ref[:, dv]
            um1 = pb_ref[:, dv]
            u_ref[:, dv] = u
        z = (cbias_ref[:, dv] + um2 * cw_ref[0:1, dv] + um1 * cw_ref[1:2, dv]
             + u * cw_ref[2:3, dv])
        y_b = cb_ref[:, dv].astype(F32) * z
        r = r_ref[:, dv].astype(F32)
        on = _rms(o_ref[:, dv], gg_ref[:, dv])
        m = (jax.nn.sigmoid(gb_ref[:, dv].astype(F32)) * y_b
             + jax.nn.sigmoid(ga_ref[:, dv].astype(F32)) * (r * jax.nn.sigmoid(r) * on))
        m_ref[:, dv] = m.astype(BF16)
    xm = x_ref[...] + jnp.dot(m_ref[...], wo_ref[...], preferred_element_type=F32)
    xm_ref[...] = xm
    h2 = _rms(xm, g2_ref[...])
    h2t_ref[...] = jnp.transpose(h2).astype(BF16)
    qp_ref[...] = jnp.dot(h2.astype(BF16), wq_ref[...], preferred_element_type=F32)


def _mix(o, p32, p16, pa, pb, x, wts, tm, prompt, seq):
    t = o.shape[0]
    n_tiles = t // tm
    tile = pl.BlockSpec((tm, D_MODEL), lambda i: (i, 0))
    col = lambda c: pl.BlockSpec((tm, D_MODEL), lambda i: (i, c))
    cc_col = P_SEGMENT // D_MODEL
    if prompt:
        prev_rows = tm // SUBLANES
        prev = lambda c: pl.BlockSpec(
            (SUBLANES, D_MODEL), lambda i: (jnp.maximum(i * prev_rows - 1, 0), c))
        pspecs = [prev(cc_col), prev(cc_col + 1)]
        uspec = pl.BlockSpec((1, SUBLANES, D_MODEL), lambda i: (i, 0, 0))
        ushape = jax.ShapeDtypeStruct((n_tiles, SUBLANES, D_MODEL), F32)
    else:
        pspecs = [tile, tile]
        uspec = tile
        ushape = jax.ShapeDtypeStruct((t, D_MODEL), F32)
    vec = lambda r: pl.BlockSpec((r, D_MODEL), lambda i: (0, 0))
    whole = pl.BlockSpec((D_MODEL, D_MODEL), lambda i: (0, 0),
                         pipeline_mode=pl.Buffered(1))
    return pl.pallas_call(
        functools.partial(_mix_kernel, prompt=prompt, tiles_per_seq=max(seq // tm, 1)),
        grid=(n_tiles,),
        in_specs=[tile, col(0), col(1), col(cc_col), col(cc_col + 1), col(2), col(3)]
                 + pspecs + [vec(1), vec(CONV_WIDTH), vec(1), tile, whole, vec(1), whole],
        out_specs=[tile, pl.BlockSpec((D_MODEL, tm), lambda i: (0, i)), tile, uspec],
        out_shape=[jax.ShapeDtypeStruct((t, D_MODEL), F32),
                   jax.ShapeDtypeStruct((D_MODEL, t), BF16),
                   jax.ShapeDtypeStruct((t, D_MODEL), F32),
                   ushape],
        scratch_shapes=[pltpu.VMEM((tm, D_MODEL), BF16)],
        compiler_params=_params(("arbitrary",)),
        name="mix_prompt" if prompt else "mix_sample",
    )(o, p16, p16, p32, p32, p16, p16, pa, pb, wts["gg"], wts["cw"], wts["cbias"],
      x, wts["wo"], wts["g2"], wts["wq"])


def _top_values(work, n, with_rank=False):
    vals = []
    rank = jnp.full(work.shape, float(n), F32) if with_rank else None
    for r in range(n):
        mx = jnp.max(work, axis=0, keepdims=True)
        hit = work == mx
        vals.append(mx)
        if with_rank:
            rank = jnp.where(hit, float(r), rank)
        work = jnp.where(hit, NEG_INF, work)
    return vals, rank


def _route_kernel(qp_ref, keys_ref, n1_ref, p1_ref, rank2_ref, p2_ref):
    for h in range(PEER_HEADS):
        st = []
        for side in range(2):
            c0 = (2 * h + side) * DQ_HALF
            st.append(lax.dot_general(keys_ref[side], qp_ref[:, c0:c0 + DQ_HALF],
                                      (((1,), (1,)), ((), ())),
                                      precision=lax.Precision.HIGHEST,
                                      preferred_element_type=F32))
        v1, _ = _top_values(st[0], PEER_TOPK)
        v2, rank2 = _top_values(st[1], PEER_TOPK, with_rank=True)
        v2 = jnp.concatenate(v2, axis=0)
        tm = v2.shape[1]
        bidx = lax.broadcasted_iota(jnp.int32, (SUBLANES, tm), 0)
        cand = []
        for a in range(PEER_TOPK // 2):
            nb = PEER_TOPK // (a + 1)
            if nb >= SUBLANES:
                cand.append(v1[a] + v2[0:nb, :])
            else:
                cand.append(jnp.where(bidx < nb, v1[a] + v2[0:SUBLANES, :], NEG_INF))
        tail = jnp.concatenate(v1[PEER_TOPK // 2:], axis=0) + v2[0:1, :]
        cand.append(tail)
        thr = _top_values(jnp.concatenate(cand, axis=0), PEER_TOPK)[0][-1]
        m1 = v1[0]
        m2 = v2[0:1, :]
        z = jnp.zeros_like(m1)
        n1 = jnp.zeros_like(st[0])
        for a, ca in enumerate(cand):
            keep = ca >= thr
            z = z + jnp.sum(jnp.where(keep, jnp.exp(ca - (m1 + m2)), 0.0),
                            axis=0, keepdims=True)
            kept = jnp.where(keep, 1.0, 0.0)
            if a < PEER_TOPK // 2:
                n1 = jnp.where(st[0] == v1[a], jnp.sum(kept, axis=0, keepdims=True), n1)
            else:
                for r in range(PEER_TOPK // 2):
                    n1 = jnp.where(st[0] == v1[a + r], kept[r:r + 1, :], n1)
        n1_ref[h] = n1
        p1_ref[h] = jnp.exp(st[0] - m1)
        rank2_ref[h] = rank2.astype(BF16)
        p2_ref[h] = (0.5 * jnp.exp(st[1] - m2) / z).astype(BF16)


def _route(qp, keys, tm):
    t = qp.shape[0]
    big = pl.BlockSpec((PEER_HEADS, N_KEYS, tm), lambda i: (0, 0, i))
    f32s = jax.ShapeDtypeStruct((PEER_HEADS, N_KEYS, t), F32)
    bf16s = jax.ShapeDtypeStruct((PEER_HEADS, N_KEYS, t), BF16)
    return pl.pallas_call(
        _route_kernel,
        grid=(t // tm,),
        in_specs=[pl.BlockSpec((tm, D_MODEL), lambda i: (i, 0)),
                  pl.BlockSpec((2, N_KEYS, DQ_HALF), lambda i: (0, 0, 0))],
        out_specs=[big, big, big, big],
        out_shape=[f32s, f32s, bf16s, bf16s],
        compiler_params=_params(("arbitrary",)),
        name="route",
    )(qp, keys)


def _row_bf16(ref, h, e1, tm):
    row = jnp.broadcast_to(ref[h, pl.ds(e1, 1), :], (BF16_ROWS, tm)).astype(BF16)
    return jnp.concatenate([row] * (N_KEYS // BF16_ROWS), axis=0)


def _peer_kernel(h2t_ref, u_ref, v_ref, n1_ref, p1_ref, rank2_ref, p2_ref,
                 xm_ref, gf_ref, out_ref, *, te):
    j = pl.program_id(1)

    @pl.when(j == 0)
    def _():
        out_ref[...] = jnp.zeros_like(out_ref)

    tm = h2t_ref.shape[1]
    rows_per_part = PEER_PART_ROWS
    acc = None
    for part in range(te // rows_per_part):
        lo = part * rows_per_part
        a = jnp.dot(u_ref[lo:lo + rows_per_part, :], h2t_ref[...],
                    preferred_element_type=F32)
        blocks = []
        for l in range(rows_per_part // N_KEYS):
            e1 = j * (te // N_KEYS) + part * (rows_per_part // N_KEYS) + l
            g = jnp.zeros((N_KEYS, tm), BF16)
            for h in range(PEER_HEADS):
                keep = rank2_ref[h] < _row_bf16(n1_ref, h, e1, tm)
                g = g + jnp.where(keep, p2_ref[h], 0.0) * _row_bf16(p1_ref, h, e1, tm)
            al = a[l * N_KEYS:(l + 1) * N_KEYS, :]
            gelu2 = al * (1.0 + lax.erf(al * INV_SQRT2))
            blocks.append(gelu2.astype(BF16) * g)
        wt = jnp.concatenate(blocks, axis=0) if len(blocks) > 1 else blocks[0]
        d = jnp.dot(jnp.transpose(wt), v_ref[lo:lo + rows_per_part, :],
                    preferred_element_type=F32)
        acc = d if acc is None else acc + d
    out_ref[...] += acc

    @pl.when(j == pl.num_programs(1) - 1)
    def _():
        out_ref[...] = _rms(xm_ref[...] + out_ref[...], gf_ref[...])


def _peer(h2t, u_bf, v_bf, n1, p1, rank2, p2, xm, gf, tm, te):
    t = xm.shape[0]
    big = pl.BlockSpec((PEER_HEADS, N_KEYS, tm), lambda i, j: (0, 0, i))
    return pl.pallas_call(
        functools.partial(_peer_kernel, te=te),
        grid=(t // tm, N_EXPERTS // te),
        in_specs=[pl.BlockSpec((D_MODEL, tm), lambda i, j: (0, i)),
                  pl.BlockSpec((te, D_MODEL), lambda i, j: (j, 0)),
                  pl.BlockSpec((te, D_MODEL), lambda i, j: (j, 0)),
                  big, big, big, big,
                  pl.BlockSpec((tm, D_MODEL), lambda i, j: (i, 0)),
                  pl.BlockSpec((1, D_MODEL), lambda i, j: (0, 0))],
        out_specs=pl.BlockSpec((tm, D_MODEL), lambda i, j: (i, 0)),
        out_shape=jax.ShapeDtypeStruct((t, D_MODEL), F32),
        compiler_params=_params(("arbitrary", "arbitrary")),
        name="peer",
    )(h2t, u_bf, v_bf, n1, p1, rank2, p2, xm, gf)


def _channel_mixer(o, p32, p16, pa, pb, x, wts, tm_mix, tm_route, tm_peer, te, prompt, seq):
    xm, h2t, qp, utail = _mix(o, p32, p16, pa, pb, x, wts, tm_mix, prompt, seq)
    n1, p1, rank2, p2 = _route(qp, wts["keys"], tm_route)
    y = _peer(h2t, wts["u"], wts["v"], n1, p1, rank2, p2, xm, wts["gf"], tm_peer, te)
    return y, utail


def kernel(x_prompt, x_sample, state_gla, state_conv, norm1_g, w_in, w_alpha_up, b_alpha,
           gla_norm_g, conv_w, conv_b, w_out, norm2_g, w_query, sub_keys, expert_u, expert_v,
           norm_f_g):
    batch, seq, _ = x_prompt.shape
    n_dec = x_sample.shape[0]
    assert w_in.shape[0] == 1 and x_sample.shape[1] == 1

    alr0 = 2 * GLA_HEADS * GLA_DK + 2 * GLA_HEADS * GLA_DV
    w_bf = w_in[0].astype(BF16)
    w_lo = w_bf
    w_hi = w_bf[:, alr0 + GLA_RANK:]
    w_alr = jnp.pad(w_bf[:, alr0:alr0 + GLA_RANK], ((0, 0), (0, LANES - GLA_RANK)))
    wup = jnp.pad(w_alpha_up[0], ((0, LANES - GLA_RANK), (0, 0))).astype(BF16)
    ba = b_alpha[0][None, :]
    g1 = norm1_g[0][None, :]
    wts = dict(gg=gla_norm_g[0][None, :], cw=conv_w[0], cbias=conv_b[0][None, :],
               g2=norm2_g[0][None, :], keys=sub_keys[0],
               gf=norm_f_g[None, :])

    xp = x_prompt.reshape(batch * seq, D_MODEL)
    p32, p16, alr, wts["u"], wts["v"], wts["wo"], wts["wq"] = _inproj(
        xp, g1, w_lo, w_hi, w_alr, tm=TM_INPROJ, tn=TN_INPROJ,
        tables=(expert_u[0], expert_v[0], w_out[0], w_query[0]))
    o, s_p = _gla_prompt(p32, alr, wup, ba, batch, seq, rows=GLA_ROWS)
    y_p, utail = _channel_mixer(o, p32, p16, p32, p32, xp, wts, tm_mix=TM_MIX,
                                tm_route=TM_ROUTE, tm_peer=TM_PEER, te=TE_PEER,
                                prompt=True, seq=seq)
    last = utail.reshape(batch, seq // TM_MIX, SUBLANES, D_MODEL)[:, -1, SUBLANES - 2:, :]

    xs = x_sample.reshape(n_dec, D_MODEL)
    ps32, ps16, alrs = _inproj(xs, g1, w_lo, w_hi, w_alr, tm=n_dec, tn=TN_INPROJ)
    os_, s_s = _gla_sample(ps32.reshape(n_dec, 1, N_MAIN // 2),
                           alrs.reshape(n_dec, 1, LANES), wup, ba, state_gla[0],
                           nb=GLA_SAMPLE_SEQS)
    y_s, us = _channel_mixer(os_.reshape(n_dec, D_MODEL), ps32, ps16, state_conv[0][:, 0, :],
                             state_conv[0][:, 1, :], xs, wts, tm_mix=n_dec, tm_route=n_dec,
                             tm_peer=n_dec, te=TE_PEER, prompt=False, seq=1)
    conv_s = jnp.stack([state_conv[0][:, 1, :], us], axis=1)

    return (y_p.reshape(batch, seq, D_MODEL),
            y_s.reshape(n_dec, 1, D_MODEL),
            s_p[None],
            last[None],
            s_s[None],
            conv_s[None])
```

```python
import functools
import math

import jax
import jax.numpy as jnp
from jax import lax
from jax.experimental import pallas as pl
from jax.experimental.pallas import tpu as pltpu

F32 = jnp.float32
BF16 = jnp.bfloat16

D_MODEL = 2048
GLA_HEADS = 4
GLA_DK = 256
GLA_DV = 512
GLA_RANK = 16
GLA_TAU = 16.0
GLA_CHUNK = 64
CONV_WIDTH = 3
PEER_HEADS = 8
N_KEYS = 128
N_EXPERTS = N_KEYS * N_KEYS
PEER_TOPK = 16
DQ_HALF = 128
EPS = 1e-6

LANES = 128
SUBLANES = 8
BF16_ROWS = 16
VMEM_LIMIT_BYTES = 60000 * 1024

N_MAIN = 16384
P_SEGMENT = 4096
GLA_SUB = 8
GLA_HEADS_PER_STEP = 4
PEER_PART_ROWS = 512

TM_INPROJ = 1024
TN_INPROJ = 1024
GLA_ROWS = 512
GLA_SAMPLE_SEQS = 4
TM_MIX = 256
TM_ROUTE = 256
TM_PEER = 512
TE_PEER = 1024
NEG_INF = float("-inf")
INV_SQRT2 = 1.0 / math.sqrt(2.0)


def _params(semantics):
    return pltpu.CompilerParams(dimension_semantics=semantics,
                                vmem_limit_bytes=VMEM_LIMIT_BYTES)


def _rms(x, g):
    return x * lax.rsqrt(jnp.mean(x * x, axis=-1, keepdims=True) + EPS) * g


def _inproj_kernel(*refs, n_lo, tiles_per_seg, table_scales):
    n_tables = len(table_scales)
    x_ref, g_ref, wlo_ref, whi_ref, walr_ref = refs[:5]
    tables_in = refs[5:5 + n_tables]
    p32_ref, p16_ref, alr_ref = refs[5 + n_tables:8 + n_tables]
    tables_out = refs[8 + n_tables:8 + 2 * n_tables]
    h_ref = refs[-1]
    j = pl.program_id(1)

    @pl.when(j == 0)
    def _():
        hb = _rms(x_ref[...], g_ref[...]).astype(BF16)
        h_ref[...] = hb
        alr_ref[...] = jnp.dot(hb, walr_ref[...], preferred_element_type=F32)

    wide = (j // tiles_per_seg) % 2 == 0
    low = j < n_lo
    for w_ref, from_low in ((wlo_ref, True), (whi_ref, False)):
        for out_ref, is_wide in ((p32_ref, True), (p16_ref, False)):
            @pl.when((low == from_low) & (wide == is_wide))
            def _():
                out_ref[...] = jnp.dot(h_ref[...], w_ref[...],
                                       preferred_element_type=F32).astype(out_ref.dtype)

    for src_ref, dst_ref, scale in zip(tables_in, tables_out, table_scales):
        val = src_ref[...]
        dst_ref[...] = (val if scale == 1.0 else val * scale).astype(BF16)


def _inproj(x, g, w_lo, w_hi, w_alr, tm, tn, tables=(), table_scales=()):
    t = x.shape[0]
    n_lo = (N_MAIN - w_hi.shape[1]) // tn
    per = P_SEGMENT // tn
    nj = N_MAIN // tn
    steps = (t // tm) * nj

    def idx32(j):
        seg, r = j // per, j % per
        return jnp.where(seg % 2 == 0, (seg // 2) * per + r, (seg // 2 + 1) * per - 1)

    def idx16(j):
        seg, r = j // per, j % per
        return jnp.where(seg % 2 == 1, (seg // 2) * per + r,
                         jnp.maximum((seg // 2) * per - 1, 0))

    slabs = [pl.BlockSpec((tb.shape[0] // steps, D_MODEL), lambda i, j: (i * nj + j, 0))
             for tb in tables]
    return pl.pallas_call(
        functools.partial(_inproj_kernel, n_lo=n_lo, tiles_per_seg=per,
                          table_scales=tuple(table_scales)),
        grid=(t // tm, nj),
        in_specs=[
            pl.BlockSpec((tm, D_MODEL), lambda i, j: (i, 0), pipeline_mode=pl.Buffered(1)),
            pl.BlockSpec((1, D_MODEL), lambda i, j: (0, 0)),
            pl.BlockSpec((D_MODEL, tn), lambda i, j: (0, jnp.minimum(j, n_lo - 1))),
            pl.BlockSpec((D_MODEL, tn), lambda i, j: (0, jnp.maximum(j - n_lo, 0))),
            pl.BlockSpec((D_MODEL, LANES), lambda i, j: (0, 0)),
        ] + slabs,
        out_specs=[
            pl.BlockSpec((tm, tn), lambda i, j: (i, idx32(j))),
            pl.BlockSpec((tm, tn), lambda i, j: (i, idx16(j))),
            pl.BlockSpec((tm, LANES), lambda i, j: (i, 0)),
        ] + slabs,
        out_shape=[
            jax.ShapeDtypeStruct((t, N_MAIN // 2), F32),
            jax.ShapeDtypeStruct((t, N_MAIN // 2), BF16),
            jax.ShapeDtypeStruct((t, LANES), F32),
        ] + [jax.ShapeDtypeStruct(tb.shape, BF16) for tb in tables],
        scratch_shapes=[pltpu.VMEM((tm, D_MODEL), BF16)],
        compiler_params=_params(("arbitrary", "arbitrary")),
        name="inproj",
    )(x, g, w_lo, w_hi, w_alr, *tables)


def _log_decay(alr, wup, ba):
    xa = jnp.dot(alr.astype(BF16), wup, preferred_element_type=F32) + ba
    return -(jnp.maximum(-xa, 0.0) + jnp.log1p(jnp.exp(-jnp.abs(xa)))) * (1.0 / GLA_TAU)


def _col_scale(row, width):
    col = jnp.transpose(jnp.broadcast_to(row, (LANES, row.shape[1])))
    return jnp.concatenate([col] * (width // LANES), axis=1)


def _gla_prompt_kernel(q_ref, k_ref, v_ref, alr_ref, wup_ref, ba_ref,
                       o_ref, sout_ref, s_ref, b_ref, *, rows):
    c = pl.program_id(2)

    @pl.when(c == 0)
    def _():
        s_ref[...] = jnp.zeros_like(s_ref)

    g = _log_decay(alr_ref[...], wup_ref[...], ba_ref[...])
    rmod = lax.broadcasted_iota(jnp.int32, g.shape, 0) & (GLA_CHUNK - 1)
    step = 1
    while step < GLA_CHUNK:
        g = g + jnp.where(rmod >= step, pltpu.roll(g, step, 0), 0.0)
        step *= 2
    b_ref[...] = g

    row = lax.broadcasted_iota(jnp.int32, (GLA_CHUNK, GLA_DK), 0)
    rsub = row & (GLA_SUB - 1)
    ri = lax.broadcasted_iota(jnp.int32, (GLA_CHUNK, GLA_CHUNK), 0)
    cj = lax.broadcasted_iota(jnp.int32, (GLA_CHUNK, GLA_CHUNK), 1)
    n_sub = GLA_CHUNK // GLA_SUB

    def head_chunk(r0, hh):
        dk = slice(hh * GLA_DK, (hh + 1) * GLA_DK)
        dv = slice(hh * GLA_DV, (hh + 1) * GLA_DV)
        qc = q_ref[pl.ds(r0, GLA_CHUNK), dk] * (GLA_DK ** -0.5)
        kc = k_ref[pl.ds(r0, GLA_CHUNK), dk]
        vc = v_ref[pl.ds(r0, GLA_CHUNK), dv]
        b = b_ref[pl.ds(r0, GLA_CHUNK), dk]
        bl = b[GLA_CHUNK - 1:GLA_CHUNK, :]
        s = s_ref[hh]
        vb = vc.astype(BF16)

        o = jnp.dot((qc * jnp.exp(b)).astype(BF16), s.astype(BF16),
                    preferred_element_type=F32)

        parts = [jnp.zeros((GLA_SUB, GLA_CHUNK), F32)]
        for blk in range(1, n_sub):
            lo = blk * GLA_SUB
            ref = b[lo - 1:lo, :]
            qs = qc[lo:lo + GLA_SUB, :] * jnp.exp(b[lo:lo + GLA_SUB, :] - ref)
            ks = kc * jnp.exp(jnp.where(row < lo, ref - b, NEG_INF))
            parts.append(lax.dot_general(qs.astype(BF16), ks.astype(BF16),
                                         (((1,), (1,)), ((), ())),
                                         preferred_element_type=F32))
        attn = jnp.concatenate(parts, axis=0)

        for d in range(GLA_SUB):
            if d == 0:
                t = qc * kc
            else:
                kd = pltpu.roll(kc, d, 0)
                bd = pltpu.roll(b, d, 0)
                t = qc * kd * jnp.exp(jnp.where(rsub >= d, b - bd, NEG_INF))
            cs = jnp.sum(t, axis=1, keepdims=True)
            attn = attn + jnp.where(ri - cj == d, cs, 0.0)

        o = o + jnp.dot(attn.astype(BF16), vb, preferred_element_type=F32)
        o_ref[pl.ds(r0, GLA_CHUNK), dv] = o

        kdec = (kc * jnp.exp(bl - b)).astype(BF16)
        s_ref[hh] = (_col_scale(jnp.exp(bl), GLA_DV) * s
                     + lax.dot_general(kdec, vb, (((0,), (0,)), ((), ())),
                                       preferred_element_type=F32))

    def chunk(ci, carry):
        r0 = pl.multiple_of(ci * GLA_CHUNK, GLA_CHUNK)
        for hh in range(GLA_HEADS_PER_STEP):
            head_chunk(r0, hh)
        return carry

    lax.fori_loop(0, rows // GLA_CHUNK, chunk, 0)

    @pl.when(c == pl.num_programs(2) - 1)
    def _():
        sout_ref[0] = s_ref[...]


def _gla_prompt(p, alr, wup, ba, batch, seq, rows):
    nsteps = seq // rows
    hps = GLA_HEADS_PER_STEP
    wk = hps * GLA_DK
    wv = hps * GLA_DV
    kq = GLA_HEADS * GLA_DK // wk
    kv = 2 * GLA_HEADS * GLA_DK // wv

    def rowmap(b, h, c):
        return b * nsteps + c

    return pl.pallas_call(
        functools.partial(_gla_prompt_kernel, rows=rows),
        grid=(batch, GLA_HEADS // hps, nsteps),
        in_specs=[
            pl.BlockSpec((rows, wk), lambda b, h, c: (rowmap(b, h, c), h)),
            pl.BlockSpec((rows, wk), lambda b, h, c: (rowmap(b, h, c), kq + h)),
            pl.BlockSpec((rows, wv), lambda b, h, c: (rowmap(b, h, c), kv + h)),
            pl.BlockSpec((rows, LANES), lambda b, h, c: (rowmap(b, h, c), 0)),
            pl.BlockSpec((LANES, wk), lambda b, h, c: (0, h)),
            pl.BlockSpec((1, wk), lambda b, h, c: (0, h)),
        ],
        out_specs=[
            pl.BlockSpec((rows, wv), lambda b, h, c: (rowmap(b, h, c), h)),
            pl.BlockSpec((1, hps, GLA_DK, GLA_DV), lambda b, h, c: (b, h, 0, 0)),
        ],
        out_shape=[
            jax.ShapeDtypeStruct((batch * seq, GLA_HEADS * GLA_DV), F32),
            jax.ShapeDtypeStruct((batch, GLA_HEADS, GLA_DK, GLA_DV), F32),
        ],
        scratch_shapes=[pltpu.VMEM((hps, GLA_DK, GLA_DV), F32),
                        pltpu.VMEM((rows, wk), F32)],
        compiler_params=_params(("arbitrary", "arbitrary", "arbitrary")),
        name="gla_prompt",
    )(p, p, p, alr, wup, ba)


def _gla_sample_kernel(q_ref, k_ref, v_ref, alr_ref, wup_ref, ba_ref, s0_ref,
                       o_ref, s1_ref):
    first = lax.broadcasted_iota(jnp.int32, (SUBLANES, GLA_DK), 0) == 0
    for bb in range(q_ref.shape[0]):
        g_all = _log_decay(jnp.broadcast_to(alr_ref[bb], (SUBLANES, LANES)),
                           wup_ref[...], ba_ref[...])
        for h in range(GLA_HEADS):
            dk = slice(h * GLA_DK, (h + 1) * GLA_DK)
            dv = slice(h * GLA_DV, (h + 1) * GLA_DV)
            q = jnp.broadcast_to(q_ref[bb, :, dk], (SUBLANES, GLA_DK)) * (GLA_DK ** -0.5)
            k = jnp.broadcast_to(k_ref[bb, :, dk], (SUBLANES, GLA_DK))
            v = jnp.broadcast_to(v_ref[bb, :, dv], (SUBLANES, GLA_DV))
            g = g_all[:, dk]
            s = s0_ref[bb, h]
            attn = jnp.sum(q * k, axis=1, keepdims=True)
            o = attn * v + jnp.dot((q * jnp.exp(g)).astype(BF16), s.astype(BF16),
                                   preferred_element_type=F32)
            o_ref[bb, :, dv] = o[0:1, :]
            k1 = jnp.where(first, k, 0.0).astype(BF16)
            s1_ref[bb, h] = (_col_scale(jnp.exp(g[0:1, :]), GLA_DV) * s
                             + lax.dot_general(k1, v.astype(BF16), (((0,), (0,)), ((), ())),
                                               preferred_element_type=F32))


def _gla_sample(p3, alr3, wup, ba, s0, nb):
    n = p3.shape[0]
    qk = GLA_HEADS * GLA_DK
    vw = GLA_HEADS * GLA_DV
    return pl.pallas_call(
        _gla_sample_kernel,
        grid=(n // nb,),
        in_specs=[
            pl.BlockSpec((nb, 1, qk), lambda b: (b, 0, 0)),
            pl.BlockSpec((nb, 1, qk), lambda b: (b, 0, 1)),
            pl.BlockSpec((nb, 1, vw), lambda b: (b, 0, 2 * qk // vw)),
            pl.BlockSpec((nb, 1, LANES), lambda b: (b, 0, 0)),
            pl.BlockSpec((LANES, qk), lambda b: (0, 0)),
            pl.BlockSpec((1, qk), lambda b: (0, 0)),
            pl.BlockSpec((nb, GLA_HEADS, GLA_DK, GLA_DV), lambda b: (b, 0, 0, 0)),
        ],
        out_specs=[
            pl.BlockSpec((nb, 1, vw), lambda b: (b, 0, 0)),
            pl.BlockSpec((nb, GLA_HEADS, GLA_DK, GLA_DV), lambda b: (b, 0, 0, 0)),
        ],
        out_shape=[
            jax.ShapeDtypeStruct((n, 1, vw), F32),
            jax.ShapeDtypeStruct((n, GLA_HEADS, GLA_DK, GLA_DV), F32),
        ],
        compiler_params=_params(("arbitrary",)),
        name="gla_sample",
    )(p3, p3, p3, alr3, wup, ba, s0)


def _mix_kernel(o_ref, r_ref, cb_ref, cc_ref, cx_ref, ga_ref, gb_ref, pa_ref, pb_ref,
                gg_ref, cw_ref, cbias_ref, x_ref, wo_ref, g2_ref, wq_ref,
                xm_ref, h2t_ref, qp_ref, u_ref, m_ref, *, prompt, tiles_per_seq):
    tm = x_ref.shape[0]
    if prompt:
        start = (pl.program_id(0) % tiles_per_seq) == 0
        row = lax.broadcasted_iota(jnp.int32, (tm, GLA_DV), 0)
    for h in range(GLA_HEADS):
        dv = slice(h * GLA_DV, (h + 1) * GLA_DV)
        u = cc_ref[:, dv] * cx_ref[:, dv]
        if prompt:
            prev = jnp.where(start, 0.0, pa_ref[:, dv] * pb_ref[:, dv])
            um1 = jnp.where(row == 0, prev[7:8, :], pltpu.roll(u, 1, 0))
            um2 = jnp.where(row == 0, prev[6:7, :],
                            jnp.where(row == 1, prev[7:8, :], pltpu.roll(u, 2, 0)))
            u_ref[0, :, dv] = u[tm - SUBLANES:, :]
        else:
            um2 = pa_ref[:, dv]
            um1 = pb_ref[:, dv]
            u_ref[:, dv] = u
        z = (cbias_ref[:, dv] + um2 * cw_ref[0:1, dv] + um1 * cw_ref[1:2, dv]
             + u * cw_ref[2:3, dv])
        y_b = cb_ref[:, dv].astype(F32) * z
        r = r_ref[:, dv].astype(F32)
        on = _rms(o_ref[:, dv], gg_ref[:, dv])
        m = (jax.nn.sigmoid(gb_ref[:, dv].astype(F32)) * y_b
             + jax.nn.sigmoid(ga_ref[:, dv].astype(F32)) * (r * jax.nn.sigmoid(r) * on))
        m_ref[:, dv] = m.astype(BF16)
    xm = x_ref[...] + jnp.dot(m_ref[...], wo_ref[...], preferred_element_type=F32)
    xm_ref[...] = xm
    h2 = _rms(xm, g2_ref[...])
    h2t_ref[...] = jnp.transpose(h2).astype(BF16)
    qp_ref[...] = jnp.dot(h2.astype(BF16), wq_ref[...], preferred_element_type=F32)


def _mix(o, p32, p16, pa, pb, x, wts, tm, prompt, seq):
    t = o.shape[0]
    n_tiles = t // tm
    tile = pl.BlockSpec((tm, D_MODEL), lambda i: (i, 0))
    col = lambda c: pl.BlockSpec((tm, D_MODEL), lambda i: (i, c))
    cc_col = P_SEGMENT // D_MODEL
    if prompt:
        prev_rows = tm // SUBLANES
        prev = lambda c: pl.BlockSpec(
            (SUBLANES, D_MODEL), lambda i: (jnp.maximum(i * prev_rows - 1, 0), c))
        pspecs = [prev(cc_col), prev(cc_col + 1)]
        uspec = pl.BlockSpec((1, SUBLANES, D_MODEL), lambda i: (i, 0, 0))
        ushape = jax.ShapeDtypeStruct((n_tiles, SUBLANES, D_MODEL), F32)
    else:
        pspecs = [tile, tile]
        uspec = tile
        ushape = jax.ShapeDtypeStruct((t, D_MODEL), F32)
    vec = lambda r: pl.BlockSpec((r, D_MODEL), lambda i: (0, 0))
    whole = pl.BlockSpec((D_MODEL, D_MODEL), lambda i: (0, 0),
                         pipeline_mode=pl.Buffered(1))
    return pl.pallas_call(
        functools.partial(_mix_kernel, prompt=prompt, tiles_per_seq=max(seq // tm, 1)),
        grid=(n_tiles,),
        in_specs=[tile, col(0), col(1), col(cc_col), col(cc_col + 1), col(2), col(3)]
                 + pspecs + [vec(1), vec(CONV_WIDTH), vec(1), tile, whole, vec(1), whole],
        out_specs=[tile, pl.BlockSpec((D_MODEL, tm), lambda i: (0, i)), tile, uspec],
        out_shape=[jax.ShapeDtypeStruct((t, D_MODEL), F32),
                   jax.ShapeDtypeStruct((D_MODEL, t), BF16),
                   jax.ShapeDtypeStruct((t, D_MODEL), F32),
                   ushape],
        scratch_shapes=[pltpu.VMEM((tm, D_MODEL), BF16)],
        compiler_params=_params(("arbitrary",)),
        name="mix_prompt" if prompt else "mix_sample",
    )(o, p16, p16, p32, p32, p16, p16, pa, pb, wts["gg"], wts["cw"], wts["cbias"],
      x, wts["wo"], wts["g2"], wts["wq"])


def _top_values(work, n, with_rank=False):
    vals = []
    rank = jnp.full(work.shape, float(n), F32) if with_rank else None
    for r in range(n):
        mx = jnp.max(work, axis=0, keepdims=True)
        hit = work == mx
        vals.append(mx)
        if with_rank:
            rank = jnp.where(hit, float(r), rank)
        work = jnp.where(hit, NEG_INF, work)
    return vals, rank


def _route_kernel(qp_ref, keys_ref, n1_ref, p1_ref, rank2_ref, p2_ref):
    for h in range(PEER_HEADS):
        st = []
        for side in range(2):
            c0 = (2 * h + side) * DQ_HALF
            st.append(lax.dot_general(keys_ref[side], qp_ref[:, c0:c0 + DQ_HALF],
                                      (((1,), (1,)), ((), ())),
                                      precision=lax.Precision.HIGHEST,
                                      preferred_element_type=F32))
        v1, _ = _top_values(st[0], PEER_TOPK)
        v2, rank2 = _top_values(st[1], PEER_TOPK, with_rank=True)
        v2 = jnp.concatenate(v2, axis=0)
        tm = v2.shape[1]
        bidx = lax.broadcasted_iota(jnp.int32, (SUBLANES, tm), 0)
        cand = []
        for a in range(PEER_TOPK // 2):
            nb = PEER_TOPK // (a + 1)
            if nb >= SUBLANES:
                cand.append(v1[a] + v2[0:nb, :])
            else:
                cand.append(jnp.where(bidx < nb, v1[a] + v2[0:SUBLANES, :], NEG_INF))
        tail = jnp.concatenate(v1[PEER_TOPK // 2:], axis=0) + v2[0:1, :]
        cand.append(tail)
        thr = _top_values(jnp.concatenate(cand, axis=0), PEER_TOPK)[0][-1]
        m1 = v1[0]
        m2 = v2[0:1, :]
        z = jnp.zeros_like(m1)
        n1 = jnp.zeros_like(st[0])
        for a, ca in enumerate(cand):
            keep = ca >= thr
            z = z + jnp.sum(jnp.where(keep, jnp.exp(ca - (m1 + m2)), 0.0),
                            axis=0, keepdims=True)
            kept = jnp.where(keep, 1.0, 0.0)
            if a < PEER_TOPK // 2:
                n1 = jnp.where(st[0] == v1[a], jnp.sum(kept, axis=0, keepdims=True), n1)
            else:
                for r in range(PEER_TOPK // 2):
                    n1 = jnp.where(st[0] == v1[a + r], kept[r:r + 1, :], n1)
        n1_ref[h] = n1
        p1_ref[h] = jnp.exp(st[0] - m1)
        rank2_ref[h] = rank2.astype(BF16)
        p2_ref[h] = (INV_SQRT2 * jnp.exp(st[1] - m2) / z).astype(BF16)


def _route(qp, keys, tm):
    t = qp.shape[0]
    big = pl.BlockSpec((PEER_HEADS, N_KEYS, tm), lambda i: (0, 0, i))
    f32s = jax.ShapeDtypeStruct((PEER_HEADS, N_KEYS, t), F32)
    bf16s = jax.ShapeDtypeStruct((PEER_HEADS, N_KEYS, t), BF16)
    return pl.pallas_call(
        _route_kernel,
        grid=(t // tm,),
        in_specs=[pl.BlockSpec((tm, D_MODEL), lambda i: (i, 0)),
                  pl.BlockSpec((2, N_KEYS, DQ_HALF), lambda i: (0, 0, 0))],
        out_specs=[big, big, big, big],
        out_shape=[f32s, f32s, bf16s, bf16s],
        compiler_params=_params(("arbitrary",)),
        name="route",
    )(qp, keys)


def _row_bf16(ref, h, e1, tm):
    row = jnp.broadcast_to(ref[h, pl.ds(e1, 1), :], (BF16_ROWS, tm)).astype(BF16)
    return jnp.concatenate([row] * (N_KEYS // BF16_ROWS), axis=0)


def _peer_kernel(h2t_ref, u_ref, v_ref, n1_ref, p1_ref, rank2_ref, p2_ref,
                 xm_ref, gf_ref, out_ref, *, te):
    j = pl.program_id(1)

    @pl.when(j == 0)
    def _():
        out_ref[...] = jnp.zeros_like(out_ref)

    tm = h2t_ref.shape[1]
    rows_per_part = PEER_PART_ROWS
    acc = None
    for part in range(te // rows_per_part):
        lo = part * rows_per_part
        a = jnp.dot(u_ref[lo:lo + rows_per_part, :], h2t_ref[...],
                    preferred_element_type=F32)
        blocks = []
        for l in range(rows_per_part // N_KEYS):
            e1 = j * (te // N_KEYS) + part * (rows_per_part // N_KEYS) + l
            g = jnp.zeros((N_KEYS, tm), BF16)
            for h in range(PEER_HEADS):
                keep = rank2_ref[h] < _row_bf16(n1_ref, h, e1, tm)
                g = g + jnp.where(keep, p2_ref[h], 0.0) * _row_bf16(p1_ref, h, e1, tm)
            al = a[l * N_KEYS:(l + 1) * N_KEYS, :]
            gelu_s = al * (1.0 + lax.erf(al))
            blocks.append(gelu_s.astype(BF16) * g)
        wt = jnp.concatenate(blocks, axis=0) if len(blocks) > 1 else blocks[0]
        d = jnp.dot(jnp.transpose(wt), v_ref[lo:lo + rows_per_part, :],
                    preferred_element_type=F32)
        acc = d if acc is None else acc + d
    out_ref[...] += acc

    @pl.when(j == pl.num_programs(1) - 1)
    def _():
        out_ref[...] = _rms(xm_ref[...] + out_ref[...], gf_ref[...])


def _peer(h2t, u_bf, v_bf, n1, p1, rank2, p2, xm, gf, tm, te):
    t = xm.shape[0]
    big = pl.BlockSpec((PEER_HEADS, N_KEYS, tm), lambda i, j: (0, 0, i))
    return pl.pallas_call(
        functools.partial(_peer_kernel, te=te),
        grid=(t // tm, N_EXPERTS // te),
        in_specs=[pl.BlockSpec((D_MODEL, tm), lambda i, j: (0, i)),
                  pl.BlockSpec((te, D_MODEL), lambda i, j: (j, 0)),
                  pl.BlockSpec((te, D_MODEL), lambda i, j: (j, 0)),
                  big, big, big, big,
                  pl.BlockSpec((tm, D_MODEL), lambda i, j: (i, 0)),
                  pl.BlockSpec((1, D_MODEL), lambda i, j: (0, 0))],
        out_specs=pl.BlockSpec((tm, D_MODEL), lambda i, j: (i, 0)),
        out_shape=jax.ShapeDtypeStruct((t, D_MODEL), F32),
        compiler_params=_params(("arbitrary", "arbitrary")),
        name="peer",
    )(h2t, u_bf, v_bf, n1, p1, rank2, p2, xm, gf)


def _channel_mixer(o, p32, p16, pa, pb, x, wts, tm_mix, tm_route, tm_peer, te, prompt, seq):
    xm, h2t, qp, utail = _mix(o, p32, p16, pa, pb, x, wts, tm_mix, prompt, seq)
    n1, p1, rank2, p2 = _route(qp, wts["keys"], tm_route)
    y = _peer(h2t, wts["u"], wts["v"], n1, p1, rank2, p2, xm, wts["gf"], tm_peer, te)
    return y, utail


def kernel(x_prompt, x_sample, state_gla, state_conv, norm1_g, w_in, w_alpha_up, b_alpha,
           gla_norm_g, conv_w, conv_b, w_out, norm2_g, w_query, sub_keys, expert_u, expert_v,
           norm_f_g):
    batch, seq, _ = x_prompt.shape
    n_dec = x_sample.shape[0]
    assert w_in.shape[0] == 1 and x_sample.shape[1] == 1

    alr0 = 2 * GLA_HEADS * GLA_DK + 2 * GLA_HEADS * GLA_DV
    w_bf = w_in[0].astype(BF16)
    w_lo = w_bf
    w_hi = w_bf[:, alr0 + GLA_RANK:]
    w_alr = jnp.pad(w_bf[:, alr0:alr0 + GLA_RANK], ((0, 0), (0, LANES - GLA_RANK)))
    wup = jnp.pad(w_alpha_up[0], ((0, LANES - GLA_RANK), (0, 0))).astype(BF16)
    ba = b_alpha[0][None, :]
    g1 = norm1_g[0][None, :]
    wts = dict(gg=gla_norm_g[0][None, :], cw=conv_w[0], cbias=conv_b[0][None, :],
               g2=norm2_g[0][None, :], keys=sub_keys[0],
               gf=norm_f_g[None, :])

    xp = x_prompt.reshape(batch * seq, D_MODEL)
    p32, p16, alr, wts["u"], wts["v"], wts["wo"], wts["wq"] = _inproj(
        xp, g1, w_lo, w_hi, w_alr, tm=TM_INPROJ, tn=TN_INPROJ,
        tables=(expert_u[0], expert_v[0], w_out[0], w_query[0]),
        table_scales=(INV_SQRT2, 1.0, 1.0, 1.0))
    o, s_p = _gla_prompt(p32, alr, wup, ba, batch, seq, rows=GLA_ROWS)
    y_p, utail = _channel_mixer(o, p32, p16, p32, p32, xp, wts, tm_mix=TM_MIX,
                                tm_route=TM_ROUTE, tm_peer=TM_PEER, te=TE_PEER,
                                prompt=True, seq=seq)
    last = utail.reshape(batch, seq // TM_MIX, SUBLANES, D_MODEL)[:, -1, SUBLANES - 2:, :]

    xs = x_sample.reshape(n_dec, D_MODEL)
    ps32, ps16, alrs = _inproj(xs, g1, w_lo, w_hi, w_alr, tm=n_dec, tn=TN_INPROJ)
    os_, s_s = _gla_sample(ps32.reshape(n_dec, 1, N_MAIN // 2),
                           alrs.reshape(n_dec, 1, LANES), wup, ba, state_gla[0],
                           nb=GLA_SAMPLE_SEQS)
    y_s, us = _channel_mixer(os_.reshape(n_dec, D_MODEL), ps32, ps16, state_conv[0][:, 0, :],
                             state_conv[0][:, 1, :], xs, wts, tm_mix=n_dec, tm_route=n_dec,
                             tm_peer=n_dec, te=TE_PEER, prompt=False, seq=1)
    conv_s = jnp.stack([state_conv[0][:, 1, :], us], axis=1)

    return (y_p.reshape(batch, seq, D_MODEL),
            y_s.reshape(n_dec, 1, D_MODEL),
            s_p[None],
            last[None],
            s_s[None],
            conv_s[None])
```

```python
import functools
import math

import jax
import jax.numpy as jnp
from jax import lax
from jax.experimental import pallas as pl
from jax.experimental.pallas import tpu as pltpu

F32 = jnp.float32
BF16 = jnp.bfloat16

D_MODEL = 2048
GLA_HEADS = 4
GLA_DK = 256
GLA_DV = 512
GLA_RANK = 16
GLA_TAU = 16.0
GLA_CHUNK = 64
CONV_WIDTH = 3
PEER_HEADS = 8
N_KEYS = 128
N_EXPERTS = N_KEYS * N_KEYS
PEER_TOPK = 16
DQ_HALF = 128
EPS = 1e-6

LANES = 128
SUBLANES = 8
BF16_ROWS = 16
VMEM_LIMIT_BYTES = 60000 * 1024

N_MAIN = 16384
P_SEGMENT = 4096
GLA_SUB = 4
GLA_HEADS_PER_STEP = 4
PEER_PART_ROWS = 512

TM_INPROJ = 1024
TN_INPROJ = 1024
GLA_ROWS = 512
GLA_SAMPLE_SEQS = 4
TM_MIX = 256
TM_ROUTE = 256
TM_PEER = 512
TE_PEER = 1024
NEG_INF = float("-inf")
INV_SQRT2 = 1.0 / math.sqrt(2.0)


def _params(semantics):
    return pltpu.CompilerParams(dimension_semantics=semantics,
                                vmem_limit_bytes=VMEM_LIMIT_BYTES)


def _rms(x, g):
    return x * lax.rsqrt(jnp.mean(x * x, axis=-1, keepdims=True) + EPS) * g


def _inproj_kernel(*refs, n_lo, tiles_per_seg, table_scales):
    n_tables = len(table_scales)
    x_ref, g_ref, wlo_ref, whi_ref, walr_ref = refs[:5]
    tables_in = refs[5:5 + n_tables]
    p32_ref, p16_ref, alr_ref = refs[5 + n_tables:8 + n_tables]
    tables_out = refs[8 + n_tables:8 + 2 * n_tables]
    h_ref = refs[-1]
    j = pl.program_id(1)

    @pl.when(j == 0)
    def _():
        hb = _rms(x_ref[...], g_ref[...]).astype(BF16)
        h_ref[...] = hb
        alr_ref[...] = jnp.dot(hb, walr_ref[...], preferred_element_type=F32)

    wide = (j // tiles_per_seg) % 2 == 0
    low = j < n_lo
    for w_ref, from_low in ((wlo_ref, True), (whi_ref, False)):
        for out_ref, is_wide in ((p32_ref, True), (p16_ref, False)):
            @pl.when((low == from_low) & (wide == is_wide))
            def _():
                out_ref[...] = jnp.dot(h_ref[...], w_ref[...],
                                       preferred_element_type=F32).astype(out_ref.dtype)

    for src_ref, dst_ref, scale in zip(tables_in, tables_out, table_scales):
        val = src_ref[...]
        dst_ref[...] = (val if scale == 1.0 else val * scale).astype(BF16)


def _inproj(x, g, w_lo, w_hi, w_alr, tm, tn, tables=(), table_scales=()):
    t = x.shape[0]
    n_lo = (N_MAIN - w_hi.shape[1]) // tn
    per = P_SEGMENT // tn
    nj = N_MAIN // tn
    steps = (t // tm) * nj

    def idx32(j):
        seg, r = j // per, j % per
        return jnp.where(seg % 2 == 0, (seg // 2) * per + r, (seg // 2 + 1) * per - 1)

    def idx16(j):
        seg, r = j // per, j % per
        return jnp.where(seg % 2 == 1, (seg // 2) * per + r,
                         jnp.maximum((seg // 2) * per - 1, 0))

    slabs = [pl.BlockSpec((tb.shape[0] // steps, D_MODEL), lambda i, j: (i * nj + j, 0))
             for tb in tables]
    return pl.pallas_call(
        functools.partial(_inproj_kernel, n_lo=n_lo, tiles_per_seg=per,
                          table_scales=tuple(table_scales)),
        grid=(t // tm, nj),
        in_specs=[
            pl.BlockSpec((tm, D_MODEL), lambda i, j: (i, 0), pipeline_mode=pl.Buffered(1)),
            pl.BlockSpec((1, D_MODEL), lambda i, j: (0, 0)),
            pl.BlockSpec((D_MODEL, tn), lambda i, j: (0, jnp.minimum(j, n_lo - 1))),
            pl.BlockSpec((D_MODEL, tn), lambda i, j: (0, jnp.maximum(j - n_lo, 0))),
            pl.BlockSpec((D_MODEL, LANES), lambda i, j: (0, 0)),
        ] + slabs,
        out_specs=[
            pl.BlockSpec((tm, tn), lambda i, j: (i, idx32(j))),
            pl.BlockSpec((tm, tn), lambda i, j: (i, idx16(j))),
            pl.BlockSpec((tm, LANES), lambda i, j: (i, 0)),
        ] + slabs,
        out_shape=[
            jax.ShapeDtypeStruct((t, N_MAIN // 2), F32),
            jax.ShapeDtypeStruct((t, N_MAIN // 2), BF16),
            jax.ShapeDtypeStruct((t, LANES), F32),
        ] + [jax.ShapeDtypeStruct(tb.shape, BF16) for tb in tables],
        scratch_shapes=[pltpu.VMEM((tm, D_MODEL), BF16)],
        compiler_params=_params(("arbitrary", "arbitrary")),
        name="inproj",
    )(x, g, w_lo, w_hi, w_alr, *tables)


def _log_decay(alr, wup, ba):
    xa = jnp.dot(alr.astype(BF16), wup, preferred_element_type=F32) + ba
    return -(jnp.maximum(-xa, 0.0) + jnp.log1p(jnp.exp(-jnp.abs(xa)))) * (1.0 / GLA_TAU)


def _col_scale(row, width):
    col = jnp.transpose(jnp.broadcast_to(row, (LANES, row.shape[1])))
    return jnp.concatenate([col] * (width // LANES), axis=1)


def _gla_prompt_kernel(q_ref, k_ref, v_ref, alr_ref, wup_ref, ba_ref,
                       o_ref, sout_ref, s_ref, b_ref, *, rows):
    c = pl.program_id(2)

    @pl.when(c == 0)
    def _():
        s_ref[...] = jnp.zeros_like(s_ref)

    g = _log_decay(alr_ref[...], wup_ref[...], ba_ref[...])
    rmod = lax.broadcasted_iota(jnp.int32, g.shape, 0) & (GLA_CHUNK - 1)
    step = 1
    while step < GLA_CHUNK:
        g = g + jnp.where(rmod >= step, pltpu.roll(g, step, 0), 0.0)
        step *= 2
    b_ref[...] = g

    row = lax.broadcasted_iota(jnp.int32, (GLA_CHUNK, GLA_DK), 0)
    rsub = row & (GLA_SUB - 1)
    ri = lax.broadcasted_iota(jnp.int32, (GLA_CHUNK, GLA_CHUNK), 0)
    cj = lax.broadcasted_iota(jnp.int32, (GLA_CHUNK, GLA_CHUNK), 1)
    n_sub = GLA_CHUNK // GLA_SUB

    def head_chunk(r0, hh):
        dk = slice(hh * GLA_DK, (hh + 1) * GLA_DK)
        dv = slice(hh * GLA_DV, (hh + 1) * GLA_DV)
        qc = q_ref[pl.ds(r0, GLA_CHUNK), dk] * (GLA_DK ** -0.5)
        kc = k_ref[pl.ds(r0, GLA_CHUNK), dk]
        vc = v_ref[pl.ds(r0, GLA_CHUNK), dv]
        b = b_ref[pl.ds(r0, GLA_CHUNK), dk]
        bl = b[GLA_CHUNK - 1:GLA_CHUNK, :]
        s = s_ref[hh]
        vb = vc.astype(BF16)

        o = jnp.dot((qc * jnp.exp(b)).astype(BF16), s.astype(BF16),
                    preferred_element_type=F32)

        parts = [jnp.zeros((GLA_SUB, GLA_CHUNK), F32)]
        for blk in range(1, n_sub):
            lo = blk * GLA_SUB
            ref = b[lo - 1:lo, :]
            qs = qc[lo:lo + GLA_SUB, :] * jnp.exp(b[lo:lo + GLA_SUB, :] - ref)
            ks = kc * jnp.exp(jnp.where(row < lo, ref - b, NEG_INF))
            parts.append(lax.dot_general(qs.astype(BF16), ks.astype(BF16),
                                         (((1,), (1,)), ((), ())),
                                         preferred_element_type=F32))
        attn = jnp.concatenate(parts, axis=0)

        for d in range(GLA_SUB):
            if d == 0:
                t = qc * kc
            else:
                kd = pltpu.roll(kc, d, 0)
                bd = pltpu.roll(b, d, 0)
                t = qc * kd * jnp.exp(jnp.where(rsub >= d, b - bd, NEG_INF))
            cs = jnp.sum(t, axis=1, keepdims=True)
            attn = attn + jnp.where(ri - cj == d, cs, 0.0)

        o = o + jnp.dot(attn.astype(BF16), vb, preferred_element_type=F32)
        o_ref[pl.ds(r0, GLA_CHUNK), dv] = o

        kdec = (kc * jnp.exp(bl - b)).astype(BF16)
        s_ref[hh] = (_col_scale(jnp.exp(bl), GLA_DV) * s
                     + lax.dot_general(kdec, vb, (((0,), (0,)), ((), ())),
                                       preferred_element_type=F32))

    def chunk(ci, carry):
        r0 = pl.multiple_of(ci * GLA_CHUNK, GLA_CHUNK)
        for hh in range(GLA_HEADS_PER_STEP):
            head_chunk(r0, hh)
        return carry

    lax.fori_loop(0, rows // GLA_CHUNK, chunk, 0)

    @pl.when(c == pl.num_programs(2) - 1)
    def _():
        sout_ref[0] = s_ref[...]


def _gla_prompt(p, alr, wup, ba, batch, seq, rows):
    nsteps = seq // rows
    hps = GLA_HEADS_PER_STEP
    wk = hps * GLA_DK
    wv = hps * GLA_DV
    kq = GLA_HEADS * GLA_DK // wk
    kv = 2 * GLA_HEADS * GLA_DK // wv

    def rowmap(b, h, c):
        return b * nsteps + c

    return pl.pallas_call(
        functools.partial(_gla_prompt_kernel, rows=rows),
        grid=(batch, GLA_HEADS // hps, nsteps),
        in_specs=[
            pl.BlockSpec((rows, wk), lambda b, h, c: (rowmap(b, h, c), h)),
            pl.BlockSpec((rows, wk), lambda b, h, c: (rowmap(b, h, c), kq + h)),
            pl.BlockSpec((rows, wv), lambda b, h, c: (rowmap(b, h, c), kv + h)),
            pl.BlockSpec((rows, LANES), lambda b, h, c: (rowmap(b, h, c), 0)),
            pl.BlockSpec((LANES, wk), lambda b, h, c: (0, h)),
            pl.BlockSpec((1, wk), lambda b, h, c: (0, h)),
        ],
        out_specs=[
            pl.BlockSpec((rows, wv), lambda b, h, c: (rowmap(b, h, c), h)),
            pl.BlockSpec((1, hps, GLA_DK, GLA_DV), lambda b, h, c: (b, h, 0, 0)),
        ],
        out_shape=[
            jax.ShapeDtypeStruct((batch * seq, GLA_HEADS * GLA_DV), F32),
            jax.ShapeDtypeStruct((batch, GLA_HEADS, GLA_DK, GLA_DV), F32),
        ],
        scratch_shapes=[pltpu.VMEM((hps, GLA_DK, GLA_DV), F32),
                        pltpu.VMEM((rows, wk), F32)],
        compiler_params=_params(("arbitrary", "arbitrary", "arbitrary")),
        name="gla_prompt",
    )(p, p, p, alr, wup, ba)


def _gla_sample_kernel(q_ref, k_ref, v_ref, alr_ref, wup_ref, ba_ref, s0_ref,
                       o_ref, s1_ref):
    first = lax.broadcasted_iota(jnp.int32, (SUBLANES, GLA_DK), 0) == 0
    for bb in range(q_ref.shape[0]):
        g_all = _log_decay(jnp.broadcast_to(alr_ref[bb], (SUBLANES, LANES)),
                           wup_ref[...], ba_ref[...])
        for h in range(GLA_HEADS):
            dk = slice(h * GLA_DK, (h + 1) * GLA_DK)
            dv = slice(h * GLA_DV, (h + 1) * GLA_DV)
            q = jnp.broadcast_to(q_ref[bb, :, dk], (SUBLANES, GLA_DK)) * (GLA_DK ** -0.5)
            k = jnp.broadcast_to(k_ref[bb, :, dk], (SUBLANES, GLA_DK))
            v = jnp.broadcast_to(v_ref[bb, :, dv], (SUBLANES, GLA_DV))
            g = g_all[:, dk]
            s = s0_ref[bb, h]
            attn = jnp.sum(q * k, axis=1, keepdims=True)
            o = attn * v + jnp.dot((q * jnp.exp(g)).astype(BF16), s.astype(BF16),
                                   preferred_element_type=F32)
            o_ref[bb, :, dv] = o[0:1, :]
            k1 = jnp.where(first, k, 0.0).astype(BF16)
            s1_ref[bb, h] = (_col_scale(jnp.exp(g[0:1, :]), GLA_DV) * s
                             + lax.dot_general(k1, v.astype(BF16), (((0,), (0,)), ((), ())),
                                               preferred_element_type=F32))


def _gla_sample(p3, alr3, wup, ba, s0, nb):
    n = p3.shape[0]
    qk = GLA_HEADS * GLA_DK
    vw = GLA_HEADS * GLA_DV
    return pl.pallas_call(
        _gla_sample_kernel,
        grid=(n // nb,),
        in_specs=[
            pl.BlockSpec((nb, 1, qk), lambda b: (b, 0, 0)),
            pl.BlockSpec((nb, 1, qk), lambda b: (b, 0, 1)),
            pl.BlockSpec((nb, 1, vw), lambda b: (b, 0, 2 * qk // vw)),
            pl.BlockSpec((nb, 1, LANES), lambda b: (b, 0, 0)),
            pl.BlockSpec((LANES, qk), lambda b: (0, 0)),
            pl.BlockSpec((1, qk), lambda b: (0, 0)),
            pl.BlockSpec((nb, GLA_HEADS, GLA_DK, GLA_DV), lambda b: (b, 0, 0, 0)),
        ],
        out_specs=[
            pl.BlockSpec((nb, 1, vw), lambda b: (b, 0, 0)),
            pl.BlockSpec((nb, GLA_HEADS, GLA_DK, GLA_DV), lambda b: (b, 0, 0, 0)),
        ],
        out_shape=[
            jax.ShapeDtypeStruct((n, 1, vw), F32),
            jax.ShapeDtypeStruct((n, GLA_HEADS, GLA_DK, GLA_DV), F32),
        ],
        compiler_params=_params(("arbitrary",)),
        name="gla_sample",
    )(p3, p3, p3, alr3, wup, ba, s0)


def _mix_kernel(o_ref, r_ref, cb_ref, cc_ref, cx_ref, ga_ref, gb_ref, pa_ref, pb_ref,
                gg_ref, cw_ref, cbias_ref, x_ref, wo_ref, g2_ref, wq_ref,
                xm_ref, h2t_ref, qp_ref, u_ref, m_ref, *, prompt, tiles_per_seq):
    tm = x_ref.shape[0]
    if prompt:
        start = (pl.program_id(0) % tiles_per_seq) == 0
        row = lax.broadcasted_iota(jnp.int32, (tm, GLA_DV), 0)
    for h in range(GLA_HEADS):
        dv = slice(h * GLA_DV, (h + 1) * GLA_DV)
        u = cc_ref[:, dv] * cx_ref[:, dv]
        if prompt:
            prev = jnp.where(start, 0.0, pa_ref[:, dv] * pb_ref[:, dv])
            um1 = jnp.where(row == 0, prev[7:8, :], pltpu.roll(u, 1, 0))
            um2 = jnp.where(row == 0, prev[6:7, :],
                            jnp.where(row == 1, prev[7:8, :], pltpu.roll(u, 2, 0)))
            u_ref[0, :, dv] = u[tm - SUBLANES:, :]
        else:
            um2 = pa_ref[:, dv]
            um1 = pb_ref[:, dv]
            u_ref[:, dv] = u
        z = (cbias_ref[:, dv] + um2 * cw_ref[0:1, dv] + um1 * cw_ref[1:2, dv]
             + u * cw_ref[2:3, dv])
        y_b = cb_ref[:, dv].astype(F32) * z
        r = r_ref[:, dv].astype(F32)
        on = _rms(o_ref[:, dv], gg_ref[:, dv])
        m = (jax.nn.sigmoid(gb_ref[:, dv].astype(F32)) * y_b
             + jax.nn.sigmoid(ga_ref[:, dv].astype(F32)) * (r * jax.nn.sigmoid(r) * on))
        m_ref[:, dv] = m.astype(BF16)
    xm = x_ref[...] + jnp.dot(m_ref[...], wo_ref[...], preferred_element_type=F32)
    xm_ref[...] = xm
    h2 = _rms(xm, g2_ref[...])
    h2t_ref[...] = jnp.transpose(h2).astype(BF16)
    qp_ref[...] = jnp.dot(h2.astype(BF16), wq_ref[...], preferred_element_type=F32)


def _mix(o, p32, p16, pa, pb, x, wts, tm, prompt, seq):
    t = o.shape[0]
    n_tiles = t // tm
    tile = pl.BlockSpec((tm, D_MODEL), lambda i: (i, 0))
    col = lambda c: pl.BlockSpec((tm, D_MODEL), lambda i: (i, c))
    cc_col = P_SEGMENT // D_MODEL
    if prompt:
        prev_rows = tm // SUBLANES
        prev = lambda c: pl.BlockSpec(
            (SUBLANES, D_MODEL), lambda i: (jnp.maximum(i * prev_rows - 1, 0), c))
        pspecs = [prev(cc_col), prev(cc_col + 1)]
        uspec = pl.BlockSpec((1, SUBLANES, D_MODEL), lambda i: (i, 0, 0))
        ushape = jax.ShapeDtypeStruct((n_tiles, SUBLANES, D_MODEL), F32)
    else:
        pspecs = [tile, tile]
        uspec = tile
        ushape = jax.ShapeDtypeStruct((t, D_MODEL), F32)
    vec = lambda r: pl.BlockSpec((r, D_MODEL), lambda i: (0, 0))
    whole = pl.BlockSpec((D_MODEL, D_MODEL), lambda i: (0, 0),
                         pipeline_mode=pl.Buffered(1))
    return pl.pallas_call(
        functools.partial(_mix_kernel, prompt=prompt, tiles_per_seq=max(seq // tm, 1)),
        grid=(n_tiles,),
        in_specs=[tile, col(0), col(1), col(cc_col), col(cc_col + 1), col(2), col(3)]
                 + pspecs + [vec(1), vec(CONV_WIDTH), vec(1), tile, whole, vec(1), whole],
        out_specs=[tile, pl.BlockSpec((D_MODEL, tm), lambda i: (0, i)), tile, uspec],
        out_shape=[jax.ShapeDtypeStruct((t, D_MODEL), F32),
                   jax.ShapeDtypeStruct((D_MODEL, t), BF16),
                   jax.ShapeDtypeStruct((t, D_MODEL), F32),
                   ushape],
        scratch_shapes=[pltpu.VMEM((tm, D_MODEL), BF16)],
        compiler_params=_params(("arbitrary",)),
        name="mix_prompt" if prompt else "mix_sample",
    )(o, p16, p16, p32, p32, p16, p16, pa, pb, wts["gg"], wts["cw"], wts["cbias"],
      x, wts["wo"], wts["g2"], wts["wq"])


def _top_values(work, n, with_rank=False):
    vals = []
    rank = jnp.full(work.shape, float(n), F32) if with_rank else None
    for r in range(n):
        mx = jnp.max(work, axis=0, keepdims=True)
        hit = work == mx
        vals.append(mx)
        if with_rank:
            rank = jnp.where(hit, float(r), rank)
        work = jnp.where(hit, NEG_INF, work)
    return vals, rank


def _route_kernel(qp_ref, keys_ref, n1_ref, p1_ref, rank2_ref, p2_ref):
    for h in range(PEER_HEADS):
        st = []
        for side in range(2):
            c0 = (2 * h + side) * DQ_HALF
            st.append(lax.dot_general(keys_ref[side], qp_ref[:, c0:c0 + DQ_HALF],
                                      (((1,), (1,)), ((), ())),
                                      precision=lax.Precision.HIGHEST,
                                      preferred_element_type=F32))
        v1, _ = _top_values(st[0], PEER_TOPK)
        v2, rank2 = _top_values(st[1], PEER_TOPK, with_rank=True)
        v2 = jnp.concatenate(v2, axis=0)
        tm = v2.shape[1]
        bidx = lax.broadcasted_iota(jnp.int32, (SUBLANES, tm), 0)
        cand = []
        for a in range(PEER_TOPK // 2):
            nb = PEER_TOPK // (a + 1)
            if nb >= SUBLANES:
                cand.append(v1[a] + v2[0:nb, :])
            else:
                cand.append(jnp.where(bidx < nb, v1[a] + v2[0:SUBLANES, :], NEG_INF))
        tail = jnp.concatenate(v1[PEER_TOPK // 2:], axis=0) + v2[0:1, :]
        cand.append(tail)
        thr = _top_values(jnp.concatenate(cand, axis=0), PEER_TOPK)[0][-1]
        m1 = v1[0]
        m2 = v2[0:1, :]
        z = jnp.zeros_like(m1)
        n1 = jnp.zeros_like(st[0])
        for a, ca in enumerate(cand):
            keep = ca >= thr
            z = z + jnp.sum(jnp.where(keep, jnp.exp(ca - (m1 + m2)), 0.0),
                            axis=0, keepdims=True)
            kept = jnp.where(keep, 1.0, 0.0)
            if a < PEER_TOPK // 2:
                n1 = jnp.where(st[0] == v1[a], jnp.sum(kept, axis=0, keepdims=True), n1)
            else:
                for r in range(PEER_TOPK // 2):
                    n1 = jnp.where(st[0] == v1[a + r], kept[r:r + 1, :], n1)
        n1_ref[h] = n1
        p1_ref[h] = jnp.exp(st[0] - m1)
        rank2_ref[h] = rank2.astype(BF16)
        p2_ref[h] = (INV_SQRT2 * jnp.exp(st[1] - m2) / z).astype(BF16)


def _route(qp, keys, tm):
    t = qp.shape[0]
    big = pl.BlockSpec((PEER_HEADS, N_KEYS, tm), lambda i: (0, 0, i))
    f32s = jax.ShapeDtypeStruct((PEER_HEADS, N_KEYS, t), F32)
    bf16s = jax.ShapeDtypeStruct((PEER_HEADS, N_KEYS, t), BF16)
    return pl.pallas_call(
        _route_kernel,
        grid=(t // tm,),
        in_specs=[pl.BlockSpec((tm, D_MODEL), lambda i: (i, 0)),
                  pl.BlockSpec((2, N_KEYS, DQ_HALF), lambda i: (0, 0, 0))],
        out_specs=[big, big, big, big],
        out_shape=[f32s, f32s, bf16s, bf16s],
        compiler_params=_params(("arbitrary",)),
        name="route",
    )(qp, keys)


def _row_bf16(ref, h, e1, tm):
    row = jnp.broadcast_to(ref[h, pl.ds(e1, 1), :], (BF16_ROWS, tm)).astype(BF16)
    return jnp.concatenate([row] * (N_KEYS // BF16_ROWS), axis=0)


def _peer_kernel(h2t_ref, u_ref, v_ref, n1_ref, p1_ref, rank2_ref, p2_ref,
                 xm_ref, gf_ref, out_ref, *, te):
    j = pl.program_id(1)

    @pl.when(j == 0)
    def _():
        out_ref[...] = jnp.zeros_like(out_ref)

    tm = h2t_ref.shape[1]
    rows_per_part = PEER_PART_ROWS
    acc = None
    for part in range(te // rows_per_part):
        lo = part * rows_per_part
        a = jnp.dot(u_ref[lo:lo + rows_per_part, :], h2t_ref[...],
                    preferred_element_type=F32)
        blocks = []
        for l in range(rows_per_part // N_KEYS):
            e1 = j * (te // N_KEYS) + part * (rows_per_part // N_KEYS) + l
            g = jnp.zeros((N_KEYS, tm), BF16)
            for h in range(PEER_HEADS):
                keep = rank2_ref[h] < _row_bf16(n1_ref, h, e1, tm)
                g = g + jnp.where(keep, p2_ref[h], 0.0) * _row_bf16(p1_ref, h, e1, tm)
            al = a[l * N_KEYS:(l + 1) * N_KEYS, :]
            gelu_s = al * (1.0 + lax.erf(al))
            blocks.append(gelu_s.astype(BF16) * g)
        wt = jnp.concatenate(blocks, axis=0) if len(blocks) > 1 else blocks[0]
        d = jnp.dot(jnp.transpose(wt), v_ref[lo:lo + rows_per_part, :],
                    preferred_element_type=F32)
        acc = d if acc is None else acc + d
    out_ref[...] += acc

    @pl.when(j == pl.num_programs(1) - 1)
    def _():
        out_ref[...] = _rms(xm_ref[...] + out_ref[...], gf_ref[...])


def _peer(h2t, u_bf, v_bf, n1, p1, rank2, p2, xm, gf, tm, te):
    t = xm.shape[0]
    big = pl.BlockSpec((PEER_HEADS, N_KEYS, tm), lambda i, j: (0, 0, i))
    return pl.pallas_call(
        functools.partial(_peer_kernel, te=te),
        grid=(t // tm, N_EXPERTS // te),
        in_specs=[pl.BlockSpec((D_MODEL, tm), lambda i, j: (0, i)),
                  pl.BlockSpec((te, D_MODEL), lambda i, j: (j, 0)),
                  pl.BlockSpec((te, D_MODEL), lambda i, j: (j, 0)),
                  big, big, big, big,
                  pl.BlockSpec((tm, D_MODEL), lambda i, j: (i, 0)),
                  pl.BlockSpec((1, D_MODEL), lambda i, j: (0, 0))],
        out_specs=pl.BlockSpec((tm, D_MODEL), lambda i, j: (i, 0)),
        out_shape=jax.ShapeDtypeStruct((t, D_MODEL), F32),
        compiler_params=_params(("arbitrary", "arbitrary")),
        name="peer",
    )(h2t, u_bf, v_bf, n1, p1, rank2, p2, xm, gf)


def _channel_mixer(o, p32, p16, pa, pb, x, wts, tm_mix, tm_route, tm_peer, te, prompt, seq):
    xm, h2t, qp, utail = _mix(o, p32, p16, pa, pb, x, wts, tm_mix, prompt, seq)
    n1, p1, rank2, p2 = _route(qp, wts["keys"], tm_route)
    y = _peer(h2t, wts["u"], wts["v"], n1, p1, rank2, p2, xm, wts["gf"], tm_peer, te)
    return y, utail


def kernel(x_prompt, x_sample, state_gla, state_conv, norm1_g, w_in, w_alpha_up, b_alpha,
           gla_norm_g, conv_w, conv_b, w_out, norm2_g, w_query, sub_keys, expert_u, expert_v,
           norm_f_g):
    batch, seq, _ = x_prompt.shape
    n_dec = x_sample.shape[0]
    assert w_in.shape[0] == 1 and x_sample.shape[1] == 1

    alr0 = 2 * GLA_HEADS * GLA_DK + 2 * GLA_HEADS * GLA_DV
    w_bf = w_in[0].astype(BF16)
    w_lo = w_bf
    w_hi = w_bf[:, alr0 + GLA_RANK:]
    w_alr = jnp.pad(w_bf[:, alr0:alr0 + GLA_RANK], ((0, 0), (0, LANES - GLA_RANK)))
    wup = jnp.pad(w_alpha_up[0], ((0, LANES - GLA_RANK), (0, 0))).astype(BF16)
    ba = b_alpha[0][None, :]
    g1 = norm1_g[0][None, :]
    wts = dict(gg=gla_norm_g[0][None, :], cw=conv_w[0], cbias=conv_b[0][None, :],
               g2=norm2_g[0][None, :], keys=sub_keys[0],
               gf=norm_f_g[None, :])

    xp = x_prompt.reshape(batch * seq, D_MODEL)
    p32, p16, alr, wts["u"], wts["v"], wts["wo"], wts["wq"] = _inproj(
        xp, g1, w_lo, w_hi, w_alr, tm=TM_INPROJ, tn=TN_INPROJ,
        tables=(expert_u[0], expert_v[0], w_out[0], w_query[0]),
        table_scales=(INV_SQRT2, 1.0, 1.0, 1.0))
    o, s_p = _gla_prompt(p32, alr, wup, ba, batch, seq, rows=GLA_ROWS)
    y_p, utail = _channel_mixer(o, p32, p16, p32, p32, xp, wts, tm_mix=TM_MIX,
                                tm_route=TM_ROUTE, tm_peer=TM_PEER, te=TE_PEER,
                                prompt=True, seq=seq)
    last = utail.reshape(batch, seq // TM_MIX, SUBLANES, D_MODEL)[:, -1, SUBLANES - 2:, :]

    xs = x_sample.reshape(n_dec, D_MODEL)
    ps32, ps16, alrs = _inproj(xs, g1, w_lo, w_hi, w_alr, tm=n_dec, tn=TN_INPROJ)
    os_, s_s = _gla_sample(ps32.reshape(n_dec, 1, N_MAIN // 2),
                           alrs.reshape(n_dec, 1, LANES), wup, ba, state_gla[0],
                           nb=GLA_SAMPLE_SEQS)
    y_s, us = _channel_mixer(os_.reshape(n_dec, D_MODEL), ps32, ps16, state_conv[0][:, 0, :],
                             state_conv[0][:, 1, :], xs, wts, tm_mix=n_dec, tm_route=n_dec,
                             tm_peer=n_dec, te=TE_PEER, prompt=False, seq=1)
    conv_s = jnp.stack([state_conv[0][:, 1, :], us], axis=1)

    return (y_p.reshape(batch, seq, D_MODEL),
            y_s.reshape(n_dec, 1, D_MODEL),
            s_p[None],
            last[None],
            s_s[None],
            conv_s[None])
```
